```python
import math
import jax, jax.numpy as jnp
from jax import lax
import numpy as np

D_MODEL = 1024
BATCH = 4
SEQ = 4096
DEPTH = 4
DEC_BATCH = 128
DEC_SEQ = 1
PAST_LEN = 2048
PAGE_SIZE = 128

D_MIX = D_MODEL
D_BR = D_MIX // 4
HD = 64
H_A = D_BR // HD
H_B = D_BR // HD
H_C = D_BR // HD
CHUNK_A = 64
LB_FLOOR = 1e-30
LORA_W = 32
LORA_A = 32
RWKV_GN_EPS = 64e-5
C_CONFIGS = ((128, 1), (512, 4), (2048, 16))
C_WIN_MAX = 2048
Q_BLOCK = 128
MASK_VALUE = -1e30
ROPE_THETA = 10000.0
POOL_WINDOWS = (2, 4, 8, 16)
POOL_MAX = 16
D_POOL_G = D_BR // len(POOL_WINDOWS)
EPS = 1e-6
A_W = 4 * D_BR
B_SHIFT_W = 3 * D_BR + LORA_W + LORA_A
B_W = B_SHIFT_W + D_BR
C_W = 4 * D_BR
D_W = 2 * D_BR
D_IN = A_W + B_W + C_W + D_W

kernel_name = "hymba_hgrn2_rwkv7_dilated_pool_decoder_step"

F32 = jnp.float32


def rms_norm(x, g, eps=EPS):
    xf = x.astype(F32)
    y = xf * lax.rsqrt(jnp.mean(xf * xf, axis=-1, keepdims=True) + eps)
    return (y * g.astype(F32)).astype(x.dtype)


def rope(x, pos):
    half = x.shape[-1] // 2
    inv_freq = jnp.float32(ROPE_THETA) ** (-jnp.arange(half, dtype=F32) / half)
    ang = pos.astype(F32)[:, None] * inv_freq[None, :]
    cos = jnp.cos(ang)[None, :, None, :]
    sin = jnp.sin(ang)[None, :, None, :]
    xf = x.astype(F32)
    x1, x2 = xf[..., :half], xf[..., half:]
    return jnp.concatenate([x1 * cos - x2 * sin, x2 * cos + x1 * sin], axis=-1).astype(x.dtype)


def hgrn2_scan(q, logf, k, v, s0):
    B, T, H, DK = q.shape
    C = math.gcd(T, CHUNK_A)
    n = T // C

    def to_chunks(a):
        return a.reshape(B, n, C, H, a.shape[-1]).transpose(1, 0, 3, 2, 4)

    xs = tuple(to_chunks(a) for a in (q, logf, k, v))
    causal = jnp.tril(jnp.ones((C, C), dtype=bool))[:, :, None]

    def step(s, inp):
        qb, gb, kb, vb = inp
        G = jnp.cumsum(gb, axis=2)
        o_inter = jnp.einsum('bhtk,bhkv->bhtv', qb * jnp.exp(G), s)
        diff = G[:, :, :, None, :] - G[:, :, None, :, :]
        decay = jnp.where(causal, jnp.exp(jnp.where(causal, diff, 0.0)), 0.0)
        att = jnp.einsum('bhtk,bhsk,bhtsk->bhts', qb, kb, decay)
        o_intra = jnp.einsum('bhts,bhsv->bhtv', att, vb)
        G_last = G[:, :, -1:, :]
        s_new = jnp.exp(G_last[:, :, 0, :])[..., None] * s + jnp.einsum(
            'bhsk,bhsv->bhkv', kb * jnp.exp(G_last - G), vb)
        return s_new, o_inter + o_intra

    s_fin, o = lax.scan(step, s0, xs)
    o = o.transpose(1, 0, 3, 2, 4).reshape(B, T, H, v.shape[-1])
    return o, s_fin


def rwkv7_scan(r, logw, k, v, a_vec, b_vec, s0):
    def step(s, inp):
        rt, wt, kt, vt, at, bt = inp
        sa = jnp.einsum('bhij,bhj->bhi', s, at)
        s = (s * jnp.exp(wt)[:, :, None, :] + sa[..., None] * bt[:, :, None, :]
             + vt[..., None] * kt[:, :, None, :])
        return s, jnp.einsum('bhij,bhj->bhi', s, rt)

    xs = tuple(jnp.moveaxis(a, 1, 0) for a in (r, logw, k, v, a_vec, b_vec))
    s_fin, y = lax.scan(step, s0, xs)
    return jnp.moveaxis(y, 0, 1), s_fin


def dilated_attention(q, k_ext, v_ext):
    B, T, H, Dh = q.shape
    P = k_ext.shape[1] - T
    QB = math.gcd(T, Q_BLOCK)
    nb = T // QB
    scale = Dh ** -0.5
    q_blocks = q.reshape(B, nb, QB, H, Dh).transpose(1, 0, 2, 3, 4)

    def block(args):
        qblk, bi = args
        e = P + bi * QB + jnp.arange(QB)
        outs, lses = [], []
        for (w, d) in C_CONFIGS:
            j = jnp.arange(w // d + 1)
            idx = e[:, None] - j[None, :] * d
            valid = idx >= 0
            idxc = jnp.maximum(idx, 0)
            kg = k_ext[:, idxc]
            vg = v_ext[:, idxc]
            s = jnp.einsum('bqhd,bqnhd->bhqn', qblk, kg, preferred_element_type=F32) * scale
            s = jnp.where(valid[None, None], s, MASK_VALUE)
            m = jnp.max(s, axis=-1, keepdims=True)
            p = jnp.exp(s - m)
            den = jnp.sum(p, axis=-1)
            o = jnp.einsum('bhqn,bqnhd->bhqd', p, vg.astype(F32)) / den[..., None]
            outs.append(o)
            lses.append(m[..., 0] + jnp.log(den))
        wts = jax.nn.softmax(jnp.stack(lses, axis=0), axis=0)
        o = jnp.sum(wts[..., None] * jnp.stack(outs, axis=0), axis=0)
        return o.transpose(0, 2, 1, 3)

    out = lax.map(block, (q_blocks, jnp.arange(nb)))
    return out.transpose(1, 0, 2, 3, 4).reshape(B, T, H, Dh).astype(q.dtype)


def multiscale_pool(u_ext, pos0, T):
    P = u_ext.shape[1] - T
    cs = jnp.cumsum(jnp.pad(u_ext.astype(F32), ((0, 0), (POOL_MAX, 0), (0, 0))), axis=1)
    e = P + jnp.arange(T)
    pos = pos0 + e
    hi = cs[:, POOL_MAX + e]
    outs = []
    for g, w in enumerate(POOL_WINDOWS):
        sl = slice(g * D_POOL_G, (g + 1) * D_POOL_G)
        s = hi[..., sl] - cs[:, POOL_MAX + e - w, sl]
        cnt = jnp.minimum(w, pos + 1).astype(F32)
        outs.append(s / cnt[None, :, None])
    return jnp.concatenate(outs, axis=-1) - u_ext[:, P:].astype(F32)


def mixer_hgrn2(pa, s0, lb, onorm_g):
    B, T, _ = pa.shape
    q, fl, i, g = jnp.split(pa.astype(F32), 4, axis=-1)
    q = jax.nn.silu(q)
    logf = jnp.logaddexp(jax.nn.log_sigmoid(fl),
                         jnp.log(jnp.maximum(lb, LB_FLOOR)) + jax.nn.log_sigmoid(-fl))
    k = jnp.exp(jnp.log1p(-lb) + jax.nn.log_sigmoid(-fl))
    heads = lambda a: a.reshape(B, T, H_A, HD)
    o, s_fin = hgrn2_scan(heads(q), heads(logf), heads(k), heads(i), s0.astype(F32))
    o = rms_norm(o, onorm_g.reshape(H_A, HD)).reshape(B, T, D_BR)
    return o * jax.nn.silu(g), s_fin


def mixer_rwkv7(pb, shift0, s0, mu, w0, w2, a0, a2, k_k, k_a, r_k, gn_w, gn_b):
    B, T, _ = pb.shape
    pbf = pb.astype(F32)
    xs, g = pbf[..., :B_SHIFT_W], pbf[..., B_SHIFT_W:]
    prev = jnp.concatenate([shift0[:, None, :].astype(F32), xs[:, :-1]], axis=1)
    xm = xs + (prev - xs) * mu
    r, k, v, xw, xa = jnp.split(xm, [D_BR, 2 * D_BR, 3 * D_BR, 3 * D_BR + LORA_W], axis=-1)
    w = -jax.nn.softplus(-(w0 + jnp.tanh(xw) @ w2)) - 0.5
    logw = -jnp.exp(w)
    a = jax.nn.sigmoid(a0 + xa @ a2)
    heads = lambda t: t.reshape(B, T, H_B, HD)
    kk = heads(k * k_k)
    kk = kk / jnp.maximum(jnp.sqrt(jnp.sum(kk * kk, axis=-1, keepdims=True)), 1e-12)
    k = k * (1.0 + (a - 1.0) * k_a)
    rh, kh, vh, ah = heads(r), heads(k), heads(v), heads(a)
    y, s_fin = rwkv7_scan(rh, heads(logw), kh, vh, -kk, kk * ah, s0.astype(F32))
    mean = jnp.mean(y, axis=-1, keepdims=True)
    var = jnp.mean(jnp.square(y - mean), axis=-1, keepdims=True)
    y = ((y - mean) * lax.rsqrt(var + RWKV_GN_EPS)).reshape(B, T, D_BR) * gn_w + gn_b
    bonus = jnp.sum(rh * kh * r_k.reshape(H_B, HD), axis=-1, keepdims=True) * vh
    out = (y + bonus.reshape(B, T, D_BR)) * jax.nn.silu(g)
    return out, s_fin, xs[:, -1]


def mixer_dilated(pc, kbuf, vbuf, pos, qn_g, kn_g):
    B, T, _ = pc.shape
    q, k, v, g = jnp.split(pc, 4, axis=-1)
    heads = lambda t: t.reshape(B, T, H_C, HD)
    q = rope(rms_norm(heads(q), qn_g), pos)
    k = rope(rms_norm(heads(k), kn_g), pos)
    v = heads(v)
    k_ext = jnp.concatenate([kbuf.astype(k.dtype), k], axis=1)
    v_ext = jnp.concatenate([vbuf.astype(v.dtype), v], axis=1)
    o = dilated_attention(q, k_ext, v_ext).reshape(B, T, D_BR)
    return o.astype(F32) * jax.nn.silu(g.astype(F32)), k, v


def mixer_pool(pd, pbuf, t0, w_pool, p_scale):
    B, T, _ = pd.shape
    u, g = jnp.split(pd, 2, axis=-1)
    u_ext = jnp.concatenate([pbuf.astype(u.dtype), u], axis=1)
    pooled = multiscale_pool(u_ext, t0 - pbuf.shape[1], T)
    mixed = jnp.einsum('btgc,gcd->btgd', pooled.reshape(B, T, len(POOL_WINDOWS), D_POOL_G),
                       w_pool.astype(F32)).reshape(B, T, D_BR) * p_scale
    return mixed * jax.nn.silu(g.astype(F32)), u


def run_group(x, c, t0, sA, sBw, sBs, ck, cv, cd,
              ada_w, ada_b, norm_g, w_in, w_out, a_lb_logits, a_onorm_g,
              b_mu, b_w0, b_w2, b_a0, b_a2, b_k_k, b_k_a, b_r_k, b_gn_w, b_gn_b,
              c_qnorm_g, c_knorm_g, d_w_pool, d_scale):
    B, T, _ = x.shape
    pos = t0 + jnp.arange(T)
    lb_sm = jax.nn.softmax(a_lb_logits.astype(F32), axis=0)
    lower_bounds = jnp.cumsum(lb_sm, axis=0) - lb_sm[0:1]
    c_keep = min(C_WIN_MAX, T)
    d_keep = min(POOL_MAX - 1, T)
    nA, nBw, nBs, nK, nV, nD = [], [], [], [], [], []
    for l in range(DEPTH):
        mod = jax.nn.silu(c) @ ada_w[l] + ada_b[l]
        shift, scale, gate = jnp.split(mod, 3, axis=-1)
        h = rms_norm(x, norm_g[l]) * (1.0 + scale[:, None, :]) + shift[:, None, :]
        proj = h @ w_in[l]
        pa, pb, pc, pd = jnp.split(proj, [A_W, A_W + B_W, A_W + B_W + C_W], axis=-1)
        oA, sA_new = mixer_hgrn2(pa, sA[l], lower_bounds[l], a_onorm_g[l])
        oB, sBw_new, sBs_new = mixer_rwkv7(pb, sBs[l], sBw[l], b_mu[l], b_w0[l], b_w2[l], b_a0[l],
                                           b_a2[l], b_k_k[l], b_k_a[l], b_r_k[l], b_gn_w[l], b_gn_b[l])
        oC, k_new, v_new = mixer_dilated(pc, ck[l], cv[l], pos, c_qnorm_g[l], c_knorm_g[l])
        oD, u_new = mixer_pool(pd, cd[l], t0, d_w_pool[l], d_scale[l])
        mix = jnp.concatenate([oA, oB, oC, oD], axis=-1).astype(x.dtype)
        x = x + gate[:, None, :] * (mix @ w_out[l])
        nA.append(sA_new.astype(sA.dtype))
        nBw.append(sBw_new.astype(sBw.dtype))
        nBs.append(sBs_new.astype(sBs.dtype))
        nK.append(k_new[:, T - c_keep:].astype(ck.dtype))
        nV.append(v_new[:, T - c_keep:].astype(cv.dtype))
        nD.append(u_new[:, T - d_keep:].astype(cd.dtype))
    return (x, jnp.stack(nA), jnp.stack(nBw), jnp.stack(nBs),
            jnp.stack(nK), jnp.stack(nV), jnp.stack(nD))


def setup_inputs(seed: int = 0) -> dict:
    key = jax.random.key(seed)
    ks = jax.random.split(key, 40)
    nrm = lambda k, s, sc=1.0: sc * jax.random.normal(k, s, dtype=F32)
    c_buf = min(C_WIN_MAX, PAST_LEN)
    return {
        "x_prompt": nrm(ks[0], (BATCH, SEQ, D_MODEL)),
        "x_sample": nrm(ks[1], (DEC_BATCH, DEC_SEQ, D_MODEL)),
        "state_A": nrm(ks[2], (DEPTH, DEC_BATCH, H_A, HD, HD), 0.3),
        "state_B_wkv": nrm(ks[3], (DEPTH, DEC_BATCH, H_B, HD, HD), 0.3),
        "state_B_shift": nrm(ks[4], (DEPTH, DEC_BATCH, B_SHIFT_W)),
        "cache_C_k": nrm(ks[5], (DEPTH, DEC_BATCH, c_buf, H_C, HD)),
        "cache_C_v": nrm(ks[6], (DEPTH, DEC_BATCH, c_buf, H_C, HD)),
        "cache_D_pool": nrm(ks[7], (DEPTH, DEC_BATCH, POOL_MAX - 1, D_BR)),
        "c_prompt": nrm(ks[8], (BATCH, D_MODEL)),
        "c_sample": nrm(ks[9], (DEC_BATCH, D_MODEL)),
        "ada_w": nrm(ks[10], (DEPTH, D_MODEL, 3 * D_MODEL), D_MODEL ** -0.5),
        "ada_b": nrm(ks[11], (DEPTH, 3 * D_MODEL), 0.02),
        "norm_g": 1.0 + nrm(ks[12], (DEPTH, D_MODEL), 0.02),
        "w_in": nrm(ks[13], (DEPTH, D_MODEL, D_IN), D_MODEL ** -0.5),
        "w_out": nrm(ks[14], (DEPTH, D_MIX, D_MODEL), D_MIX ** -0.5),
        "a_lb_logits": nrm(ks[15], (DEPTH, D_BR), 0.5),
        "a_onorm_g": 1.0 + nrm(ks[16], (DEPTH, D_BR), 0.02),
        "b_mu": jax.random.uniform(ks[17], (DEPTH, B_SHIFT_W), dtype=F32),
        "b_w0": jax.random.uniform(ks[18], (DEPTH, D_BR), dtype=F32, minval=-4.0, maxval=1.0),
        "b_w2": nrm(ks[19], (DEPTH, LORA_W, D_BR), 0.1),
        "b_a0": nrm(ks[20], (DEPTH, D_BR), 0.1),
        "b_a2": nrm(ks[21], (DEPTH, LORA_A, D_BR), 0.1),
        "b_k_k": 0.85 + nrm(ks[22], (DEPTH, D_BR), 0.05),
        "b_k_a": 1.0 + nrm(ks[23], (DEPTH, D_BR), 0.05),
        "b_r_k": nrm(ks[24], (DEPTH, D_BR), 0.1),
        "b_gn_w": 1.0 + nrm(ks[25], (DEPTH, D_BR), 0.02),
        "b_gn_b": nrm(ks[26], (DEPTH, D_BR), 0.02),
        "c_qnorm_g": 1.0 + nrm(ks[27], (DEPTH, HD), 0.02),
        "c_knorm_g": 1.0 + nrm(ks[28], (DEPTH, HD), 0.02),
        "d_w_pool": nrm(ks[29], (DEPTH, len(POOL_WINDOWS), D_POOL_G, D_POOL_G), D_POOL_G ** -0.5),
        "d_scale": 1.0 + nrm(ks[30], (DEPTH, D_BR), 0.1),
    }


def reference(x_prompt, x_sample, state_A, state_B_wkv, state_B_shift, cache_C_k, cache_C_v,
              cache_D_pool, c_prompt, c_sample, ada_w, ada_b, norm_g, w_in, w_out,
              a_lb_logits, a_onorm_g, b_mu, b_w0, b_w2, b_a0, b_a2, b_k_k, b_k_a, b_r_k,
              b_gn_w, b_gn_b, c_qnorm_g, c_knorm_g, d_w_pool, d_scale):
    weights = (ada_w, ada_b, norm_g, w_in, w_out, a_lb_logits, a_onorm_g,
               b_mu, b_w0, b_w2, b_a0, b_a2, b_k_k, b_k_a, b_r_k, b_gn_w, b_gn_b,
               c_qnorm_g, c_knorm_g, d_w_pool, d_scale)
    B = x_prompt.shape[0]
    p_sA = jnp.zeros((DEPTH, B, H_A, HD, HD), state_A.dtype)
    p_sBw = jnp.zeros((DEPTH, B, H_B, HD, HD), state_B_wkv.dtype)
    p_sBs = jnp.zeros((DEPTH, B, B_SHIFT_W), state_B_shift.dtype)
    p_ck = jnp.zeros((DEPTH, B, 0, H_C, HD), cache_C_k.dtype)
    p_cv = jnp.zeros((DEPTH, B, 0, H_C, HD), cache_C_v.dtype)
    p_cd = jnp.zeros((DEPTH, B, 0, D_BR), cache_D_pool.dtype)
    y_prompt, pA, pBw, pBs, pK, pV, pD = run_group(
        x_prompt, c_prompt, 0, p_sA, p_sBw, p_sBs, p_ck, p_cv, p_cd, *weights)
    y_sample, sA, sBw, sBs, sK, sV, sD = run_group(
        x_sample, c_sample, PAST_LEN, state_A, state_B_wkv, state_B_shift,
        cache_C_k, cache_C_v, cache_D_pool, *weights)
    return (y_prompt, y_sample, pA, sA, pBw, sBw, pBs, sBs, pK, sK, pV, sV, pD, sD)
```

```python
import functools

import jax
import jax.numpy as jnp
from jax import lax
from jax.experimental import pallas as pl
from jax.experimental.pallas import tpu as pltpu

F32 = jnp.float32
BF16 = jnp.bfloat16

D_MODEL = 1024
DEPTH = 4
PAST_LEN = 2048
D_BR = 256
HD = 64
NH = D_BR // HD
LB_FLOOR = 1e-30
LORA = 32
RWKV_GN_EPS = 64e-5
C_CONFIGS = ((128, 1), (512, 4), (2048, 16))
C_WIN_MAX = 2048
MASK_VALUE = -1e30
ROPE_THETA = 10000.0
POOL_WINDOWS = (2, 4, 8, 16)
POOL_MAX = 16
EPS = 1e-6
A_W = 4 * D_BR
B_SHIFT_W = 3 * D_BR + 2 * LORA
B_W = B_SHIFT_W + D_BR
C_W = 4 * D_BR
D_W = 2 * D_BR

ROW_TILE = 512
MIX_TILE = 256
CHUNK_A = 16
CHUNK_B = 64
INV_BLOCK = 16
Q_BLOCK = 128
STEP_ROWS = 8
ATTN_STEP_ROWS = 2
POOL_STEP_ROWS = 32
VMEM_LIMIT = 56 * 1024 * 1024


def _cparams(*sem):
    return pltpu.CompilerParams(dimension_semantics=sem, vmem_limit_bytes=VMEM_LIMIT)


def _sigmoid(x):
    return 1.0 / (1.0 + jnp.exp(-x))


def _silu(x):
    return x * _sigmoid(x)


def _softplus(x):
    return jnp.maximum(x, 0.0) + jnp.log1p(jnp.exp(-jnp.abs(x)))


def _log_sigmoid(x):
    return -_softplus(-x)


def _logaddexp(a, b):
    return jnp.maximum(a, b) + jnp.log1p(jnp.exp(-jnp.abs(a - b)))


def _bdot(a, b):
    return jnp.dot(a.astype(BF16), b.astype(BF16), preferred_element_type=F32)


def _bdot_nt(a, b):
    return lax.dot_general(a.astype(BF16), b.astype(BF16), (((1,), (1,)), ((), ())),
                           preferred_element_type=F32)


def _bdot_tn(a, b):
    return lax.dot_general(a.astype(BF16), b.astype(BF16), (((0,), (0,)), ((), ())),
                           preferred_element_type=F32)


def _split3(x):
    h1 = x.astype(BF16)
    r1 = x - h1.astype(F32)
    h2 = r1.astype(BF16)
    r2 = r1 - h2.astype(F32)
    return h1, h2, r2.astype(BF16)


def _seg_sum(x):
    n = x.shape[-1]
    r = lax.broadcasted_iota(jnp.int32, (n, n), 0) // HD
    c = lax.broadcasted_iota(jnp.int32, (n, n), 1) // HD
    ones = jnp.where(r == c, 1.0, 0.0).astype(BF16)
    h1, h2, h3 = _split3(x)
    d = lambda h: jnp.dot(h, ones, preferred_element_type=F32)
    return d(h1) + d(h2) + d(h3)


def _chunk_cumsum(x, chunk):
    t = x.shape[0]
    r = lax.broadcasted_iota(jnp.int32, (t, t), 0)
    c = lax.broadcasted_iota(jnp.int32, (t, t), 1)
    tri = jnp.where((r // chunk == c // chunk) & (c <= r), 1.0, 0.0).astype(BF16)
    h1, h2, h3 = _split3(x)
    d = lambda h: jnp.dot(tri, h, preferred_element_type=F32)
    return d(h1) + d(h2) + d(h3)


def _head_rms(x, gamma):
    ms = _seg_sum(x * x) * (1.0 / HD)
    return x * lax.rsqrt(ms + EPS) * gamma


def _rope(x, cos, sin_signed):
    lane = lax.broadcasted_iota(jnp.int32, x.shape, 1) % HD
    partner = jnp.where(lane < HD // 2,
                        pltpu.roll(x, D_BR - HD // 2, 1),
                        pltpu.roll(x, HD // 2, 1))
    return x * cos + partner * sin_signed


def _eye(n):
    r = lax.broadcasted_iota(jnp.int32, (n, n), 0)
    c = lax.broadcasted_iota(jnp.int32, (n, n), 1)
    return jnp.where(r == c, 1.0, 0.0).astype(F32)


def _to_col(row, eye):
    return jnp.sum(eye * row, axis=1, keepdims=True)


def _to_row(col, eye):
    return jnp.sum(eye * col, axis=0, keepdims=True)


def _mod_kernel(c_ref, w_ref, b_ref, o_ref):
    c = _silu(c_ref[...])
    o_ref[0] = _bdot(c, w_ref[0]) + b_ref[0]


def _modulation(c_all, ada_w, ada_b):
    n = c_all.shape[0]
    tn = 512
    return pl.pallas_call(
        _mod_kernel,
        grid=(DEPTH, 3 * D_MODEL // tn),
        in_specs=[pl.BlockSpec((n, D_MODEL), lambda l, j: (0, 0)),
                  pl.BlockSpec((1, D_MODEL, tn), lambda l, j: (l, 0, j)),
                  pl.BlockSpec((1, 1, tn), lambda l, j: (l, 0, j))],
        out_specs=pl.BlockSpec((1, n, tn), lambda l, j: (l, 0, j)),
        out_shape=jax.ShapeDtypeStruct((DEPTH, n, 3 * D_MODEL), F32),
        compiler_params=_cparams("parallel", "parallel"),
        name="adaln_modulation",
    )(c_all, ada_w, ada_b.reshape(DEPTH, 1, 3 * D_MODEL))


def _in_proj_kernel(x_ref, shift_ref, scale_ref, g_ref, wa_ref, wb_ref, wc_ref, wd_ref,
                    pa_ref, pb_ref, pc_ref, pd_ref):
    x = x_ref[0]
    ms = jnp.mean(x * x, axis=-1, keepdims=True)
    h = x * lax.rsqrt(ms + EPS) * g_ref[...]
    h = (h * (1.0 + scale_ref[0]) + shift_ref[0]).astype(BF16)
    pa_ref[0] = jnp.dot(h, wa_ref[...], preferred_element_type=F32)
    pb_ref[0] = jnp.dot(h, wb_ref[...], preferred_element_type=F32)
    pc_ref[0] = jnp.dot(h, wc_ref[...], preferred_element_type=F32)
    pd_ref[0] = jnp.dot(h, wd_ref[...], preferred_element_type=F32)


def _in_proj(x, shift, scale, norm_g, w_parts):
    B, T, D = x.shape
    tm = min(ROW_TILE, T)
    mt = shift.shape[1]
    mod_block = (1, tm, D) if mt == T else (1, 1, D)
    mod_map = (lambda b, i: (b, i, 0)) if mt == T else (lambda b, i: (b, 0, 0))
    widths = [w.shape[1] for w in w_parts]
    return pl.pallas_call(
        _in_proj_kernel,
        grid=(B, T // tm),
        in_specs=[pl.BlockSpec((1, tm, D), lambda b, i: (b, i, 0)),
                  pl.BlockSpec(mod_block, mod_map),
                  pl.BlockSpec(mod_block, mod_map),
                  pl.BlockSpec((1, D), lambda b, i: (0, 0))]
                 + [pl.BlockSpec((D, w), lambda b, i: (0, 0)) for w in widths],
        out_specs=[pl.BlockSpec((1, tm, w), lambda b, i: (b, i, 0)) for w in widths],
        out_shape=[jax.ShapeDtypeStruct((B, T, w), F32) for w in widths],
        compiler_params=_cparams("parallel", "parallel"),
        name="norm_in_proj",
    )(x, shift, scale, norm_g.reshape(1, D), *w_parts)


def _out_proj_kernel(oa_ref, ob_ref, oc_ref, od_ref, x_ref, gate_ref, w_ref, y_ref):
    mix = jnp.concatenate([oa_ref[0], ob_ref[0], oc_ref[0], od_ref[0]], axis=-1).astype(BF16)
    res = jnp.dot(mix, w_ref[...], preferred_element_type=F32)
    y_ref[0] = x_ref[0] + gate_ref[0] * res


def _out_proj(oa, ob, oc, od, x, gate, w_out):
    B, T, D = x.shape
    tm = min(ROW_TILE, T)
    mt = gate.shape[1]
    mod_block = (1, tm, D) if mt == T else (1, 1, D)
    mod_map = (lambda b, i: (b, i, 0)) if mt == T else (lambda b, i: (b, 0, 0))
    br = pl.BlockSpec((1, tm, D_BR), lambda b, i: (b, i, 0))
    return pl.pallas_call(
        _out_proj_kernel,
        grid=(B, T // tm),
        in_specs=[br, br, br, br,
                  pl.BlockSpec((1, tm, D), lambda b, i: (b, i, 0)),
                  pl.BlockSpec(mod_block, mod_map),
                  pl.BlockSpec((D, D), lambda b, i: (0, 0))],
        out_specs=pl.BlockSpec((1, tm, D), lambda b, i: (b, i, 0)),
        out_shape=jax.ShapeDtypeStruct((B, T, D), F32),
        compiler_params=_cparams("parallel", "parallel"),
        name="out_proj_residual",
    )(oa, ob, oc, od, x, gate, w_out)


def _hgrn2_gates(pa, log_lb, log1m_lb):
    q = _silu(pa[:, 0:D_BR])
    fl = pa[:, D_BR:2 * D_BR]
    v = pa[:, 2 * D_BR:3 * D_BR]
    ls_neg = _log_sigmoid(-fl)
    logf = _logaddexp(_log_sigmoid(fl), log_lb + ls_neg)
    k = jnp.exp(log1m_lb + ls_neg)
    return q, logf, k, v


def _hgrn2_seq_kernel(pa_ref, s0_ref, loglb_ref, log1mlb_ref, og_ref, o_ref, sfin_ref,
                      q_s, k_s, v_s, g_s, o_s, st_s, *, tb, chunk):
    step = pl.program_id(1)

    @pl.when(step == 0)
    def _():
        st_s[...] = s0_ref[0]

    q, logf, k, v = _hgrn2_gates(pa_ref[0], loglb_ref[...], log1mlb_ref[...])
    gcum = _chunk_cumsum(logf, chunk)
    for h in range(NH):
        sl = slice(h * HD, (h + 1) * HD)
        q_s[h] = q[:, sl]
        k_s[h] = k[:, sl]
        v_s[h] = v[:, sl]
        g_s[h] = gcum[:, sl]

    rows = lax.broadcasted_iota(jnp.int32, (chunk, 1), 0)

    def body(c, carry):
        r0 = pl.multiple_of(c * chunk, chunk)
        for h in range(NH):
            qc = q_s[h, pl.ds(r0, chunk), :]
            kc = k_s[h, pl.ds(r0, chunk), :]
            vc = v_s[h, pl.ds(r0, chunk), :]
            gc = g_s[h, pl.ds(r0, chunk), :]
            st = st_s[h]
            o_acc = _bdot_nt(qc * jnp.exp(gc), st)
            for t in range(chunk):
                causal = rows <= t
                d = jnp.where(causal, gc[t:t + 1, :] - gc, 0.0)
                e = jnp.where(causal, jnp.exp(d), 0.0)
                att = jnp.sum(e * kc * qc[t:t + 1, :], axis=1, keepdims=True)
                ot = jnp.sum(att * vc, axis=0, keepdims=True)
                o_acc = o_acc + jnp.where(rows == t, ot, 0.0)
            glast = gc[chunk - 1:chunk, :]
            st_s[h] = st * jnp.exp(glast) + _bdot_tn(vc, kc * jnp.exp(glast - gc))
            o_s[pl.ds(r0, chunk), h * HD:(h + 1) * HD] = o_acc
        return carry

    lax.fori_loop(0, tb // chunk, body, 0)

    o = _head_rms(o_s[...], og_ref[...])
    o_ref[0] = o * _silu(pa_ref[0][:, 3 * D_BR:4 * D_BR])

    @pl.when(step == pl.num_programs(1) - 1)
    def _():
        sfin_ref[0] = st_s[...]


def _hgrn2_seq(pa, s0_t, log_lb, log1m_lb, onorm_g):
    B, T, _ = pa.shape
    tb = min(MIX_TILE, T)
    vec = pl.BlockSpec((1, D_BR), lambda b, i: (0, 0))
    st = pl.BlockSpec((1, NH, HD, HD), lambda b, i: (b, 0, 0, 0))
    return pl.pallas_call(
        functools.partial(_hgrn2_seq_kernel, tb=tb, chunk=CHUNK_A),
        grid=(B, T // tb),
        in_specs=[pl.BlockSpec((1, tb, A_W), lambda b, i: (b, i, 0)), st, vec, vec, vec],
        out_specs=[pl.BlockSpec((1, tb, D_BR), lambda b, i: (b, i, 0)), st],
        out_shape=[jax.ShapeDtypeStruct((B, T, D_BR), F32),
                   jax.ShapeDtypeStruct((B, NH, HD, HD), F32)],
        scratch_shapes=[pltpu.VMEM((NH, tb, HD), F32)] * 4
                       + [pltpu.VMEM((tb, D_BR), F32), pltpu.VMEM((NH, HD, HD), F32)],
        compiler_params=_cparams("parallel", "arbitrary"),
        name="hgrn2_seq",
    )(pa, s0_t, log_lb, log1m_lb, onorm_g)


def _hgrn2_step_kernel(pa_ref, s_ref, loglb_ref, log1mlb_ref, og_ref, o_ref, snew_ref, o_s, *, bb):
    pa = pa_ref[...]
    q, logf, k, v = _hgrn2_gates(pa, loglb_ref[...], log1mlb_ref[...])
    f = jnp.exp(logf)
    eye = _eye(HD)
    for b in range(bb):
        rows = []
        for h in range(NH):
            sl = slice(h * HD, (h + 1) * HD)
            f_col = _to_col(f[b:b + 1, sl], eye)
            k_col = _to_col(k[b:b + 1, sl], eye)
            q_col = _to_col(q[b:b + 1, sl], eye)
            s_new = f_col * s_ref[b, h] + k_col * v[b:b + 1, sl]
            snew_ref[b, h] = s_new
            rows.append(jnp.sum(q_col * s_new, axis=0, keepdims=True))
        o_s[b:b + 1, :] = jnp.concatenate(rows, axis=1)
    o = _head_rms(o_s[...], og_ref[...])
    o_ref[...] = o * _silu(pa[:, 3 * D_BR:4 * D_BR])


def _hgrn2_step(pa, state, log_lb, log1m_lb, onorm_g):
    B = pa.shape[0]
    bb = min(STEP_ROWS, B)
    vec = pl.BlockSpec((1, D_BR), lambda i: (0, 0))
    st = pl.BlockSpec((bb, NH, HD, HD), lambda i: (i, 0, 0, 0))
    return pl.pallas_call(
        functools.partial(_hgrn2_step_kernel, bb=bb),
        grid=(B // bb,),
        in_specs=[pl.BlockSpec((bb, A_W), lambda i: (i, 0)), st, vec, vec, vec],
        out_specs=[pl.BlockSpec((bb, D_BR), lambda i: (i, 0)), st],
        out_shape=[jax.ShapeDtypeStruct((B, D_BR), F32),
                   jax.ShapeDtypeStruct((B, NH, HD, HD), F32)],
        scratch_shapes=[pltpu.VMEM((bb, D_BR), F32)],
        compiler_params=_cparams("parallel"),
        name="hgrn2_step",
    )(pa, state, log_lb, log1m_lb, onorm_g)


def _rwkv_prep(xs, prev, p):
    xm = xs + (prev - xs) * p["mu"]
    r = xm[:, 0:D_BR]
    k = xm[:, D_BR:2 * D_BR]
    v = xm[:, 2 * D_BR:3 * D_BR]
    xw = xm[:, 3 * D_BR:3 * D_BR + LORA]
    xa = xm[:, 3 * D_BR + LORA:3 * D_BR + 2 * LORA]
    w = -_softplus(-(p["w0"] + _bdot(jnp.tanh(xw), p["w2"]))) - 0.5
    logw = -jnp.exp(w)
    a = _sigmoid(p["a0"] + _bdot(xa, p["a2"]))
    kk = k * p["k_k"]
    kk = kk / jnp.maximum(jnp.sqrt(_seg_sum(kk * kk)), 1e-12)
    k = k * (1.0 + (a - 1.0) * p["k_a"])
    bonus = _seg_sum(r * k * p["r_k"]) * v
    return r, logw, k, v, -kk, kk * a, bonus


def _rwkv_post(y, bonus, g, p):
    mean = _seg_sum(y) * (1.0 / HD)
    yc = y - mean
    var = _seg_sum(yc * yc) * (1.0 / HD)
    yn = yc * lax.rsqrt(var + RWKV_GN_EPS) * p["gn_w"] + p["gn_b"]
    return (yn + bonus) * _silu(g)


def _unit_lower_inverse(m, m_t, n):
    nb = INV_BLOCK
    rows = lax.broadcasted_iota(jnp.int32, (nb, 1), 0)
    eye = _eye(nb)
    blocks = []
    for b in range(n // nb):
        mt = m_t[b * nb:(b + 1) * nb, b * nb:(b + 1) * nb]
        d = eye
        for i in range(1, nb):
            row = jnp.sum(mt[:, i:i + 1] * d, axis=0, keepdims=True)
            d = d + jnp.where(rows == i, row, 0.0)
        blocks.append(d)
    size = nb
    while size < n:
        merged = []
        for j in range(0, len(blocks), 2):
            lo, hi = blocks[j], blocks[j + 1]
            r0 = (j + 1) * size
            cross = _rw_inv(hi, _rw_inv(m[r0:r0 + size, r0 - size:r0], lo))
            top = jnp.concatenate([lo, jnp.zeros((size, size), F32)], axis=1)
            bot = jnp.concatenate([cross, hi], axis=1)
            merged.append(jnp.concatenate([top, bot], axis=0))
        blocks = merged
        size *= 2
    return blocks[0]


_rw_score = _bdot_nt
_rw_score_out = _bdot_nt
_rw_inv = _bdot
_rw_solve = _bdot
_rw_read = _bdot_nt
_rw_read_out = _bdot_nt
_rw_apply = _bdot
_rw_apply_out = _bdot
_rw_update = _bdot_tn

_RWKV_PARAMS = ("mu", "w0", "w2", "a0", "a2", "k_k", "k_a", "r_k", "gn_w", "gn_b")


def _rwkv_seq_kernel(pb_ref, shift0_ref, s0_ref, *rest, tb, chunk):
    np_ = len(_RWKV_PARAMS)
    p = {n: r[...] for n, r in zip(_RWKV_PARAMS, rest[:np_])}
    o_ref, sfin_ref, shift_ref = rest[np_:np_ + 3]
    r_s, k_s, v_s, a_s, b_s, w_s, g_s, y_s, bon_s, st_s, carry_s = rest[np_ + 3:]
    step = pl.program_id(1)

    @pl.when(step == 0)
    def _():
        st_s[...] = s0_ref[0]
        carry_s[...] = shift0_ref[0]

    xs = pb_ref[0][:, 0:B_SHIFT_W]
    row = lax.broadcasted_iota(jnp.int32, (tb, 1), 0)
    prev = jnp.where(row == 0, carry_s[...], pltpu.roll(xs, 1, 0))
    carry_s[...] = xs[tb - 1:tb, :]
    r, logw, k, v, av, bv, bonus = _rwkv_prep(xs, prev, p)
    gcum = _chunk_cumsum(logw, chunk)
    bon_s[...] = bonus
    for h in range(NH):
        sl = slice(h * HD, (h + 1) * HD)
        r_s[h] = r[:, sl]
        k_s[h] = k[:, sl]
        v_s[h] = v[:, sl]
        a_s[h] = av[:, sl]
        b_s[h] = bv[:, sl]
        w_s[h] = logw[:, sl]
        g_s[h] = gcum[:, sl]

    ri = lax.broadcasted_iota(jnp.int32, (chunk, chunk), 0)
    ci = lax.broadcasted_iota(jnp.int32, (chunk, chunk), 1)
    strict = ci < ri
    incl = ci <= ri

    def body(c, carry):
        r0 = pl.multiple_of(c * chunk, chunk)
        for h in range(NH):
            ld = lambda ref: ref[h, pl.ds(r0, chunk), :]
            rc, kc, vc, ac, bc, wc, gc = (ld(x) for x in (r_s, k_s, v_s, a_s, b_s, w_s, g_s))
            glast = gc[chunk - 1:chunk, :]
            eg = jnp.exp(gc)
            einv = jnp.exp(-gc)
            etail = jnp.exp(glast - gc)
            a_t = ac * jnp.exp(gc - wc)
            r_t = rc * eg
            b_t = bc * einv
            k_t = kc * einv
            s = st_s[h]
            m_ab = jnp.where(strict, _rw_score(a_t, b_t), 0.0)
            m_ab_t = jnp.where(ri < ci, _rw_score(b_t, a_t), 0.0)
            m_ak = jnp.where(strict, _rw_score(a_t, k_t), 0.0)
            n_rb = jnp.where(incl, _rw_score_out(r_t, b_t), 0.0)
            n_rk = jnp.where(incl, _rw_score_out(r_t, k_t), 0.0)
            inv = _unit_lower_inverse(m_ab, m_ab_t, chunk)
            sa = _rw_solve(inv, _rw_read(a_t, s) + _rw_apply(m_ak, vc))
            y = _rw_read_out(r_t, s) + _rw_apply_out(n_rb, sa) + _rw_apply_out(n_rk, vc)
            st_s[h] = s * jnp.exp(glast) + _rw_update(sa, bc * etail) + _rw_update(vc, kc * etail)
            y_s[pl.ds(r0, chunk), h * HD:(h + 1) * HD] = y
        return carry

    lax.fori_loop(0, tb // chunk, body, 0)

    o_ref[0] = _rwkv_post(y_s[...], bon_s[...], pb_ref[0][:, B_SHIFT_W:B_W], p)

    @pl.when(step == pl.num_programs(1) - 1)
    def _():
        sfin_ref[0] = st_s[...]
        shift_ref[0] = carry_s[...]


def _rwkv_param_specs(params, nargs):
    zero2 = (lambda b, i: (0, 0)) if nargs == 2 else (lambda i: (0, 0))
    return [pl.BlockSpec(x.shape, zero2) for x in params]


def _rwkv_seq(pb, shift0, s0, params):
    B, T, _ = pb.shape
    tb = min(MIX_TILE, T)
    st = pl.BlockSpec((1, NH, HD, HD), lambda b, i: (b, 0, 0, 0))
    sh = pl.BlockSpec((1, 1, B_SHIFT_W), lambda b, i: (b, 0, 0))
    return pl.pallas_call(
        functools.partial(_rwkv_seq_kernel, tb=tb, chunk=CHUNK_B),
        grid=(B, T // tb),
        in_specs=[pl.BlockSpec((1, tb, B_W), lambda b, i: (b, i, 0)), sh, st]
                 + _rwkv_param_specs(params, 2),
        out_specs=[pl.BlockSpec((1, tb, D_BR), lambda b, i: (b, i, 0)), st, sh],
        out_shape=[jax.ShapeDtypeStruct((B, T, D_BR), F32),
                   jax.ShapeDtypeStruct((B, NH, HD, HD), F32),
                   jax.ShapeDtypeStruct((B, 1, B_SHIFT_W), F32)],
        scratch_shapes=[pltpu.VMEM((NH, tb, HD), F32)] * 7
                       + [pltpu.VMEM((tb, D_BR), F32), pltpu.VMEM((tb, D_BR), F32),
                          pltpu.VMEM((NH, HD, HD), F32), pltpu.VMEM((1, B_SHIFT_W), F32)],
        compiler_params=_cparams("parallel", "arbitrary"),
        name="rwkv7_seq",
    )(pb, shift0, s0, *params)


def _rwkv_step_kernel(pb_ref, shift0_ref, s_ref, *rest, bb):
    np_ = len(_RWKV_PARAMS)
    p = {n: r[...] for n, r in zip(_RWKV_PARAMS, rest[:np_])}
    o_ref, snew_ref, y_s = rest[np_:]
    pb = pb_ref[...]
    r, logw, k, v, av, bv, bonus = _rwkv_prep(pb[:, 0:B_SHIFT_W], shift0_ref[...], p)
    dec = jnp.exp(logw)
    eye = _eye(HD)
    for b in range(bb):
        rows = []
        for h in range(NH):
            sl = slice(h * HD, (h + 1) * HD)
            row = lambda x: x[b:b + 1, sl]
            s = s_ref[b, h]
            sa = jnp.sum(s * row(av), axis=1, keepdims=True)
            v_col = _to_col(row(v), eye)
            s_new = s * row(dec) + sa * row(bv) + v_col * row(k)
            snew_ref[b, h] = s_new
            y_col = jnp.sum(s_new * row(r), axis=1, keepdims=True)
            rows.append(_to_row(y_col, eye))
        y_s[b:b + 1, :] = jnp.concatenate(rows, axis=1)
    o_ref[...] = _rwkv_post(y_s[...], bonus, pb[:, B_SHIFT_W:B_W], p)


def _rwkv_step(pb, shift0, state, params):
    B = pb.shape[0]
    bb = min(STEP_ROWS, B)
    st = pl.BlockSpec((bb, NH, HD, HD), lambda i: (i, 0, 0, 0))
    return pl.pallas_call(
        functools.partial(_rwkv_step_kernel, bb=bb),
        grid=(B // bb,),
        in_specs=[pl.BlockSpec((bb, B_W), lambda i: (i, 0)),
                  pl.BlockSpec((bb, B_SHIFT_W), lambda i: (i, 0)), st]
                 + _rwkv_param_specs(params, 1),
        out_specs=[pl.BlockSpec((bb, D_BR), lambda i: (i, 0)), st],
        out_shape=[jax.ShapeDtypeStruct((B, D_BR), F32),
                   jax.ShapeDtypeStruct((B, NH, HD, HD), F32)],
        scratch_shapes=[pltpu.VMEM((bb, D_BR), F32)],
        compiler_params=_cparams("parallel"),
        name="rwkv7_step",
    )(pb, shift0, state, *params)


def _qkv_prep_kernel(pc_ref, cos_ref, sin_ref, qg_ref, kg_ref, qh_ref, kh_ref, vh_ref, kc_ref):
    pc = pc_ref[0]
    cos = cos_ref[...]
    sin = sin_ref[...]
    q = _rope(_head_rms(pc[:, 0:D_BR], qg_ref[...]), cos, sin) * (HD ** -0.5)
    k = _rope(_head_rms(pc[:, D_BR:2 * D_BR], kg_ref[...]), cos, sin)
    v = pc[:, 2 * D_BR:3 * D_BR]
    kc_ref[0] = k
    for h in range(NH):
        sl = slice(h * HD, (h + 1) * HD)
        qh_ref[0, h] = q[:, sl].astype(BF16)
        kh_ref[0, h] = k[:, sl].astype(BF16)
        vh_ref[0, h] = v[:, sl].astype(BF16)


def _qkv_prep(pc, cos, sin, qg, kg):
    B, T, _ = pc.shape
    tm = min(ROW_TILE, T)
    vec = pl.BlockSpec((1, D_BR), lambda b, i: (0, 0))
    tab = pl.BlockSpec((tm, D_BR), lambda b, i: (i, 0))
    hm = pl.BlockSpec((1, NH, tm, HD), lambda b, i: (b, 0, i, 0))
    return pl.pallas_call(
        _qkv_prep_kernel,
        grid=(B, T // tm),
        in_specs=[pl.BlockSpec((1, tm, C_W), lambda b, i: (b, i, 0)), tab, tab, vec, vec],
        out_specs=[hm, hm, hm, pl.BlockSpec((1, tm, D_BR), lambda b, i: (b, i, 0))],
        out_shape=[jax.ShapeDtypeStruct((B, NH, T, HD), BF16)] * 3
                  + [jax.ShapeDtypeStruct((B, T, D_BR), F32)],
        compiler_params=_cparams("parallel", "parallel"),
        name="attn_qkv_prep",
    )(pc, cos, sin, qg, kg)


def _attn_seq_kernel(q_ref, k_ref, v_ref, g_ref, mult_ref, o_ref, *, n_back):
    i = pl.program_id(1)
    lo = jnp.maximum(i - n_back, 0)
    outs = []
    for h in range(NH):
        qh = q_ref[0, h]

        def body(kb, carry):
            m, l, acc = carry
            r0 = pl.multiple_of(kb * Q_BLOCK, Q_BLOCK)
            kblk = k_ref[0, h, pl.ds(r0, Q_BLOCK), :]
            vblk = v_ref[0, h, pl.ds(r0, Q_BLOCK), :]
            s = lax.dot_general(qh, kblk, (((1,), (1,)), ((), ())), preferred_element_type=F32)
            mult = mult_ref[i - kb]
            sm = jnp.where(mult > 0.0, s, MASK_VALUE)
            m_new = jnp.maximum(m, jnp.max(sm, axis=1, keepdims=True))
            alpha = jnp.exp(m - m_new)
            p = jnp.exp(sm - m_new) * mult
            l = l * alpha + jnp.sum(p, axis=1, keepdims=True)
            acc = acc * alpha + jnp.dot(p.astype(BF16), vblk, preferred_element_type=F32)
            return m_new, l, acc

        init = (jnp.full((Q_BLOCK, 1), MASK_VALUE, F32), jnp.zeros((Q_BLOCK, 1), F32),
                jnp.zeros((Q_BLOCK, HD), F32))
        m, l, acc = lax.fori_loop(lo, i + 1, body, init)
        outs.append(acc / l)
    o_ref[0] = jnp.concatenate(outs, axis=1) * _silu(g_ref[0])


def _window_multiplicity(n_back):
    d = jnp.arange(n_back + 1)[:, None, None] * Q_BLOCK
    delta = d + jnp.arange(Q_BLOCK)[None, :, None] - jnp.arange(Q_BLOCK)[None, None, :]
    mult = jnp.zeros(delta.shape, F32)
    for w, dil in C_CONFIGS:
        mult = mult + ((delta >= 0) & (delta <= w) & (delta % dil == 0)).astype(F32)
    return mult


def _attn_seq(qh, kh, vh, pc):
    B, _, T, _ = qh.shape
    n_back = C_WIN_MAX // Q_BLOCK
    mult = _window_multiplicity(n_back)
    full = pl.BlockSpec((1, NH, T, HD), lambda b, i: (b, 0, 0, 0))
    return pl.pallas_call(
        functools.partial(_attn_seq_kernel, n_back=n_back),
        grid=(B, T // Q_BLOCK),
        in_specs=[pl.BlockSpec((1, NH, Q_BLOCK, HD), lambda b, i: (b, 0, i, 0)), full, full,
                  pl.BlockSpec((1, Q_BLOCK, D_BR), lambda b, i: (b, i, 3)),
                  pl.BlockSpec(mult.shape, lambda b, i: (0, 0, 0))],
        out_specs=pl.BlockSpec((1, Q_BLOCK, D_BR), lambda b, i: (b, i, 0)),
        out_shape=jax.ShapeDtypeStruct((B, T, D_BR), F32),
        compiler_params=_cparams("parallel", "arbitrary"),
        name="dilated_attn_seq",
    )(qh, kh, vh, pc, mult)


def _attn_step_kernel(pc_ref, cos_ref, sin_ref, qg_ref, kg_ref,
                      k1_ref, k2_ref, k3_ref, v1_ref, v2_ref, v3_ref,
                      o_ref, knew_ref, vnew_ref, *, bb):
    pc = pc_ref[...]
    cos = cos_ref[...]
    sin = sin_ref[...]
    q = _rope(_head_rms(pc[:, 0:D_BR], qg_ref[...]), cos, sin) * (HD ** -0.5)
    k = _rope(_head_rms(pc[:, D_BR:2 * D_BR], kg_ref[...]), cos, sin)
    v = pc[:, 2 * D_BR:3 * D_BR]
    knew_ref[...] = k
    vnew_ref[...] = v
    n_self = float(len(C_CONFIGS))
    heads = lambda x, b: jnp.concatenate(
        [x[b:b + 1, h * HD:(h + 1) * HD] for h in range(NH)], axis=0)
    for b in range(bb):
        qb, kb, vb = heads(q, b), heads(k, b), heads(v, b)
        s_self = jnp.sum(qb * kb, axis=1, keepdims=True)
        scores = [jnp.sum(kr[b] * qb[None], axis=2, keepdims=True)
                  for kr in (k1_ref, k2_ref, k3_ref)]
        m = s_self
        for s in scores:
            m = jnp.maximum(m, jnp.max(s, axis=0))
        p_self = n_self * jnp.exp(s_self - m)
        den = p_self
        acc = p_self * vb
        for s, vr in zip(scores, (v1_ref, v2_ref, v3_ref)):
            p = jnp.exp(s - m[None])
            den = den + jnp.sum(p, axis=0)
            acc = acc + jnp.sum(p * vr[b], axis=0)
        ob = acc / den
        o_ref[b:b + 1, :] = jnp.concatenate([ob[h:h + 1, :] for h in range(NH)], axis=1)
    o_ref[...] = o_ref[...] * _silu(pc[:, 3 * D_BR:4 * D_BR])


def _attn_step(pc, cos, sin, qg, kg, cache_k, cache_v, layer):
    B = pc.shape[0]
    P = cache_k.shape[2]
    bb = min(ATTN_STEP_ROWS, B)
    n = Q_BLOCK
    views, specs = [], []
    for c in (cache_k, cache_v):
        for w, dil in C_CONFIGS:
            groups = P // dil
            views.append(c.reshape(DEPTH, B, groups // n, n, dil, NH, HD))
            specs.append(pl.BlockSpec((None, bb, None, n, None, NH, HD),
                                      lambda i, g=groups // n - 1: (layer, i, g, 0, 0, 0, 0)))
    vec = pl.BlockSpec((1, D_BR), lambda i: (0, 0))
    row = pl.BlockSpec((None, bb, D_BR), lambda i: (i, 0, 0))
    outs = pl.pallas_call(
        functools.partial(_attn_step_kernel, bb=bb),
        grid=(B // bb,),
        in_specs=[pl.BlockSpec((None, bb, C_W), lambda i: (i, 0, 0)), vec, vec, vec, vec] + specs,
        out_specs=[row, row, row],
        out_shape=[jax.ShapeDtypeStruct((B // bb, bb, D_BR), F32)] * 3,
        compiler_params=_cparams("parallel"),
        name="dilated_attn_step",
    )(pc.reshape(B // bb, bb, C_W), cos, sin, qg, kg, *views)
    return [o.reshape(B, D_BR) for o in outs]


def _pool_mix(pooled, u, g, w_ref, scale_ref):
    mixed = _bdot(pooled - u, w_ref[...]) * scale_ref[...]
    return mixed * _silu(g)


def _pool_select(sums, cnts):
    lane = lax.broadcasted_iota(jnp.int32, sums[0].shape, 1) // (D_BR // len(POOL_WINDOWS))
    out = sums[-1] / cnts[-1]
    for gi in range(len(POOL_WINDOWS) - 2, -1, -1):
        out = jnp.where(lane == gi, sums[gi] / cnts[gi], out)
    return out


def _pool_seq_kernel(pd_ref, buf0_ref, w_ref, scale_ref, o_ref, carry_s, *, tb, t0):
    step = pl.program_id(1)

    @pl.when(step == 0)
    def _():
        carry_s[...] = buf0_ref[0]

    pd = pd_ref[0]
    u = pd[:, 0:D_BR]
    ext = jnp.concatenate([carry_s[...], u], axis=0)
    carry_s[...] = u[tb - POOL_MAX:tb, :]
    pos = (t0 + step * tb + lax.broadcasted_iota(jnp.int32, (tb, 1), 0)).astype(F32)
    sums, cnts = [], []
    run = ext
    shift = 1
    for w in POOL_WINDOWS:
        while shift < w:
            run = run + pltpu.roll(run, shift, 0)
            shift *= 2
        sums.append(run[POOL_MAX:, :])
        cnts.append(jnp.minimum(float(w), pos + 1.0))
    o_ref[0] = _pool_mix(_pool_select(sums, cnts), u, pd[:, D_BR:2 * D_BR], w_ref, scale_ref)


def _pool_seq(pd, buf0, w_bd, scale, t0):
    B, T, _ = pd.shape
    tb = min(ROW_TILE, T)
    return pl.pallas_call(
        functools.partial(_pool_seq_kernel, tb=tb, t0=t0),
        grid=(B, T // tb),
        in_specs=[pl.BlockSpec((1, tb, D_W), lambda b, i: (b, i, 0)),
                  pl.BlockSpec((1, POOL_MAX, D_BR), lambda b, i: (b, 0, 0)),
                  pl.BlockSpec((D_BR, D_BR), lambda b, i: (0, 0)),
                  pl.BlockSpec((1, D_BR), lambda b, i: (0, 0))],
        out_specs=pl.BlockSpec((1, tb, D_BR), lambda b, i: (b, i, 0)),
        out_shape=jax.ShapeDtypeStruct((B, T, D_BR), F32),
        scratch_shapes=[pltpu.VMEM((POOL_MAX, D_BR), F32)],
        compiler_params=_cparams("parallel", "arbitrary"),
        name="pool_seq",
    )(pd, buf0, w_bd, scale)


def _pool_step_kernel(pd_ref, buf_ref, w_ref, scale_ref, o_ref, *, t0):
    pd = pd_ref[...]
    u = pd[:, 0:D_BR]
    nbuf = POOL_MAX - 1
    sums, cnts = [], []
    run = u
    taken = 1
    for w in POOL_WINDOWS:
        while taken < w:
            run = run + buf_ref[:, nbuf - taken, :]
            taken += 1
        sums.append(run)
        cnts.append(float(min(w, t0 + 1)))
    o_ref[...] = _pool_mix(_pool_select(sums, cnts), u, pd[:, D_BR:2 * D_BR], w_ref, scale_ref)


def _pool_step(pd, buf, w_bd, scale, t0):
    B = pd.shape[0]
    bb = min(POOL_STEP_ROWS, B)
    return pl.pallas_call(
        functools.partial(_pool_step_kernel, t0=t0),
        grid=(B // bb,),
        in_specs=[pl.BlockSpec((bb, D_W), lambda i: (i, 0)),
                  pl.BlockSpec((bb, POOL_MAX - 1, D_BR), lambda i: (i, 0, 0)),
                  pl.BlockSpec((D_BR, D_BR), lambda i: (0, 0)),
                  pl.BlockSpec((1, D_BR), lambda i: (0, 0))],
        out_specs=pl.BlockSpec((bb, D_BR), lambda i: (i, 0)),
        out_shape=jax.ShapeDtypeStruct((B, D_BR), F32),
        compiler_params=_cparams("parallel"),
        name="pool_step",
    )(pd, buf, w_bd, scale)


def _rope_tables(pos):
    half = HD // 2
    inv_freq = jnp.float32(ROPE_THETA) ** (-jnp.arange(half, dtype=F32) / half)
    ang = pos.astype(F32)[:, None] * inv_freq[None, :]
    cos = jnp.cos(ang)
    sin = jnp.sin(ang)
    cos_t = jnp.tile(jnp.concatenate([cos, cos], axis=-1), (1, NH))
    sin_t = jnp.tile(jnp.concatenate([-sin, sin], axis=-1), (1, NH))
    return cos_t, sin_t


def _block_diag(w):
    g, c, d = w.shape
    out = jnp.zeros((g * c, g * d), w.dtype)
    for i in range(g):
        out = out.at[i * c:(i + 1) * c, i * d:(i + 1) * d].set(w[i])
    return out


def _layer_params(l, W):
    row = lambda x: x[l].reshape(1, -1)
    w_in = W["w_in"][l].astype(BF16)
    cuts = (0, A_W, A_W + B_W, A_W + B_W + C_W, A_W + B_W + C_W + D_W)
    return dict(
        norm_g=W["norm_g"][l],
        w_parts=[w_in[:, cuts[i]:cuts[i + 1]] for i in range(4)],
        w_out=W["w_out"][l].astype(BF16),
        onorm_g=row(W["a_onorm_g"]),
        rwkv=[row(W["b_mu"]), row(W["b_w0"]), W["b_w2"][l], row(W["b_a0"]), W["b_a2"][l],
              row(W["b_k_k"]), row(W["b_k_a"]), row(W["b_r_k"]), row(W["b_gn_w"]), row(W["b_gn_b"])],
        qg=jnp.tile(W["c_qnorm_g"][l], NH).reshape(1, D_BR),
        kg=jnp.tile(W["c_knorm_g"][l], NH).reshape(1, D_BR),
        w_pool=_block_diag(W["d_w_pool"][l]),
        p_scale=row(W["d_scale"]),
    )


def _run_prompt(x, mods, layers, log_lb, log1m_lb):
    B, T, _ = x.shape
    cos, sin = _rope_tables(jnp.arange(T))
    c_keep = min(C_WIN_MAX, T)
    d_keep = min(POOL_MAX - 1, T)
    zero_state = jnp.zeros((B, NH, HD, HD), F32)
    zero_shift = jnp.zeros((B, 1, B_SHIFT_W), F32)
    zero_pool = jnp.zeros((B, POOL_MAX, D_BR), F32)
    nA, nBw, nBs, nK, nV, nD = [], [], [], [], [], []
    for l, L in enumerate(layers):
        shift, scale, gate = (mods[l][:, None, i * D_MODEL:(i + 1) * D_MODEL] for i in range(3))
        pa, pb, pc, pd = _in_proj(x, shift, scale, L["norm_g"], L["w_parts"])
        oa, sa_t = _hgrn2_seq(pa, zero_state, log_lb[l], log1m_lb[l], L["onorm_g"])
        ob, sb, sbs = _rwkv_seq(pb, zero_shift, zero_state, L["rwkv"])
        qh, kh, vh, k_rot = _qkv_prep(pc, cos, sin, L["qg"], L["kg"])
        oc = _attn_seq(qh, kh, vh, pc)
        od = _pool_seq(pd, zero_pool, L["w_pool"], L["p_scale"], 0)
        x = _out_proj(oa, ob, oc, od, x, gate, L["w_out"])
        nA.append(jnp.swapaxes(sa_t, -1, -2))
        nBw.append(sb)
        nBs.append(sbs[:, 0])
        nK.append(k_rot[:, T - c_keep:].reshape(B, c_keep, NH, HD))
        nV.append(pc[:, T - c_keep:, 2 * D_BR:3 * D_BR].reshape(B, c_keep, NH, HD))
        nD.append(pd[:, T - d_keep:, 0:D_BR])
    return (x,) + tuple(jnp.stack(a) for a in (nA, nBw, nBs, nK, nV, nD))


def _run_sample(x, mods, layers, log_lb, log1m_lb, sA, sBw, sBs, ck, cv, cd):
    B, T, _ = x.shape
    assert T == 1 and ck.shape[2] == PAST_LEN == C_WIN_MAX
    cos, sin = _rope_tables(jnp.full((1,), PAST_LEN))
    xr = x.reshape(1, B, D_MODEL)
    nA, nBw, nBs, nK, nV, nD = [], [], [], [], [], []
    for l, L in enumerate(layers):
        shift, scale, gate = (mods[l][None, :, i * D_MODEL:(i + 1) * D_MODEL] for i in range(3))
        pa, pb, pc, pd = (p[0] for p in _in_proj(xr, shift, scale, L["norm_g"], L["w_parts"]))
        oa, sa = _hgrn2_step(pa, sA[l], log_lb[l], log1m_lb[l], L["onorm_g"])
        ob, sb = _rwkv_step(pb, sBs[l], sBw[l], L["rwkv"])
        oc, k_new, v_new = _attn_step(pc, cos, sin, L["qg"], L["kg"], ck, cv, l)
        od = _pool_step(pd, cd[l], L["w_pool"], L["p_scale"], PAST_LEN)
        xr = _out_proj(oa[None], ob[None], oc[None], od[None], xr, gate, L["w_out"])
        nA.append(sa)
        nBw.append(sb)
        nBs.append(pb[:, 0:B_SHIFT_W])
        nK.append(k_new.reshape(B, 1, NH, HD))
        nV.append(v_new.reshape(B, 1, NH, HD))
        nD.append(pd[:, None, 0:D_BR])
    return (xr.reshape(B, 1, D_MODEL),) + tuple(jnp.stack(a) for a in (nA, nBw, nBs, nK, nV, nD))


def kernel(x_prompt, x_sample, state_A, state_B_wkv, state_B_shift, cache_C_k, cache_C_v, cache_D_pool, c_prompt, c_sample, ada_w, ada_b, norm_g, w_in, w_out, a_lb_logits, a_onorm_g, b_mu, b_w0, b_w2, b_a0, b_a2, b_k_k, b_k_a, b_r_k, b_gn_w, b_gn_b, c_qnorm_g, c_knorm_g, d_w_pool, d_scale):
    W = dict(norm_g=norm_g, w_in=w_in, w_out=w_out, a_onorm_g=a_onorm_g, b_mu=b_mu, b_w0=b_w0,
             b_w2=b_w2, b_a0=b_a0, b_a2=b_a2, b_k_k=b_k_k, b_k_a=b_k_a, b_r_k=b_r_k,
             b_gn_w=b_gn_w, b_gn_b=b_gn_b, c_qnorm_g=c_qnorm_g, c_knorm_g=c_knorm_g,
             d_w_pool=d_w_pool, d_scale=d_scale)
    layers = [_layer_params(l, W) for l in range(DEPTH)]
    lb_sm = jax.nn.softmax(a_lb_logits.astype(F32), axis=0)
    lower = jnp.cumsum(lb_sm, axis=0) - lb_sm[0:1]
    log_lb = jnp.log(jnp.maximum(lower, LB_FLOOR))[:, None, :]
    log1m_lb = jnp.log1p(-lower)[:, None, :]

    nb = x_prompt.shape[0]
    mods = _modulation(jnp.concatenate([c_prompt, c_sample], axis=0), ada_w, ada_b)
    y_p, pA, pBw, pBs, pK, pV, pD = _run_prompt(x_prompt, mods[:, :nb], layers, log_lb, log1m_lb)
    y_s, sA, sBw, sBs, sK, sV, sD = _run_sample(
        x_sample, mods[:, nb:], layers, log_lb, log1m_lb,
        state_A, state_B_wkv, state_B_shift, cache_C_k, cache_C_v, cache_D_pool)
    return (y_p, y_s, pA, sA, pBw, sBw, pBs, sBs, pK, sK, pV, sV, pD, sD)
```

```python
import functools

import jax
import jax.numpy as jnp
from jax import lax
from jax.experimental import pallas as pl
from jax.experimental.pallas import tpu as pltpu

F32 = jnp.float32
BF16 = jnp.bfloat16

D_MODEL = 1024
DEPTH = 4
PAST_LEN = 2048
D_BR = 256
HD = 64
NH = D_BR // HD
LB_FLOOR = 1e-30
LORA = 32
RWKV_GN_EPS = 64e-5
C_CONFIGS = ((128, 1), (512, 4), (2048, 16))
C_WIN_MAX = 2048
MASK_VALUE = -1e30
ROPE_THETA = 10000.0
POOL_WINDOWS = (2, 4, 8, 16)
POOL_MAX = 16
EPS = 1e-6
A_W = 4 * D_BR
B_SHIFT_W = 3 * D_BR + 2 * LORA
B_W = B_SHIFT_W + D_BR
C_W = 4 * D_BR
D_W = 2 * D_BR

LANES = 128

ROW_TILE = 512
MIX_TILE = 256
CHUNK_A = 16
CHUNK_B = 64
INV_BLOCK = 16
Q_BLOCK = 128
ATTN_TILE = 256
STEP_ROWS = 8
ATTN_STEP_ROWS = 2
POOL_STEP_ROWS = 32
VMEM_LIMIT = 56 * 1024 * 1024


def _cparams(*sem):
    return pltpu.CompilerParams(dimension_semantics=sem, vmem_limit_bytes=VMEM_LIMIT)


def _sigmoid(x):
    return 1.0 / (1.0 + jnp.exp(-x))


def _silu(x):
    return x * _sigmoid(x)


def _softplus(x):
    return jnp.maximum(x, 0.0) + jnp.log1p(jnp.exp(-jnp.abs(x)))


def _log_sigmoid(x):
    return -_softplus(-x)


def _logaddexp(a, b):
    return jnp.maximum(a, b) + jnp.log1p(jnp.exp(-jnp.abs(a - b)))


def _bdot(a, b):
    return jnp.dot(a.astype(BF16), b.astype(BF16), preferred_element_type=F32)


def _bdot_nt(a, b):
    return lax.dot_general(a.astype(BF16), b.astype(BF16), (((1,), (1,)), ((), ())),
                           preferred_element_type=F32)


def _bdot_tn(a, b):
    return lax.dot_general(a.astype(BF16), b.astype(BF16), (((0,), (0,)), ((), ())),
                           preferred_element_type=F32)


def _split3(x):
    h1 = x.astype(BF16)
    r1 = x - h1.astype(F32)
    h2 = r1.astype(BF16)
    r2 = r1 - h2.astype(F32)
    return h1, h2, r2.astype(BF16)


def _seg_sum(x):
    n = x.shape[-1]
    r = lax.broadcasted_iota(jnp.int32, (n, n), 0) // HD
    c = lax.broadcasted_iota(jnp.int32, (n, n), 1) // HD
    ones = jnp.where(r == c, 1.0, 0.0).astype(BF16)
    h1, h2, h3 = _split3(x)
    d = lambda h: jnp.dot(h, ones, preferred_element_type=F32)
    return d(h1) + d(h2) + d(h3)


def _chunk_cumsum(x, chunk):
    t = x.shape[0]
    r = lax.broadcasted_iota(jnp.int32, (t, t), 0)
    c = lax.broadcasted_iota(jnp.int32, (t, t), 1)
    tri = jnp.where((r // chunk == c // chunk) & (c <= r), 1.0, 0.0).astype(BF16)
    h1, h2, h3 = _split3(x)
    d = lambda h: jnp.dot(tri, h, preferred_element_type=F32)
    return d(h1) + d(h2) + d(h3)


def _head_rms(x, gamma):
    ms = _seg_sum(x * x) * (1.0 / HD)
    return x * lax.rsqrt(ms + EPS) * gamma


def _rope(x, cos, sin_signed):
    lane = lax.broadcasted_iota(jnp.int32, x.shape, 1) % HD
    partner = jnp.where(lane < HD // 2,
                        pltpu.roll(x, D_BR - HD // 2, 1),
                        pltpu.roll(x, HD // 2, 1))
    return x * cos + partner * sin_signed


def _eye(n):
    r = lax.broadcasted_iota(jnp.int32, (n, n), 0)
    c = lax.broadcasted_iota(jnp.int32, (n, n), 1)
    return jnp.where(r == c, 1.0, 0.0).astype(F32)


def _to_col(row, eye):
    return jnp.sum(eye * row, axis=1, keepdims=True)


def _to_row(col, eye):
    return jnp.sum(eye * col, axis=0, keepdims=True)


def _mod_kernel(c_ref, w_ref, b_ref, o_ref):
    c = _silu(c_ref[...])
    o_ref[0] = _bdot(c, w_ref[0]) + b_ref[0]


def _modulation(c_all, ada_w, ada_b):
    n = c_all.shape[0]
    tn = 512
    return pl.pallas_call(
        _mod_kernel,
        grid=(DEPTH, 3 * D_MODEL // tn),
        in_specs=[pl.BlockSpec((n, D_MODEL), lambda l, j: (0, 0)),
                  pl.BlockSpec((1, D_MODEL, tn), lambda l, j: (l, 0, j)),
                  pl.BlockSpec((1, 1, tn), lambda l, j: (l, 0, j))],
        out_specs=pl.BlockSpec((1, n, tn), lambda l, j: (l, 0, j)),
        out_shape=jax.ShapeDtypeStruct((DEPTH, n, 3 * D_MODEL), F32),
        compiler_params=_cparams("parallel", "parallel"),
        name="adaln_modulation",
    )(c_all, ada_w, ada_b.reshape(DEPTH, 1, 3 * D_MODEL))


def _in_proj_kernel(x_ref, shift_ref, scale_ref, g_ref, wa_ref, wb_ref, wc_ref, wd_ref,
                    pa_ref, pb_ref, pc_ref, pd_ref):
    x = x_ref[0]
    ms = jnp.mean(x * x, axis=-1, keepdims=True)
    h = x * lax.rsqrt(ms + EPS) * g_ref[...]
    h = (h * (1.0 + scale_ref[0]) + shift_ref[0]).astype(BF16)
    pa_ref[0] = jnp.dot(h, wa_ref[...], preferred_element_type=F32)
    pb_ref[0] = jnp.dot(h, wb_ref[...], preferred_element_type=F32)
    pc_ref[0] = jnp.dot(h, wc_ref[...], preferred_element_type=F32)
    pd_ref[0] = jnp.dot(h, wd_ref[...], preferred_element_type=F32)


def _in_proj(x, shift, scale, norm_g, w_parts):
    B, T, D = x.shape
    tm = min(ROW_TILE, T)
    mt = shift.shape[1]
    mod_block = (1, tm, D) if mt == T else (1, 1, D)
    mod_map = (lambda b, i: (b, i, 0)) if mt == T else (lambda b, i: (b, 0, 0))
    widths = [w.shape[1] for w in w_parts]
    return pl.pallas_call(
        _in_proj_kernel,
        grid=(B, T // tm),
        in_specs=[pl.BlockSpec((1, tm, D), lambda b, i: (b, i, 0)),
                  pl.BlockSpec(mod_block, mod_map),
                  pl.BlockSpec(mod_block, mod_map),
                  pl.BlockSpec((1, D), lambda b, i: (0, 0))]
                 + [pl.BlockSpec((D, w), lambda b, i: (0, 0)) for w in widths],
        out_specs=[pl.BlockSpec((1, tm, w), lambda b, i: (b, i, 0)) for w in widths],
        out_shape=[jax.ShapeDtypeStruct((B, T, w), F32) for w in widths],
        compiler_params=_cparams("parallel", "parallel"),
        name="norm_in_proj",
    )(x, shift, scale, norm_g.reshape(1, D), *w_parts)


def _out_proj_kernel(oa_ref, ob_ref, oc_ref, od_ref, x_ref, gate_ref, w_ref, y_ref):
    mix = jnp.concatenate([oa_ref[0], ob_ref[0], oc_ref[0], od_ref[0]], axis=-1).astype(BF16)
    res = jnp.dot(mix, w_ref[...], preferred_element_type=F32)
    y_ref[0] = x_ref[0] + gate_ref[0] * res


def _out_proj(oa, ob, oc, od, x, gate, w_out):
    B, T, D = x.shape
    tm = min(ROW_TILE, T)
    mt = gate.shape[1]
    mod_block = (1, tm, D) if mt == T else (1, 1, D)
    mod_map = (lambda b, i: (b, i, 0)) if mt == T else (lambda b, i: (b, 0, 0))
    br = pl.BlockSpec((1, tm, D_BR), lambda b, i: (b, i, 0))
    return pl.pallas_call(
        _out_proj_kernel,
        grid=(B, T // tm),
        in_specs=[br, br, br, br,
                  pl.BlockSpec((1, tm, D), lambda b, i: (b, i, 0)),
                  pl.BlockSpec(mod_block, mod_map),
                  pl.BlockSpec((D, D), lambda b, i: (0, 0))],
        out_specs=pl.BlockSpec((1, tm, D), lambda b, i: (b, i, 0)),
        out_shape=jax.ShapeDtypeStruct((B, T, D), F32),
        compiler_params=_cparams("parallel", "parallel"),
        name="out_proj_residual",
    )(oa, ob, oc, od, x, gate, w_out)


def _hgrn2_gates(pa, log_lb, log1m_lb):
    q = _silu(pa[:, 0:D_BR])
    fl = pa[:, D_BR:2 * D_BR]
    v = pa[:, 2 * D_BR:3 * D_BR]
    ls_neg = _log_sigmoid(-fl)
    logf = _logaddexp(_log_sigmoid(fl), log_lb + ls_neg)
    k = jnp.exp(log1m_lb + ls_neg)
    return q, logf, k, v


def _hgrn2_seq_kernel(pa_ref, s0_ref, loglb_ref, log1mlb_ref, og_ref, o_ref, sfin_ref,
                      q_s, k_s, v_s, g_s, o_s, st_s, *, tb, chunk):
    step = pl.program_id(1)

    @pl.when(step == 0)
    def _():
        st_s[...] = s0_ref[0]

    q, logf, k, v = _hgrn2_gates(pa_ref[0], loglb_ref[...], log1mlb_ref[...])
    gcum = _chunk_cumsum(logf, chunk)
    for h in range(NH):
        sl = slice(h * HD, (h + 1) * HD)
        q_s[h] = q[:, sl]
        k_s[h] = k[:, sl]
        v_s[h] = v[:, sl]
        g_s[h] = gcum[:, sl]

    rows = lax.broadcasted_iota(jnp.int32, (chunk, 1), 0)

    def body(c, carry):
        r0 = pl.multiple_of(c * chunk, chunk)
        for h in range(NH):
            qc = q_s[h, pl.ds(r0, chunk), :]
            kc = k_s[h, pl.ds(r0, chunk), :]
            vc = v_s[h, pl.ds(r0, chunk), :]
            gc = g_s[h, pl.ds(r0, chunk), :]
            st = st_s[h]
            o_acc = _bdot_nt(qc * jnp.exp(gc), st)
            for t in range(chunk):
                causal = rows <= t
                d = jnp.where(causal, gc[t:t + 1, :] - gc, 0.0)
                e = jnp.where(causal, jnp.exp(d), 0.0)
                att = jnp.sum(e * kc * qc[t:t + 1, :], axis=1, keepdims=True)
                ot = jnp.sum(att * vc, axis=0, keepdims=True)
                o_acc = o_acc + jnp.where(rows == t, ot, 0.0)
            glast = gc[chunk - 1:chunk, :]
            st_s[h] = st * jnp.exp(glast) + _bdot_tn(vc, kc * jnp.exp(glast - gc))
            o_s[pl.ds(r0, chunk), h * HD:(h + 1) * HD] = o_acc
        return carry

    lax.fori_loop(0, tb // chunk, body, 0)

    o = _head_rms(o_s[...], og_ref[...])
    o_ref[0] = o * _silu(pa_ref[0][:, 3 * D_BR:4 * D_BR])

    @pl.when(step == pl.num_programs(1) - 1)
    def _():
        sfin_ref[0] = st_s[...]


def _hgrn2_seq(pa, s0_t, log_lb, log1m_lb, onorm_g):
    B, T, _ = pa.shape
    tb = min(MIX_TILE, T)
    vec = pl.BlockSpec((1, D_BR), lambda b, i: (0, 0))
    st = pl.BlockSpec((1, NH, HD, HD), lambda b, i: (b, 0, 0, 0))
    return pl.pallas_call(
        functools.partial(_hgrn2_seq_kernel, tb=tb, chunk=CHUNK_A),
        grid=(B, T // tb),
        in_specs=[pl.BlockSpec((1, tb, A_W), lambda b, i: (b, i, 0)), st, vec, vec, vec],
        out_specs=[pl.BlockSpec((1, tb, D_BR), lambda b, i: (b, i, 0)), st],
        out_shape=[jax.ShapeDtypeStruct((B, T, D_BR), F32),
                   jax.ShapeDtypeStruct((B, NH, HD, HD), F32)],
        scratch_shapes=[pltpu.VMEM((NH, tb, HD), F32)] * 4
                       + [pltpu.VMEM((tb, D_BR), F32), pltpu.VMEM((NH, HD, HD), F32)],
        compiler_params=_cparams("parallel", "arbitrary"),
        name="hgrn2_seq",
    )(pa, s0_t, log_lb, log1m_lb, onorm_g)


def _hgrn2_step_kernel(pa_ref, s_ref, loglb_ref, log1mlb_ref, og_ref, o_ref, snew_ref, o_s, *, bb):
    pa = pa_ref[...]
    q, logf, k, v = _hgrn2_gates(pa, loglb_ref[...], log1mlb_ref[...])
    f = jnp.exp(logf)
    eye = _eye(HD)
    for b in range(bb):
        rows = []
        for h in range(NH):
            sl = slice(h * HD, (h + 1) * HD)
            f_col = _to_col(f[b:b + 1, sl], eye)
            k_col = _to_col(k[b:b + 1, sl], eye)
            q_col = _to_col(q[b:b + 1, sl], eye)
            s_new = f_col * s_ref[b, h] + k_col * v[b:b + 1, sl]
            snew_ref[b, h] = s_new
            rows.append(jnp.sum(q_col * s_new, axis=0, keepdims=True))
        o_s[b:b + 1, :] = jnp.concatenate(rows, axis=1)
    o = _head_rms(o_s[...], og_ref[...])
    o_ref[...] = o * _silu(pa[:, 3 * D_BR:4 * D_BR])


def _hgrn2_step(pa, state, log_lb, log1m_lb, onorm_g):
    B = pa.shape[0]
    bb = min(STEP_ROWS, B)
    vec = pl.BlockSpec((1, D_BR), lambda i: (0, 0))
    st = pl.BlockSpec((bb, NH, HD, HD), lambda i: (i, 0, 0, 0))
    return pl.pallas_call(
        functools.partial(_hgrn2_step_kernel, bb=bb),
        grid=(B // bb,),
        in_specs=[pl.BlockSpec((bb, A_W), lambda i: (i, 0)), st, vec, vec, vec],
        out_specs=[pl.BlockSpec((bb, D_BR), lambda i: (i, 0)), st],
        out_shape=[jax.ShapeDtypeStruct((B, D_BR), F32),
                   jax.ShapeDtypeStruct((B, NH, HD, HD), F32)],
        scratch_shapes=[pltpu.VMEM((bb, D_BR), F32)],
        compiler_params=_cparams("parallel"),
        name="hgrn2_step",
    )(pa, state, log_lb, log1m_lb, onorm_g)


def _rwkv_prep(xs, prev, p):
    xm = xs + (prev - xs) * p["mu"]
    r = xm[:, 0:D_BR]
    k = xm[:, D_BR:2 * D_BR]
    v = xm[:, 2 * D_BR:3 * D_BR]
    xw = xm[:, 3 * D_BR:3 * D_BR + LORA]
    xa = xm[:, 3 * D_BR + LORA:3 * D_BR + 2 * LORA]
    w = -_softplus(-(p["w0"] + _bdot(jnp.tanh(xw), p["w2"]))) - 0.5
    logw = -jnp.exp(w)
    a = _sigmoid(p["a0"] + _bdot(xa, p["a2"]))
    kk = k * p["k_k"]
    kk = kk / jnp.maximum(jnp.sqrt(_seg_sum(kk * kk)), 1e-12)
    k = k * (1.0 + (a - 1.0) * p["k_a"])
    bonus = _seg_sum(r * k * p["r_k"]) * v
    return r, logw, k, v, -kk, kk * a, bonus


def _rwkv_post(y, bonus, g, p):
    mean = _seg_sum(y) * (1.0 / HD)
    yc = y - mean
    var = _seg_sum(yc * yc) * (1.0 / HD)
    yn = yc * lax.rsqrt(var + RWKV_GN_EPS) * p["gn_w"] + p["gn_b"]
    return (yn + bonus) * _silu(g)


def _unit_lower_inverse(m, m_t, n):
    nb = INV_BLOCK
    rows = lax.broadcasted_iota(jnp.int32, (nb, 1), 0)
    eye = _eye(nb)
    blocks = []
    for b in range(n // nb):
        mt = m_t[b * nb:(b + 1) * nb, b * nb:(b + 1) * nb]
        d = eye
        for i in range(1, nb):
            row = jnp.sum(mt[:, i:i + 1] * d, axis=0, keepdims=True)
            d = d + jnp.where(rows == i, row, 0.0)
        blocks.append(d)
    size = nb
    while size < n:
        merged = []
        for j in range(0, len(blocks), 2):
            lo, hi = blocks[j], blocks[j + 1]
            r0 = (j + 1) * size
            cross = _rw_inv(hi, _rw_inv(m[r0:r0 + size, r0 - size:r0], lo))
            top = jnp.concatenate([lo, jnp.zeros((size, size), F32)], axis=1)
            bot = jnp.concatenate([cross, hi], axis=1)
            merged.append(jnp.concatenate([top, bot], axis=0))
        blocks = merged
        size *= 2
    return blocks[0]


_rw_score = _bdot_nt
_rw_score_out = _bdot_nt
_rw_inv = _bdot
_rw_solve = _bdot
_rw_read = _bdot_nt
_rw_read_out = _bdot_nt
_rw_apply = _bdot
_rw_apply_out = _bdot
_rw_update = _bdot_tn

_RWKV_PARAMS = ("mu", "w0", "w2", "a0", "a2", "k_k", "k_a", "r_k", "gn_w", "gn_b")


def _rwkv_seq_kernel(pb_ref, shift0_ref, s0_ref, *rest, tb, chunk):
    np_ = len(_RWKV_PARAMS)
    p = {n: r[...] for n, r in zip(_RWKV_PARAMS, rest[:np_])}
    o_ref, sfin_ref, shift_ref = rest[np_:np_ + 3]
    r_s, k_s, v_s, a_s, b_s, w_s, g_s, y_s, bon_s, st_s, carry_s = rest[np_ + 3:]
    step = pl.program_id(1)

    @pl.when(step == 0)
    def _():
        st_s[...] = s0_ref[0]
        carry_s[...] = shift0_ref[0]

    xs = pb_ref[0][:, 0:B_SHIFT_W]
    row = lax.broadcasted_iota(jnp.int32, (tb, 1), 0)
    prev = jnp.where(row == 0, carry_s[...], pltpu.roll(xs, 1, 0))
    carry_s[...] = xs[tb - 1:tb, :]
    r, logw, k, v, av, bv, bonus = _rwkv_prep(xs, prev, p)
    gcum = _chunk_cumsum(logw, chunk)
    bon_s[...] = bonus
    for h in range(NH):
        sl = slice(h * HD, (h + 1) * HD)
        r_s[h] = r[:, sl]
        k_s[h] = k[:, sl]
        v_s[h] = v[:, sl]
        a_s[h] = av[:, sl]
        b_s[h] = bv[:, sl]
        w_s[h] = logw[:, sl]
        g_s[h] = gcum[:, sl]

    ri = lax.broadcasted_iota(jnp.int32, (chunk, chunk), 0)
    ci = lax.broadcasted_iota(jnp.int32, (chunk, chunk), 1)
    strict = ci < ri
    incl = ci <= ri

    def body(c, carry):
        r0 = pl.multiple_of(c * chunk, chunk)
        for h in range(NH):
            ld = lambda ref: ref[h, pl.ds(r0, chunk), :]
            rc, kc, vc, ac, bc, wc, gc = (ld(x) for x in (r_s, k_s, v_s, a_s, b_s, w_s, g_s))
            glast = gc[chunk - 1:chunk, :]
            eg = jnp.exp(gc)
            einv = jnp.exp(-gc)
            etail = jnp.exp(glast - gc)
            a_t = ac * jnp.exp(gc - wc)
            r_t = rc * eg
            b_t = bc * einv
            k_t = kc * einv
            s = st_s[h]
            m_ab = jnp.where(strict, _rw_score(a_t, b_t), 0.0)
            m_ab_t = jnp.where(ri < ci, _rw_score(b_t, a_t), 0.0)
            m_ak = jnp.where(strict, _rw_score(a_t, k_t), 0.0)
            n_rb = jnp.where(incl, _rw_score_out(r_t, b_t), 0.0)
            n_rk = jnp.where(incl, _rw_score_out(r_t, k_t), 0.0)
            inv = _unit_lower_inverse(m_ab, m_ab_t, chunk)
            sa = _rw_solve(inv, _rw_read(a_t, s) + _rw_apply(m_ak, vc))
            y = _rw_read_out(r_t, s) + _rw_apply_out(n_rb, sa) + _rw_apply_out(n_rk, vc)
            st_s[h] = s * jnp.exp(glast) + _rw_update(sa, bc * etail) + _rw_update(vc, kc * etail)
            y_s[pl.ds(r0, chunk), h * HD:(h + 1) * HD] = y
        return carry

    lax.fori_loop(0, tb // chunk, body, 0)

    o_ref[0] = _rwkv_post(y_s[...], bon_s[...], pb_ref[0][:, B_SHIFT_W:B_W], p)

    @pl.when(step == pl.num_programs(1) - 1)
    def _():
        sfin_ref[0] = st_s[...]
        shift_ref[0] = carry_s[...]


def _rwkv_param_specs(params, nargs):
    zero2 = (lambda b, i: (0, 0)) if nargs == 2 else (lambda i: (0, 0))
    return [pl.BlockSpec(x.shape, zero2) for x in params]


def _rwkv_seq(pb, shift0, s0, params):
    B, T, _ = pb.shape
    tb = min(MIX_TILE, T)
    st = pl.BlockSpec((1, NH, HD, HD), lambda b, i: (b, 0, 0, 0))
    sh = pl.BlockSpec((1, 1, B_SHIFT_W), lambda b, i: (b, 0, 0))
    return pl.pallas_call(
        functools.partial(_rwkv_seq_kernel, tb=tb, chunk=CHUNK_B),
        grid=(B, T // tb),
        in_specs=[pl.BlockSpec((1, tb, B_W), lambda b, i: (b, i, 0)), sh, st]
                 + _rwkv_param_specs(params, 2),
        out_specs=[pl.BlockSpec((1, tb, D_BR), lambda b, i: (b, i, 0)), st, sh],
        out_shape=[jax.ShapeDtypeStruct((B, T, D_BR), F32),
                   jax.ShapeDtypeStruct((B, NH, HD, HD), F32),
                   jax.ShapeDtypeStruct((B, 1, B_SHIFT_W), F32)],
        scratch_shapes=[pltpu.VMEM((NH, tb, HD), F32)] * 7
                       + [pltpu.VMEM((tb, D_BR), F32), pltpu.VMEM((tb, D_BR), F32),
                          pltpu.VMEM((NH, HD, HD), F32), pltpu.VMEM((1, B_SHIFT_W), F32)],
        compiler_params=_cparams("parallel", "arbitrary"),
        name="rwkv7_seq",
    )(pb, shift0, s0, *params)


def _rwkv_step_kernel(pb_ref, shift0_ref, s_ref, *rest, bb):
    np_ = len(_RWKV_PARAMS)
    p = {n: r[...] for n, r in zip(_RWKV_PARAMS, rest[:np_])}
    o_ref, snew_ref, y_s = rest[np_:]
    pb = pb_ref[...]
    r, logw, k, v, av, bv, bonus = _rwkv_prep(pb[:, 0:B_SHIFT_W], shift0_ref[...], p)
    dec = jnp.exp(logw)
    eye = _eye(HD)
    for b in range(bb):
        rows = []
        for h in range(NH):
            sl = slice(h * HD, (h + 1) * HD)
            row = lambda x: x[b:b + 1, sl]
            s = s_ref[b, h]
            sa = jnp.sum(s * row(av), axis=1, keepdims=True)
            v_col = _to_col(row(v), eye)
            s_new = s * row(dec) + sa * row(bv) + v_col * row(k)
            snew_ref[b, h] = s_new
            y_col = jnp.sum(s_new * row(r), axis=1, keepdims=True)
            rows.append(_to_row(y_col, eye))
        y_s[b:b + 1, :] = jnp.concatenate(rows, axis=1)
    o_ref[...] = _rwkv_post(y_s[...], bonus, pb[:, B_SHIFT_W:B_W], p)


def _rwkv_step(pb, shift0, state, params):
    B = pb.shape[0]
    bb = min(STEP_ROWS, B)
    st = pl.BlockSpec((bb, NH, HD, HD), lambda i: (i, 0, 0, 0))
    return pl.pallas_call(
        functools.partial(_rwkv_step_kernel, bb=bb),
        grid=(B // bb,),
        in_specs=[pl.BlockSpec((bb, B_W), lambda i: (i, 0)),
                  pl.BlockSpec((bb, B_SHIFT_W), lambda i: (i, 0)), st]
                 + _rwkv_param_specs(params, 1),
        out_specs=[pl.BlockSpec((bb, D_BR), lambda i: (i, 0)), st],
        out_shape=[jax.ShapeDtypeStruct((B, D_BR), F32),
                   jax.ShapeDtypeStruct((B, NH, HD, HD), F32)],
        scratch_shapes=[pltpu.VMEM((bb, D_BR), F32)],
        compiler_params=_cparams("parallel"),
        name="rwkv7_step",
    )(pb, shift0, state, *params)


def _qkv_prep_kernel(pc_ref, cos_ref, sin_ref, qg_ref, kg_ref, *rest, tm):
    nd = len(C_CONFIGS)
    outs = rest[:3 * nd]
    kc_ref = rest[3 * nd]
    scr = rest[3 * nd + 1:]
    pc = pc_ref[0]
    cos = cos_ref[...]
    sin = sin_ref[...]
    q = _rope(_head_rms(pc[:, 0:D_BR], qg_ref[...]), cos, sin) * (HD ** -0.5)
    k = _rope(_head_rms(pc[:, D_BR:2 * D_BR], kg_ref[...]), cos, sin)
    kc_ref[0] = k
    for a, x in enumerate((q, k, pc[:, 2 * D_BR:3 * D_BR])):
        for half in range(D_BR // LANES):
            scr[a][half] = x[:, half * LANES:(half + 1) * LANES]
    hpl = LANES // HD
    for ci, (_, dil) in enumerate(C_CONFIGS):
        for a in range(3):
            out = outs[3 * ci + a]
            for rho in range(dil):
                for half in range(D_BR // LANES):
                    rows = scr[a][half, pl.ds(rho, tm // dil, stride=dil), :]
                    for j in range(hpl):
                        out[0, rho, half * hpl + j] = rows[:, j * HD:(j + 1) * HD].astype(BF16)


def _qkv_prep(pc, cos, sin, qg, kg):
    B, T, _ = pc.shape
    tm = min(ROW_TILE, T)
    vec = pl.BlockSpec((1, D_BR), lambda b, i: (0, 0))
    tab = pl.BlockSpec((tm, D_BR), lambda b, i: (i, 0))
    out_specs, out_shape = [], []
    for _, dil in C_CONFIGS:
        for _ in range(3):
            out_specs.append(pl.BlockSpec((1, dil, NH, tm // dil, HD), lambda b, i: (b, 0, 0, i, 0)))
            out_shape.append(jax.ShapeDtypeStruct((B, dil, NH, T // dil, HD), BF16))
    out_specs.append(pl.BlockSpec((1, tm, D_BR), lambda b, i: (b, i, 0)))
    out_shape.append(jax.ShapeDtypeStruct((B, T, D_BR), F32))
    return pl.pallas_call(
        functools.partial(_qkv_prep_kernel, tm=tm),
        grid=(B, T // tm),
        in_specs=[pl.BlockSpec((1, tm, C_W), lambda b, i: (b, i, 0)), tab, tab, vec, vec],
        out_specs=out_specs,
        out_shape=out_shape,
        scratch_shapes=[pltpu.VMEM((D_BR // LANES, tm, LANES), F32)] * 3,
        compiler_params=_cparams("parallel", "parallel"),
        name="attn_qkv_prep",
    )(pc, cos, sin, qg, kg)


def _swa_kernel(q_ref, kc_ref, vc_ref, kp_ref, vp_ref, o_ref, lse_ref, *, qt):
    i = pl.program_id(1)
    n = Q_BLOCK
    r = lax.broadcasted_iota(jnp.int32, (n, 2 * n), 0)
    c = lax.broadcasted_iota(jnp.int32, (n, 2 * n), 1)
    band = (c >= r) & (c <= r + n)
    first = band & (c >= jnp.where(i > 0, 0, n))
    for u in range(qt // n):
        o_parts, lse_parts = [], []
        for h in range(NH):
            q = q_ref[0, h, u * n:(u + 1) * n, :]
            if u == 0:
                kwin = jnp.concatenate([kp_ref[0, h], kc_ref[0, h, 0:n, :]], axis=0)
                vwin = jnp.concatenate([vp_ref[0, h], vc_ref[0, h, 0:n, :]], axis=0)
                mask = first
            else:
                kwin = kc_ref[0, h, (u - 1) * n:(u + 1) * n, :]
                vwin = vc_ref[0, h, (u - 1) * n:(u + 1) * n, :]
                mask = band
            s = lax.dot_general(q, kwin, (((1,), (1,)), ((), ())), preferred_element_type=F32)
            s = jnp.where(mask, s, MASK_VALUE)
            m = jnp.max(s, axis=1, keepdims=True)
            p = jnp.exp(s - m)
            l = jnp.sum(p, axis=1, keepdims=True)
            o_parts.append(jnp.dot(p.astype(BF16), vwin, preferred_element_type=F32) / l)
            lse_parts.append(jnp.broadcast_to(m + jnp.log(l), (n, HD)))
        o_ref[0, u * n:(u + 1) * n, :] = jnp.concatenate(o_parts, axis=1)
        lse_ref[0, u * n:(u + 1) * n, :] = jnp.concatenate(lse_parts, axis=1)


def _swa(q, k, v):
    N, _, L, _ = q.shape
    n = Q_BLOCK
    qt = min(ATTN_TILE, L)
    cur = pl.BlockSpec((1, NH, qt, HD), lambda s, i: (s, 0, i, 0))
    prev = pl.BlockSpec((1, NH, n, HD), lambda s, i: (s, 0, jnp.maximum(i * (qt // n) - 1, 0), 0))
    out = pl.BlockSpec((1, qt, D_BR), lambda s, i: (s, i, 0))
    return pl.pallas_call(
        functools.partial(_swa_kernel, qt=qt),
        grid=(N, L // qt),
        in_specs=[cur, cur, cur, prev, prev],
        out_specs=[out, out],
        out_shape=[jax.ShapeDtypeStruct((N, L, D_BR), F32)] * 2,
        compiler_params=_cparams("parallel", "parallel"),
        name="sliding_window_attn",
    )(q, k, v, k, v)


def _attn_combine_kernel(*refs, tm):
    nd = len(C_CONFIGS)
    o_refs, lse_refs = refs[0:nd], refs[nd:2 * nd]
    g_ref, out_ref = refs[2 * nd], refs[2 * nd + 1]
    o_s, lse_s = refs[2 * nd + 2], refs[2 * nd + 3]
    os_, lses = [], []
    for ci, (_, dil) in enumerate(C_CONFIGS):
        if dil == 1:
            os_.append(o_refs[ci][0, 0])
            lses.append(lse_refs[ci][0, 0])
            continue
        nh = D_BR // LANES
        for rho in range(dil):
            for half in range(nh):
                ls = slice(half * LANES, (half + 1) * LANES)
                o_s[ci, half, pl.ds(rho, tm // dil, stride=dil), :] = o_refs[ci][0, rho][:, ls]
                lse_s[ci, half, pl.ds(rho, tm // dil, stride=dil), :] = lse_refs[ci][0, rho][:, ls]
        os_.append(jnp.concatenate([o_s[ci, half] for half in range(nh)], axis=1))
        lses.append(jnp.concatenate([lse_s[ci, half] for half in range(nh)], axis=1))
    top = functools.reduce(jnp.maximum, lses)
    ws = [jnp.exp(x - top) for x in lses]
    num = functools.reduce(lambda a, b: a + b, [w * o for w, o in zip(ws, os_)])
    den = functools.reduce(lambda a, b: a + b, ws)
    out_ref[0] = num / den * _silu(g_ref[0])


def _attn_combine(outs, lses, pc):
    B, T, _ = pc.shape
    tm = min(ROW_TILE, T)
    nd = len(C_CONFIGS)
    specs = [pl.BlockSpec((1, dil, tm // dil, D_BR), lambda b, i: (b, 0, i, 0)) for _, dil in C_CONFIGS]
    return pl.pallas_call(
        functools.partial(_attn_combine_kernel, tm=tm),
        grid=(B, T // tm),
        in_specs=specs + specs + [pl.BlockSpec((1, tm, D_BR), lambda b, i: (b, i, 3))],
        out_specs=pl.BlockSpec((1, tm, D_BR), lambda b, i: (b, i, 0)),
        out_shape=jax.ShapeDtypeStruct((B, T, D_BR), F32),
        scratch_shapes=[pltpu.VMEM((nd, D_BR // LANES, tm, LANES), F32)] * 2,
        compiler_params=_cparams("parallel", "parallel"),
        name="attn_combine",
    )(*outs, *lses, pc)


def _attn_seq(pc, cos, sin, qg, kg):
    B, T, _ = pc.shape
    assert T % (Q_BLOCK * max(d for _, d in C_CONFIGS)) == 0
    res = _qkv_prep(pc, cos, sin, qg, kg)
    outs, lses = [], []
    for ci, (w, dil) in enumerate(C_CONFIGS):
        assert w == Q_BLOCK * dil
        q, k, v = (x.reshape(B * dil, NH, T // dil, HD) for x in res[3 * ci:3 * ci + 3])
        o, lse = _swa(q, k, v)
        outs.append(o.reshape(B, dil, T // dil, D_BR))
        lses.append(lse.reshape(B, dil, T // dil, D_BR))
    return _attn_combine(outs, lses, pc), res[-1]


def _attn_step_kernel(pc_ref, cos_ref, sin_ref, qg_ref, kg_ref, mult_ref, kt_ref, vt_ref,
                      o_ref, knew_ref, vnew_ref, *, bb):
    pc = pc_ref[...]
    cos = cos_ref[...]
    sin = sin_ref[...]
    q = _rope(_head_rms(pc[:, 0:D_BR], qg_ref[...]), cos, sin) * (HD ** -0.5)
    k = _rope(_head_rms(pc[:, D_BR:2 * D_BR], kg_ref[...]), cos, sin)
    v = pc[:, 2 * D_BR:3 * D_BR]
    knew_ref[...] = k
    vnew_ref[...] = v
    mult = mult_ref[...]
    live = mult > 0.0
    n_self = float(len(C_CONFIGS))
    eye = _eye(HD)
    for b in range(bb):
        parts = []
        for h in range(NH):
            sl = slice(h * HD, (h + 1) * HD)
            qr, kr, vr = q[b:b + 1, sl], k[b:b + 1, sl], v[b:b + 1, sl]
            s_self = jnp.sum(qr * kr, axis=1, keepdims=True)
            s = jnp.sum(kt_ref[b, h] * _to_col(qr, eye), axis=0, keepdims=True)
            s = jnp.where(live, s, MASK_VALUE)
            m = jnp.maximum(jnp.max(s, axis=1, keepdims=True), s_self)
            p = jnp.exp(s - m) * mult
            p_self = n_self * jnp.exp(s_self - m)
            den = jnp.sum(p, axis=1, keepdims=True) + p_self
            acc = _to_row(jnp.sum(vt_ref[b, h] * p, axis=1, keepdims=True), eye) + p_self * vr
            parts.append(acc / den)
        o_ref[b:b + 1, :] = jnp.concatenate(parts, axis=1)
    o_ref[...] = o_ref[...] * _silu(pc[:, 3 * D_BR:4 * D_BR])


def _attn_step(pc, cos, sin, qg, kg, cache_kt, cache_vt, layer):
    B = pc.shape[0]
    P = cache_kt.shape[-1]
    bb = min(ATTN_STEP_ROWS, B)
    dist = P - jnp.arange(P)
    mult = jnp.zeros((P,), F32)
    for w, dil in C_CONFIGS:
        mult = mult + ((dist <= w) & (dist % dil == 0)).astype(F32)
    vec = pl.BlockSpec((1, D_BR), lambda i: (0, 0))
    row = pl.BlockSpec((None, bb, D_BR), lambda i: (i, 0, 0))
    cache = pl.BlockSpec((None, bb, NH, HD, P), lambda i: (layer, i, 0, 0, 0))
    outs = pl.pallas_call(
        functools.partial(_attn_step_kernel, bb=bb),
        grid=(B // bb,),
        in_specs=[pl.BlockSpec((None, bb, C_W), lambda i: (i, 0, 0)), vec, vec, vec, vec,
                  pl.BlockSpec((1, P), lambda i: (0, 0)), cache, cache],
        out_specs=[row, row, row],
        out_shape=[jax.ShapeDtypeStruct((B // bb, bb, D_BR), F32)] * 3,
        compiler_params=_cparams("parallel"),
        name="dilated_attn_step",
    )(pc.reshape(B // bb, bb, C_W), cos, sin, qg, kg, mult.reshape(1, P), cache_kt, cache_vt)
    return [o.reshape(B, D_BR) for o in outs]


def _pool_mix(pooled, u, g, w_ref, scale_ref):
    mixed = _bdot(pooled - u, w_ref[...]) * scale_ref[...]
    return mixed * _silu(g)


def _pool_select(sums, cnts):
    lane = lax.broadcasted_iota(jnp.int32, sums[0].shape, 1) // (D_BR // len(POOL_WINDOWS))
    out = sums[-1] / cnts[-1]
    for gi in range(len(POOL_WINDOWS) - 2, -1, -1):
        out = jnp.where(lane == gi, sums[gi] / cnts[gi], out)
    return out


def _pool_seq_kernel(pd_ref, buf0_ref, w_ref, scale_ref, o_ref, carry_s, *, tb, t0):
    step = pl.program_id(1)

    @pl.when(step == 0)
    def _():
        carry_s[...] = buf0_ref[0]

    pd = pd_ref[0]
    u = pd[:, 0:D_BR]
    ext = jnp.concatenate([carry_s[...], u], axis=0)
    carry_s[...] = u[tb - POOL_MAX:tb, :]
    pos = (t0 + step * tb + lax.broadcasted_iota(jnp.int32, (tb, 1), 0)).astype(F32)
    sums, cnts = [], []
    run = ext
    shift = 1
    for w in POOL_WINDOWS:
        while shift < w:
            run = run + pltpu.roll(run, shift, 0)
            shift *= 2
        sums.append(run[POOL_MAX:, :])
        cnts.append(jnp.minimum(float(w), pos + 1.0))
    o_ref[0] = _pool_mix(_pool_select(sums, cnts), u, pd[:, D_BR:2 * D_BR], w_ref, scale_ref)


def _pool_seq(pd, buf0, w_bd, scale, t0):
    B, T, _ = pd.shape
    tb = min(ROW_TILE, T)
    return pl.pallas_call(
        functools.partial(_pool_seq_kernel, tb=tb, t0=t0),
        grid=(B, T // tb),
        in_specs=[pl.BlockSpec((1, tb, D_W), lambda b, i: (b, i, 0)),
                  pl.BlockSpec((1, POOL_MAX, D_BR), lambda b, i: (b, 0, 0)),
                  pl.BlockSpec((D_BR, D_BR), lambda b, i: (0, 0)),
                  pl.BlockSpec((1, D_BR), lambda b, i: (0, 0))],
        out_specs=pl.BlockSpec((1, tb, D_BR), lambda b, i: (b, i, 0)),
        out_shape=jax.ShapeDtypeStruct((B, T, D_BR), F32),
        scratch_shapes=[pltpu.VMEM((POOL_MAX, D_BR), F32)],
        compiler_params=_cparams("parallel", "arbitrary"),
        name="pool_seq",
    )(pd, buf0, w_bd, scale)


def _pool_step_kernel(pd_ref, buf_ref, w_ref, scale_ref, o_ref, *, t0):
    pd = pd_ref[...]
    u = pd[:, 0:D_BR]
    nbuf = POOL_MAX - 1
    sums, cnts = [], []
    run = u
    taken = 1
    for w in POOL_WINDOWS:
        while taken < w:
            run = run + buf_ref[:, nbuf - taken, :]
            taken += 1
        sums.append(run)
        cnts.append(float(min(w, t0 + 1)))
    o_ref[...] = _pool_mix(_pool_select(sums, cnts), u, pd[:, D_BR:2 * D_BR], w_ref, scale_ref)


def _pool_step(pd, buf, w_bd, scale, t0):
    B = pd.shape[0]
    bb = min(POOL_STEP_ROWS, B)
    return pl.pallas_call(
        functools.partial(_pool_step_kernel, t0=t0),
        grid=(B // bb,),
        in_specs=[pl.BlockSpec((bb, D_W), lambda i: (i, 0)),
                  pl.BlockSpec((bb, POOL_MAX - 1, D_BR), lambda i: (i, 0, 0)),
                  pl.BlockSpec((D_BR, D_BR), lambda i: (0, 0)),
                  pl.BlockSpec((1, D_BR), lambda i: (0, 0))],
        out_specs=pl.BlockSpec((bb, D_BR), lambda i: (i, 0)),
        out_shape=jax.ShapeDtypeStruct((B, D_BR), F32),
        compiler_params=_cparams("parallel"),
        name="pool_step",
    )(pd, buf, w_bd, scale)


def _rope_tables(pos):
    half = HD // 2
    inv_freq = jnp.float32(ROPE_THETA) ** (-jnp.arange(half, dtype=F32) / half)
    ang = pos.astype(F32)[:, None] * inv_freq[None, :]
    cos = jnp.cos(ang)
    sin = jnp.sin(ang)
    cos_t = jnp.tile(jnp.concatenate([cos, cos], axis=-1), (1, NH))
    sin_t = jnp.tile(jnp.concatenate([-sin, sin], axis=-1), (1, NH))
    return cos_t, sin_t


def _block_diag(w):
    g, c, d = w.shape
    out = jnp.zeros((g * c, g * d), w.dtype)
    for i in range(g):
        out = out.at[i * c:(i + 1) * c, i * d:(i + 1) * d].set(w[i])
    return out


def _layer_params(l, W):
    row = lambda x: x[l].reshape(1, -1)
    w_in = W["w_in"][l].astype(BF16)
    cuts = (0, A_W, A_W + B_W, A_W + B_W + C_W, A_W + B_W + C_W + D_W)
    return dict(
        norm_g=W["norm_g"][l],
        w_parts=[w_in[:, cuts[i]:cuts[i + 1]] for i in range(4)],
        w_out=W["w_out"][l].astype(BF16),
        onorm_g=row(W["a_onorm_g"]),
        rwkv=[row(W["b_mu"]), row(W["b_w0"]), W["b_w2"][l], row(W["b_a0"]), W["b_a2"][l],
              row(W["b_k_k"]), row(W["b_k_a"]), row(W["b_r_k"]), row(W["b_gn_w"]), row(W["b_gn_b"])],
        qg=jnp.tile(W["c_qnorm_g"][l], NH).reshape(1, D_BR),
        kg=jnp.tile(W["c_knorm_g"][l], NH).reshape(1, D_BR),
        w_pool=_block_diag(W["d_w_pool"][l]),
        p_scale=row(W["d_scale"]),
    )


def _run_prompt(x, mods, layers, log_lb, log1m_lb):
    B, T, _ = x.shape
    cos, sin = _rope_tables(jnp.arange(T))
    c_keep = min(C_WIN_MAX, T)
    d_keep = min(POOL_MAX - 1, T)
    zero_state = jnp.zeros((B, NH, HD, HD), F32)
    zero_shift = jnp.zeros((B, 1, B_SHIFT_W), F32)
    zero_pool = jnp.zeros((B, POOL_MAX, D_BR), F32)
    nA, nBw, nBs, nK, nV, nD = [], [], [], [], [], []
    for l, L in enumerate(layers):
        shift, scale, gate = (mods[l][:, None, i * D_MODEL:(i + 1) * D_MODEL] for i in range(3))
        pa, pb, pc, pd = _in_proj(x, shift, scale, L["norm_g"], L["w_parts"])
        oa, sa_t = _hgrn2_seq(pa, zero_state, log_lb[l], log1m_lb[l], L["onorm_g"])
        ob, sb, sbs = _rwkv_seq(pb, zero_shift, zero_state, L["rwkv"])
        oc, k_rot = _attn_seq(pc, cos, sin, L["qg"], L["kg"])
        od = _pool_seq(pd, zero_pool, L["w_pool"], L["p_scale"], 0)
        x = _out_proj(oa, ob, oc, od, x, gate, L["w_out"])
        nA.append(jnp.swapaxes(sa_t, -1, -2))
        nBw.append(sb)
        nBs.append(sbs[:, 0])
        nK.append(k_rot[:, T - c_keep:].reshape(B, c_keep, NH, HD))
        nV.append(pc[:, T - c_keep:, 2 * D_BR:3 * D_BR].reshape(B, c_keep, NH, HD))
        nD.append(pd[:, T - d_keep:, 0:D_BR])
    return (x,) + tuple(jnp.stack(a) for a in (nA, nBw, nBs, nK, nV, nD))


def _run_sample(x, mods, layers, log_lb, log1m_lb, sA, sBw, sBs, ck, cv, cd):
    B, T, _ = x.shape
    assert T == 1 and ck.shape[2] == PAST_LEN == C_WIN_MAX
    cos, sin = _rope_tables(jnp.full((1,), PAST_LEN))
    ck_t = jnp.transpose(ck, (0, 1, 3, 4, 2))
    cv_t = jnp.transpose(cv, (0, 1, 3, 4, 2))
    xr = x.reshape(1, B, D_MODEL)
    nA, nBw, nBs, nK, nV, nD = [], [], [], [], [], []
    for l, L in enumerate(layers):
        shift, scale, gate = (mods[l][None, :, i * D_MODEL:(i + 1) * D_MODEL] for i in range(3))
        pa, pb, pc, pd = (p[0] for p in _in_proj(xr, shift, scale, L["norm_g"], L["w_parts"]))
        oa, sa = _hgrn2_step(pa, sA[l], log_lb[l], log1m_lb[l], L["onorm_g"])
        ob, sb = _rwkv_step(pb, sBs[l], sBw[l], L["rwkv"])
        oc, k_new, v_new = _attn_step(pc, cos, sin, L["qg"], L["kg"], ck_t, cv_t, l)
        od = _pool_step(pd, cd[l], L["w_pool"], L["p_scale"], PAST_LEN)
        xr = _out_proj(oa[None], ob[None], oc[None], od[None], xr, gate, L["w_out"])
        nA.append(sa)
        nBw.append(sb)
        nBs.append(pb[:, 0:B_SHIFT_W])
        nK.append(k_new.reshape(B, 1, NH, HD))
        nV.append(v_new.reshape(B, 1, NH, HD))
        nD.append(pd[:, None, 0:D_BR])
    return (xr.reshape(B, 1, D_MODEL),) + tuple(jnp.stack(a) for a in (nA, nBw, nBs, nK, nV, nD))


def kernel(x_prompt, x_sample, state_A, state_B_wkv, state_B_shift, cache_C_k, cache_C_v, cache_D_pool, c_prompt, c_sample, ada_w, ada_b, norm_g, w_in, w_out, a_lb_logits, a_onorm_g, b_mu, b_w0, b_w2, b_a0, b_a2, b_k_k, b_k_a, b_r_k, b_gn_w, b_gn_b, c_qnorm_g, c_knorm_g, d_w_pool, d_scale):
    W = dict(norm_g=norm_g, w_in=w_in, w_out=w_out, a_onorm_g=a_onorm_g, b_mu=b_mu, b_w0=b_w0,
             b_w2=b_w2, b_a0=b_a0, b_a2=b_a2, b_k_k=b_k_k, b_k_a=b_k_a, b_r_k=b_r_k,
             b_gn_w=b_gn_w, b_gn_b=b_gn_b, c_qnorm_g=c_qnorm_g, c_knorm_g=c_knorm_g,
             d_w_pool=d_w_pool, d_scale=d_scale)
    layers = [_layer_params(l, W) for l in range(DEPTH)]
    lb_sm = jax.nn.softmax(a_lb_logits.astype(F32), axis=0)
    lower = jnp.cumsum(lb_sm, axis=0) - lb_sm[0:1]
    log_lb = jnp.log(jnp.maximum(lower, LB_FLOOR))[:, None, :]
    log1m_lb = jnp.log1p(-lower)[:, None, :]

    nb = x_prompt.shape[0]
    mods = _modulation(jnp.concatenate([c_prompt, c_sample], axis=0), ada_w, ada_b)
    y_p, pA, pBw, pBs, pK, pV, pD = _run_prompt(x_prompt, mods[:, :nb], layers, log_lb, log1m_lb)
    y_s, sA, sBw, sBs, sK, sV, sD = _run_sample(
        x_sample, mods[:, nb:], layers, log_lb, log1m_lb,
        state_A, state_B_wkv, state_B_shift, cache_C_k, cache_C_v, cache_D_pool)
    return (y_p, y_s, pA, sA, pBw, sBw, pBs, sBs, pK, sK, pV, sV, pD, sD)
```

```python
import functools

import jax
import jax.numpy as jnp
from jax import lax
from jax.experimental import pallas as pl
from jax.experimental.pallas import tpu as pltpu

F32 = jnp.float32
BF16 = jnp.bfloat16

D_MODEL = 1024
DEPTH = 4
PAST_LEN = 2048
D_BR = 256
HD = 64
NH = D_BR // HD
LB_FLOOR = 1e-30
LORA = 32
RWKV_GN_EPS = 64e-5
C_CONFIGS = ((128, 1), (512, 4), (2048, 16))
C_WIN_MAX = 2048
MASK_VALUE = -1e30
ROPE_THETA = 10000.0
POOL_WINDOWS = (2, 4, 8, 16)
POOL_MAX = 16
EPS = 1e-6
A_W = 4 * D_BR
B_SHIFT_W = 3 * D_BR + 2 * LORA
B_W = B_SHIFT_W + D_BR
C_W = 4 * D_BR
D_W = 2 * D_BR

LANES = 128

ROW_TILE = 512
MIX_TILE = 256
CHUNK_A = 8
CHUNKS_PER_IT_A = 8
CHUNK_B = 64
INV_BLOCK = 16
CHUNKS_PER_PASS = 4
Q_BLOCK = 128
ATTN_TILE = 256
STEP_ROWS = 8
ATTN_STEP_ROWS = 2
POOL_STEP_ROWS = 32
VMEM_LIMIT = 56 * 1024 * 1024


def _cparams(*sem):
    return pltpu.CompilerParams(dimension_semantics=sem, vmem_limit_bytes=VMEM_LIMIT)


def _sigmoid(x):
    return 1.0 / (1.0 + jnp.exp(-x))


def _silu(x):
    return x * _sigmoid(x)


def _softplus(x):
    return jnp.maximum(x, 0.0) + jnp.log1p(jnp.exp(-jnp.abs(x)))


def _log_sigmoid(x):
    return -_softplus(-x)


def _logaddexp(a, b):
    return jnp.maximum(a, b) + jnp.log1p(jnp.exp(-jnp.abs(a - b)))


def _bdot(a, b):
    return jnp.dot(a.astype(BF16), b.astype(BF16), preferred_element_type=F32)


def _bdot_nt(a, b):
    return lax.dot_general(a.astype(BF16), b.astype(BF16), (((1,), (1,)), ((), ())),
                           preferred_element_type=F32)


def _bdot_tn(a, b):
    return lax.dot_general(a.astype(BF16), b.astype(BF16), (((0,), (0,)), ((), ())),
                           preferred_element_type=F32)


def _split3(x):
    h1 = x.astype(BF16)
    r1 = x - h1.astype(F32)
    h2 = r1.astype(BF16)
    r2 = r1 - h2.astype(F32)
    return h1, h2, r2.astype(BF16)


def _seg_sum(x):
    n = x.shape[-1]
    r = lax.broadcasted_iota(jnp.int32, (n, n), 0) // HD
    c = lax.broadcasted_iota(jnp.int32, (n, n), 1) // HD
    ones = jnp.where(r == c, 1.0, 0.0).astype(BF16)
    h1, h2, h3 = _split3(x)
    d = lambda h: jnp.dot(h, ones, preferred_element_type=F32)
    return d(h1) + d(h2) + d(h3)


def _chunk_cumsum(x, chunk):
    t = x.shape[0]
    r = lax.broadcasted_iota(jnp.int32, (t, t), 0)
    c = lax.broadcasted_iota(jnp.int32, (t, t), 1)
    tri = jnp.where((r // chunk == c // chunk) & (c <= r), 1.0, 0.0).astype(BF16)
    h1, h2, h3 = _split3(x)
    d = lambda h: jnp.dot(tri, h, preferred_element_type=F32)
    return d(h1) + d(h2) + d(h3)


def _head_rms(x, gamma):
    ms = _seg_sum(x * x) * (1.0 / HD)
    return x * lax.rsqrt(ms + EPS) * gamma


def _rope(x, cos, sin_signed):
    lane = lax.broadcasted_iota(jnp.int32, x.shape, 1) % HD
    partner = jnp.where(lane < HD // 2,
                        pltpu.roll(x, D_BR - HD // 2, 1),
                        pltpu.roll(x, HD // 2, 1))
    return x * cos + partner * sin_signed


def _eye(n):
    r = lax.broadcasted_iota(jnp.int32, (n, n), 0)
    c = lax.broadcasted_iota(jnp.int32, (n, n), 1)
    return jnp.where(r == c, 1.0, 0.0).astype(F32)


def _to_col(row, eye):
    return jnp.sum(eye * row, axis=1, keepdims=True)


def _to_row(col, eye):
    return jnp.sum(eye * col, axis=0, keepdims=True)


def _mod_kernel(c_ref, w_ref, b_ref, o_ref):
    c = _silu(c_ref[...])
    o_ref[0] = _bdot(c, w_ref[0]) + b_ref[0]


def _modulation(c_all, ada_w, ada_b):
    n = c_all.shape[0]
    tn = 512
    return pl.pallas_call(
        _mod_kernel,
        grid=(DEPTH, 3 * D_MODEL // tn),
        in_specs=[pl.BlockSpec((n, D_MODEL), lambda l, j: (0, 0)),
                  pl.BlockSpec((1, D_MODEL, tn), lambda l, j: (l, 0, j)),
                  pl.BlockSpec((1, 1, tn), lambda l, j: (l, 0, j))],
        out_specs=pl.BlockSpec((1, n, tn), lambda l, j: (l, 0, j)),
        out_shape=jax.ShapeDtypeStruct((DEPTH, n, 3 * D_MODEL), F32),
        compiler_params=_cparams("parallel", "parallel"),
        name="adaln_modulation",
    )(c_all, ada_w, ada_b.reshape(DEPTH, 1, 3 * D_MODEL))


def _in_proj_kernel(x_ref, shift_ref, scale_ref, g_ref, wa_ref, wb_ref, wc_ref, wd_ref,
                    pa_ref, pb_ref, pc_ref, pd_ref):
    x = x_ref[0]
    ms = jnp.mean(x * x, axis=-1, keepdims=True)
    h = x * lax.rsqrt(ms + EPS) * g_ref[...]
    h = (h * (1.0 + scale_ref[0]) + shift_ref[0]).astype(BF16)
    pa_ref[0] = jnp.dot(h, wa_ref[...], preferred_element_type=F32)
    pb_ref[0] = jnp.dot(h, wb_ref[...], preferred_element_type=F32)
    pc_ref[0] = jnp.dot(h, wc_ref[...], preferred_element_type=F32)
    pd_ref[0] = jnp.dot(h, wd_ref[...], preferred_element_type=F32)


def _in_proj(x, shift, scale, norm_g, w_parts):
    B, T, D = x.shape
    tm = min(ROW_TILE, T)
    mt = shift.shape[1]
    mod_block = (1, tm, D) if mt == T else (1, 1, D)
    mod_map = (lambda b, i: (b, i, 0)) if mt == T else (lambda b, i: (b, 0, 0))
    widths = [w.shape[1] for w in w_parts]
    return pl.pallas_call(
        _in_proj_kernel,
        grid=(B, T // tm),
        in_specs=[pl.BlockSpec((1, tm, D), lambda b, i: (b, i, 0)),
                  pl.BlockSpec(mod_block, mod_map),
                  pl.BlockSpec(mod_block, mod_map),
                  pl.BlockSpec((1, D), lambda b, i: (0, 0))]
                 + [pl.BlockSpec((D, w), lambda b, i: (0, 0)) for w in widths],
        out_specs=[pl.BlockSpec((1, tm, w), lambda b, i: (b, i, 0)) for w in widths],
        out_shape=[jax.ShapeDtypeStruct((B, T, w), F32) for w in widths],
        compiler_params=_cparams("parallel", "parallel"),
        name="norm_in_proj",
    )(x, shift, scale, norm_g.reshape(1, D), *w_parts)


def _out_proj_kernel(oa_ref, ob_ref, oc_ref, od_ref, x_ref, gate_ref, w_ref, y_ref):
    mix = jnp.concatenate([oa_ref[0], ob_ref[0], oc_ref[0], od_ref[0]], axis=-1).astype(BF16)
    res = jnp.dot(mix, w_ref[...], preferred_element_type=F32)
    y_ref[0] = x_ref[0] + gate_ref[0] * res


def _out_proj(oa, ob, oc, od, x, gate, w_out):
    B, T, D = x.shape
    tm = min(ROW_TILE, T)
    mt = gate.shape[1]
    mod_block = (1, tm, D) if mt == T else (1, 1, D)
    mod_map = (lambda b, i: (b, i, 0)) if mt == T else (lambda b, i: (b, 0, 0))
    br = pl.BlockSpec((1, tm, D_BR), lambda b, i: (b, i, 0))
    return pl.pallas_call(
        _out_proj_kernel,
        grid=(B, T // tm),
        in_specs=[br, br, br, br,
                  pl.BlockSpec((1, tm, D), lambda b, i: (b, i, 0)),
                  pl.BlockSpec(mod_block, mod_map),
                  pl.BlockSpec((D, D), lambda b, i: (0, 0))],
        out_specs=pl.BlockSpec((1, tm, D), lambda b, i: (b, i, 0)),
        out_shape=jax.ShapeDtypeStruct((B, T, D), F32),
        compiler_params=_cparams("parallel", "parallel"),
        name="out_proj_residual",
    )(oa, ob, oc, od, x, gate, w_out)


def _hgrn2_gates(pa, log_lb, log1m_lb):
    q = _silu(pa[:, 0:D_BR])
    fl = pa[:, D_BR:2 * D_BR]
    v = pa[:, 2 * D_BR:3 * D_BR]
    ls_neg = _log_sigmoid(-fl)
    logf = _logaddexp(_log_sigmoid(fl), log_lb + ls_neg)
    k = jnp.exp(log1m_lb + ls_neg)
    return q, logf, k, v


def _hgrn2_seq_kernel(pa_ref, s0_ref, loglb_ref, log1mlb_ref, og_ref, o_ref, sfin_ref,
                      q_s, k_s, v_s, g_s, o_s, st_s, *, tb, chunk, per_it):
    step = pl.program_id(1)
    hpl = LANES // HD
    npair = NH // hpl
    rl = lax.broadcasted_iota(jnp.int32, (LANES, LANES), 0) // HD
    cl = lax.broadcasted_iota(jnp.int32, (LANES, LANES), 1) // HD
    same_head = rl == cl

    @pl.when(step == 0)
    def _():
        for p in range(npair):
            blocks = [jnp.concatenate([s0_ref[0, p * hpl + a] if a == b else jnp.zeros((HD, HD), F32)
                                       for b in range(hpl)], axis=1) for a in range(hpl)]
            st_s[p] = jnp.concatenate(blocks, axis=0)

    q, logf, k, v = _hgrn2_gates(pa_ref[0], loglb_ref[...], log1mlb_ref[...])
    gcum = _chunk_cumsum(logf, chunk)
    for p in range(npair):
        sl = slice(p * LANES, (p + 1) * LANES)
        q_s[p] = q[:, sl]
        k_s[p] = k[:, sl]
        v_s[p] = v[:, sl]
        g_s[p] = gcum[:, sl]

    rows = lax.broadcasted_iota(jnp.int32, (chunk, 1), 0)
    lane_head = lax.broadcasted_iota(jnp.int32, (chunk, LANES), 1) // HD

    def body(it, carry):
        units = [(j, p) for j in range(per_it) for p in range(npair)]
        starts = [pl.multiple_of((it * per_it + j) * chunk, chunk) for j in range(per_it)]
        ld = lambda ref: [ref[p, pl.ds(starts[j], chunk), :] for j, p in units]
        qc, kc, vc, gc = ld(q_s), ld(k_s), ld(v_s), ld(g_s)
        glast = [g[chunk - 1:chunk, :] for g in gc]
        upd = [jnp.where(same_head, _bdot_tn(v_, k_ * jnp.exp(gl - g)), 0.0)
               for v_, k_, gl, g in zip(vc, kc, glast, gc)]
        decay = [jnp.exp(gl) for gl in glast]
        qe = [q_ * jnp.exp(g) for q_, g in zip(qc, gc)]
        o_acc = [jnp.zeros((chunk, LANES), F32) for _ in units]
        for t in range(chunk):
            x = [jnp.exp(jnp.minimum(g[t:t + 1, :] - g, 0.0)) * k_ * q_[t:t + 1, :]
                 for g, k_, q_ in zip(gc, kc, qc)]
            sel = [(lane_head == a) & (rows <= t) for a in range(hpl)]
            att = []
            for xi in x:
                sums = [jnp.sum(jnp.where(m, xi, 0.0), axis=1, keepdims=True) for m in sel]
                a_ = sums[-1]
                for a in range(hpl - 2, -1, -1):
                    a_ = jnp.where(lane_head == a, sums[a], a_)
                att.append(a_)
            ot = [jnp.sum(a * v_, axis=0, keepdims=True) for a, v_ in zip(att, vc)]
            o_acc = [jnp.where(rows == t, r, o) for o, r in zip(o_acc, ot)]
        st = [st_s[p] for p in range(npair)]
        for i, (j, p) in enumerate(units):
            o_s[pl.ds(starts[j], chunk), p * LANES:(p + 1) * LANES] = o_acc[i] + _bdot_nt(qe[i], st[p])
            st[p] = st[p] * decay[i] + upd[i]
        for p in range(npair):
            st_s[p] = st[p]
        return carry

    lax.fori_loop(0, tb // (chunk * per_it), body, 0)

    o = _head_rms(o_s[...], og_ref[...])
    o_ref[0] = o * _silu(pa_ref[0][:, 3 * D_BR:4 * D_BR])

    @pl.when(step == pl.num_programs(1) - 1)
    def _():
        for h in range(NH):
            a = (h % hpl) * HD
            sfin_ref[0, h] = st_s[h // hpl][a:a + HD, a:a + HD]


def _hgrn2_seq(pa, s0_t, log_lb, log1m_lb, onorm_g):
    B, T, _ = pa.shape
    tb = min(MIX_TILE, T)
    vec = pl.BlockSpec((1, D_BR), lambda b, i: (0, 0))
    st = pl.BlockSpec((1, NH, HD, HD), lambda b, i: (b, 0, 0, 0))
    return pl.pallas_call(
        functools.partial(_hgrn2_seq_kernel, tb=tb, chunk=CHUNK_A, per_it=CHUNKS_PER_IT_A),
        grid=(B, T // tb),
        in_specs=[pl.BlockSpec((1, tb, A_W), lambda b, i: (b, i, 0)), st, vec, vec, vec],
        out_specs=[pl.BlockSpec((1, tb, D_BR), lambda b, i: (b, i, 0)), st],
        out_shape=[jax.ShapeDtypeStruct((B, T, D_BR), F32),
                   jax.ShapeDtypeStruct((B, NH, HD, HD), F32)],
        scratch_shapes=[pltpu.VMEM((D_BR // LANES, tb, LANES), F32)] * 4
                       + [pltpu.VMEM((tb, D_BR), F32), pltpu.VMEM((D_BR // LANES, LANES, LANES), F32)],
        compiler_params=_cparams("parallel", "arbitrary"),
        name="hgrn2_seq",
    )(pa, s0_t, log_lb, log1m_lb, onorm_g)


def _hgrn2_step_kernel(pa_ref, s_ref, loglb_ref, log1mlb_ref, og_ref, o_ref, snew_ref, o_s, *, bb):
    pa = pa_ref[...]
    q, logf, k, v = _hgrn2_gates(pa, loglb_ref[...], log1mlb_ref[...])
    f = jnp.exp(logf)
    eye = _eye(HD)
    for b in range(bb):
        rows = []
        for h in range(NH):
            sl = slice(h * HD, (h + 1) * HD)
            f_col = _to_col(f[b:b + 1, sl], eye)
            k_col = _to_col(k[b:b + 1, sl], eye)
            q_col = _to_col(q[b:b + 1, sl], eye)
            s_new = f_col * s_ref[b, h] + k_col * v[b:b + 1, sl]
            snew_ref[b, h] = s_new
            rows.append(jnp.sum(q_col * s_new, axis=0, keepdims=True))
        o_s[b:b + 1, :] = jnp.concatenate(rows, axis=1)
    o = _head_rms(o_s[...], og_ref[...])
    o_ref[...] = o * _silu(pa[:, 3 * D_BR:4 * D_BR])


def _hgrn2_step(pa, state, log_lb, log1m_lb, onorm_g):
    B = pa.shape[0]
    bb = min(STEP_ROWS, B)
    vec = pl.BlockSpec((1, D_BR), lambda i: (0, 0))
    st = pl.BlockSpec((bb, NH, HD, HD), lambda i: (i, 0, 0, 0))
    return pl.pallas_call(
        functools.partial(_hgrn2_step_kernel, bb=bb),
        grid=(B // bb,),
        in_specs=[pl.BlockSpec((bb, A_W), lambda i: (i, 0)), st, vec, vec, vec],
        out_specs=[pl.BlockSpec((bb, D_BR), lambda i: (i, 0)), st],
        out_shape=[jax.ShapeDtypeStruct((B, D_BR), F32),
                   jax.ShapeDtypeStruct((B, NH, HD, HD), F32)],
        scratch_shapes=[pltpu.VMEM((bb, D_BR), F32)],
        compiler_params=_cparams("parallel"),
        name="hgrn2_step",
    )(pa, state, log_lb, log1m_lb, onorm_g)


def _rwkv_prep(xs, prev, p):
    xm = xs + (prev - xs) * p["mu"]
    r = xm[:, 0:D_BR]
    k = xm[:, D_BR:2 * D_BR]
    v = xm[:, 2 * D_BR:3 * D_BR]
    xw = xm[:, 3 * D_BR:3 * D_BR + LORA]
    xa = xm[:, 3 * D_BR + LORA:3 * D_BR + 2 * LORA]
    w = -_softplus(-(p["w0"] + _bdot(jnp.tanh(xw), p["w2"]))) - 0.5
    logw = -jnp.exp(w)
    a = _sigmoid(p["a0"] + _bdot(xa, p["a2"]))
    kk = k * p["k_k"]
    kk = kk / jnp.maximum(jnp.sqrt(_seg_sum(kk * kk)), 1e-12)
    k = k * (1.0 + (a - 1.0) * p["k_a"])
    bonus = _seg_sum(r * k * p["r_k"]) * v
    return r, logw, k, v, -kk, kk * a, bonus


def _rwkv_post(y, bonus, g, p):
    mean = _seg_sum(y) * (1.0 / HD)
    yc = y - mean
    var = _seg_sum(yc * yc) * (1.0 / HD)
    yn = yc * lax.rsqrt(var + RWKV_GN_EPS) * p["gn_w"] + p["gn_b"]
    return (yn + bonus) * _silu(g)


def _unit_lower_inverse(m, m_t, n):
    nb = INV_BLOCK
    rows = lax.broadcasted_iota(jnp.int32, (nb, 1), 0)
    eye = _eye(nb)
    nm = len(m)
    per = n // nb
    group = lax.broadcasted_iota(jnp.int32, (nb, n), 1) // nb
    diag_t = []
    for q in range(nm):
        dq = m_t[q][0:nb, :]
        for b in range(1, per):
            dq = jnp.where(group == b, m_t[q][b * nb:(b + 1) * nb, :], dq)
        diag_t.append(dq)
    qpl = LANES // n
    tiles = [jnp.concatenate(diag_t[t * qpl:(t + 1) * qpl], axis=1) for t in range(nm // qpl)]
    lane = lax.broadcasted_iota(jnp.int32, (nb, LANES), 1)
    base = lane - lane % nb
    ds = [jnp.where(lane % nb == rows, 1.0, 0.0).astype(F32)] * len(tiles)
    for i in range(1, nb):
        coef = [jnp.take_along_axis(t, base + i, axis=1) for t in tiles]
        rws = [jnp.sum(cf * d, axis=0, keepdims=True) for cf, d in zip(coef, ds)]
        ds = [d + jnp.where(rows == i, rw, 0.0) for d, rw in zip(ds, rws)]
    blocks = [[ds[q // qpl][:, (q % qpl) * n + b * nb:(q % qpl) * n + (b + 1) * nb] for b in range(per)]
              for q in range(nm)]
    size = nb
    while size < n:
        pairs = [(q, j) for q in range(nm) for j in range(0, len(blocks[q]), 2)]
        inner = [_rw_inv(m[q][(j + 1) * size:(j + 2) * size, j * size:(j + 1) * size], blocks[q][j])
                 for q, j in pairs]
        cross = [_rw_inv(blocks[q][j + 1], t) for (q, j), t in zip(pairs, inner)]
        merged = [[] for _ in range(nm)]
        for (q, j), cr in zip(pairs, cross):
            top = jnp.concatenate([blocks[q][j], jnp.zeros((size, size), F32)], axis=1)
            bot = jnp.concatenate([cr, blocks[q][j + 1]], axis=1)
            merged[q].append(jnp.concatenate([top, bot], axis=0))
        blocks = merged
        size *= 2
    return [b[0] for b in blocks]


_rw_score = _bdot_nt
_rw_score_out = _bdot_nt
_rw_inv = _bdot
_rw_solve = _bdot
_rw_read = _bdot_nt
_rw_read_out = _bdot_nt
_rw_apply = _bdot
_rw_apply_out = _bdot
_rw_update = _bdot_tn
_rw_state = _bdot

_RWKV_PARAMS = ("mu", "w0", "w2", "a0", "a2", "k_k", "k_a", "r_k", "gn_w", "gn_b")


def _rwkv_seq_kernel(pb_ref, shift0_ref, s0_ref, *rest, tb, chunk):
    np_ = len(_RWKV_PARAMS)
    p = {n: r[...] for n, r in zip(_RWKV_PARAMS, rest[:np_])}
    o_ref, sfin_ref, shift_ref = rest[np_:np_ + 3]
    r_s, k_s, v_s, a_s, b_s, w_s, g_s, y_s, bon_s, st_s, carry_s = rest[np_ + 3:]
    step = pl.program_id(1)

    @pl.when(step == 0)
    def _():
        st_s[...] = s0_ref[0]
        carry_s[...] = shift0_ref[0]

    xs = pb_ref[0][:, 0:B_SHIFT_W]
    row = lax.broadcasted_iota(jnp.int32, (tb, 1), 0)
    prev = jnp.where(row == 0, carry_s[...], pltpu.roll(xs, 1, 0))
    carry_s[...] = xs[tb - 1:tb, :]
    r, logw, k, v, av, bv, bonus = _rwkv_prep(xs, prev, p)
    gcum = _chunk_cumsum(logw, chunk)
    bon_s[...] = bonus
    for h in range(NH):
        sl = slice(h * HD, (h + 1) * HD)
        r_s[h] = r[:, sl]
        k_s[h] = k[:, sl]
        v_s[h] = v[:, sl]
        a_s[h] = av[:, sl]
        b_s[h] = bv[:, sl]
        w_s[h] = logw[:, sl]
        g_s[h] = gcum[:, sl]

    ri = lax.broadcasted_iota(jnp.int32, (chunk, chunk), 0)
    ci = lax.broadcasted_iota(jnp.int32, (chunk, chunk), 1)
    strict = ci < ri
    incl = ci <= ri

    def run_chunks(chunks):
        units = [(c, h) for c in chunks for h in range(NH)]
        hs = range(len(units))
        ld = lambda ref: [ref[h, c * chunk:(c + 1) * chunk, :] for c, h in units]
        rc, kc, vc, ac, bc, wc, gc = (ld(x) for x in (r_s, k_s, v_s, a_s, b_s, w_s, g_s))
        glast = [g[chunk - 1:chunk, :] for g in gc]
        einv = [jnp.exp(-g) for g in gc]
        etail = [jnp.exp(gl - g) for gl, g in zip(glast, gc)]
        a_t = [a * jnp.exp(g - w) for a, g, w in zip(ac, gc, wc)]
        r_t = [r * jnp.exp(g) for r, g in zip(rc, gc)]
        b_t = [b * e for b, e in zip(bc, einv)]
        k_t = [k * e for k, e in zip(kc, einv)]
        b_h = [b * e for b, e in zip(bc, etail)]
        k_h = [k * e for k, e in zip(kc, etail)]
        m_ab = [jnp.where(strict, _rw_score(a_t[h], b_t[h]), 0.0) for h in hs]
        m_ab_t = [jnp.where(ri < ci, _rw_score(b_t[h], a_t[h]), 0.0) for h in hs]
        m_ak = [jnp.where(strict, _rw_score(a_t[h], k_t[h]), 0.0) for h in hs]
        n_rb = [jnp.where(incl, _rw_score_out(r_t[h], b_t[h]), 0.0) for h in hs]
        n_rk = [jnp.where(incl, _rw_score_out(r_t[h], k_t[h]), 0.0) for h in hs]
        inv = _unit_lower_inverse(m_ab, m_ab_t, chunk)
        x = [_rw_apply(m_ak[h], vc[h]) for h in hs]
        a_hat = [_rw_solve(inv[h], a_t[h]) for h in hs]
        u = [_rw_solve(inv[h], x[h]) for h in hs]
        r_hat = [r_t[h] + _rw_apply_out(n_rb[h], a_hat[h]) for h in hs]
        y0 = [_rw_apply_out(n_rb[h], u[h]) + _rw_apply_out(n_rk[h], vc[h]) for h in hs]
        w_kk = [_rw_update(a_hat[h], b_h[h]) for h in hs]
        z = [_rw_update(u[h], b_h[h]) + _rw_update(vc[h], k_h[h]) for h in hs]
        decay = [jnp.exp(gl) for gl in glast]
        s = [st_s[h] for h in range(NH)]
        for i, (c, h) in enumerate(units):
            y_s[c * chunk:(c + 1) * chunk, h * HD:(h + 1) * HD] = _rw_read_out(r_hat[i], s[h]) + y0[i]
            s[h] = s[h] * decay[i] + _rw_state(s[h], w_kk[i]) + z[i]
        for h in range(NH):
            st_s[h] = s[h]

    n_chunks = tb // chunk
    for c0 in range(0, n_chunks, CHUNKS_PER_PASS):
        run_chunks(range(c0, min(c0 + CHUNKS_PER_PASS, n_chunks)))

    o_ref[0] = _rwkv_post(y_s[...], bon_s[...], pb_ref[0][:, B_SHIFT_W:B_W], p)

    @pl.when(step == pl.num_programs(1) - 1)
    def _():
        sfin_ref[0] = st_s[...]
        shift_ref[0] = carry_s[...]


def _rwkv_param_specs(params, nargs):
    zero2 = (lambda b, i: (0, 0)) if nargs == 2 else (lambda i: (0, 0))
    return [pl.BlockSpec(x.shape, zero2) for x in params]


def _rwkv_seq(pb, shift0, s0, params):
    B, T, _ = pb.shape
    tb = min(MIX_TILE, T)
    st = pl.BlockSpec((1, NH, HD, HD), lambda b, i: (b, 0, 0, 0))
    sh = pl.BlockSpec((1, 1, B_SHIFT_W), lambda b, i: (b, 0, 0))
    return pl.pallas_call(
        functools.partial(_rwkv_seq_kernel, tb=tb, chunk=CHUNK_B),
        grid=(B, T // tb),
        in_specs=[pl.BlockSpec((1, tb, B_W), lambda b, i: (b, i, 0)), sh, st]
                 + _rwkv_param_specs(params, 2),
        out_specs=[pl.BlockSpec((1, tb, D_BR), lambda b, i: (b, i, 0)), st, sh],
        out_shape=[jax.ShapeDtypeStruct((B, T, D_BR), F32),
                   jax.ShapeDtypeStruct((B, NH, HD, HD), F32),
                   jax.ShapeDtypeStruct((B, 1, B_SHIFT_W), F32)],
        scratch_shapes=[pltpu.VMEM((NH, tb, HD), F32)] * 7
                       + [pltpu.VMEM((tb, D_BR), F32), pltpu.VMEM((tb, D_BR), F32),
                          pltpu.VMEM((NH, HD, HD), F32), pltpu.VMEM((1, B_SHIFT_W), F32)],
        compiler_params=_cparams("parallel", "arbitrary"),
        name="rwkv7_seq",
    )(pb, shift0, s0, *params)


def _rwkv_step_kernel(pb_ref, shift0_ref, s_ref, *rest, bb):
    np_ = len(_RWKV_PARAMS)
    p = {n: r[...] for n, r in zip(_RWKV_PARAMS, rest[:np_])}
    o_ref, snew_ref, y_s = rest[np_:]
    pb = pb_ref[...]
    r, logw, k, v, av, bv, bonus = _rwkv_prep(pb[:, 0:B_SHIFT_W], shift0_ref[...], p)
    dec = jnp.exp(logw)
    eye = _eye(HD)
    for b in range(bb):
        rows = []
        for h in range(NH):
            sl = slice(h * HD, (h + 1) * HD)
            row = lambda x: x[b:b + 1, sl]
            s = s_ref[b, h]
            sa = jnp.sum(s * row(av), axis=1, keepdims=True)
            v_col = _to_col(row(v), eye)
            s_new = s * row(dec) + sa * row(bv) + v_col * row(k)
            snew_ref[b, h] = s_new
            y_col = jnp.sum(s_new * row(r), axis=1, keepdims=True)
            rows.append(_to_row(y_col, eye))
        y_s[b:b + 1, :] = jnp.concatenate(rows, axis=1)
    o_ref[...] = _rwkv_post(y_s[...], bonus, pb[:, B_SHIFT_W:B_W], p)


def _rwkv_step(pb, shift0, state, params):
    B = pb.shape[0]
    bb = min(STEP_ROWS, B)
    st = pl.BlockSpec((bb, NH, HD, HD), lambda i: (i, 0, 0, 0))
    return pl.pallas_call(
        functools.partial(_rwkv_step_kernel, bb=bb),
        grid=(B // bb,),
        in_specs=[pl.BlockSpec((bb, B_W), lambda i: (i, 0)),
                  pl.BlockSpec((bb, B_SHIFT_W), lambda i: (i, 0)), st]
                 + _rwkv_param_specs(params, 1),
        out_specs=[pl.BlockSpec((bb, D_BR), lambda i: (i, 0)), st],
        out_shape=[jax.ShapeDtypeStruct((B, D_BR), F32),
                   jax.ShapeDtypeStruct((B, NH, HD, HD), F32)],
        scratch_shapes=[pltpu.VMEM((bb, D_BR), F32)],
        compiler_params=_cparams("parallel"),
        name="rwkv7_step",
    )(pb, shift0, state, *params)


def _qkv_prep_kernel(pc_ref, cos_ref, sin_ref, qg_ref, kg_ref, *rest, tm):
    nd = len(C_CONFIGS)
    outs = rest[:3 * nd]
    kc_ref = rest[3 * nd]
    scr = rest[3 * nd + 1:]
    pc = pc_ref[0]
    cos = cos_ref[...]
    sin = sin_ref[...]
    q = _rope(_head_rms(pc[:, 0:D_BR], qg_ref[...]), cos, sin) * (HD ** -0.5)
    k = _rope(_head_rms(pc[:, D_BR:2 * D_BR], kg_ref[...]), cos, sin)
    kc_ref[0] = k
    for a, x in enumerate((q, k, pc[:, 2 * D_BR:3 * D_BR])):
        for half in range(D_BR // LANES):
            scr[a][half] = x[:, half * LANES:(half + 1) * LANES]
    hpl = LANES // HD
    for ci, (_, dil) in enumerate(C_CONFIGS):
        for a in range(3):
            out = outs[3 * ci + a]
            for rho in range(dil):
                for half in range(D_BR // LANES):
                    rows = scr[a][half, pl.ds(rho, tm // dil, stride=dil), :]
                    for j in range(hpl):
                        out[0, rho, half * hpl + j] = rows[:, j * HD:(j + 1) * HD].astype(BF16)


def _qkv_prep(pc, cos, sin, qg, kg):
    B, T, _ = pc.shape
    tm = min(ROW_TILE, T)
    vec = pl.BlockSpec((1, D_BR), lambda b, i: (0, 0))
    tab = pl.BlockSpec((tm, D_BR), lambda b, i: (i, 0))
    out_specs, out_shape = [], []
    for _, dil in C_CONFIGS:
        for _ in range(3):
            out_specs.append(pl.BlockSpec((1, dil, NH, tm // dil, HD), lambda b, i: (b, 0, 0, i, 0)))
            out_shape.append(jax.ShapeDtypeStruct((B, dil, NH, T // dil, HD), BF16))
    out_specs.append(pl.BlockSpec((1, tm, D_BR), lambda b, i: (b, i, 0)))
    out_shape.append(jax.ShapeDtypeStruct((B, T, D_BR), F32))
    return pl.pallas_call(
        functools.partial(_qkv_prep_kernel, tm=tm),
        grid=(B, T // tm),
        in_specs=[pl.BlockSpec((1, tm, C_W), lambda b, i: (b, i, 0)), tab, tab, vec, vec],
        out_specs=out_specs,
        out_shape=out_shape,
        scratch_shapes=[pltpu.VMEM((D_BR // LANES, tm, LANES), F32)] * 3,
        compiler_params=_cparams("parallel", "parallel"),
        name="attn_qkv_prep",
    )(pc, cos, sin, qg, kg)


def _swa_kernel(q_ref, kc_ref, vc_ref, kp_ref, vp_ref, o_ref, lse_ref, *, qt):
    i = pl.program_id(1)
    n = Q_BLOCK
    r = lax.broadcasted_iota(jnp.int32, (n, 2 * n), 0)
    c = lax.broadcasted_iota(jnp.int32, (n, 2 * n), 1)
    band = (c >= r) & (c <= r + n)
    first = band & (c >= jnp.where(i > 0, 0, n))
    for u in range(qt // n):
        o_parts, lse_parts = [], []
        for h in range(NH):
            q = q_ref[0, h, u * n:(u + 1) * n, :]
            if u == 0:
                kwin = jnp.concatenate([kp_ref[0, h], kc_ref[0, h, 0:n, :]], axis=0)
                vwin = jnp.concatenate([vp_ref[0, h], vc_ref[0, h, 0:n, :]], axis=0)
                mask = first
            else:
                kwin = kc_ref[0, h, (u - 1) * n:(u + 1) * n, :]
                vwin = vc_ref[0, h, (u - 1) * n:(u + 1) * n, :]
                mask = band
            s = lax.dot_general(q, kwin, (((1,), (1,)), ((), ())), preferred_element_type=F32)
            s = jnp.where(mask, s, MASK_VALUE)
            m = jnp.max(s, axis=1, keepdims=True)
            p = jnp.exp(s - m)
            l = jnp.sum(p, axis=1, keepdims=True)
            o_parts.append(jnp.dot(p.astype(BF16), vwin, preferred_element_type=F32) / l)
            lse_parts.append(jnp.broadcast_to(m + jnp.log(l), (n, HD)))
        o_ref[0, u * n:(u + 1) * n, :] = jnp.concatenate(o_parts, axis=1)
        lse_ref[0, u * n:(u + 1) * n, :] = jnp.concatenate(lse_parts, axis=1)


def _swa(q, k, v):
    N, _, L, _ = q.shape
    n = Q_BLOCK
    qt = min(ATTN_TILE, L)
    cur = pl.BlockSpec((1, NH, qt, HD), lambda s, i: (s, 0, i, 0))
    prev = pl.BlockSpec((1, NH, n, HD), lambda s, i: (s, 0, jnp.maximum(i * (qt // n) - 1, 0), 0))
    out = pl.BlockSpec((1, qt, D_BR), lambda s, i: (s, i, 0))
    return pl.pallas_call(
        functools.partial(_swa_kernel, qt=qt),
        grid=(N, L // qt),
        in_specs=[cur, cur, cur, prev, prev],
        out_specs=[out, out],
        out_shape=[jax.ShapeDtypeStruct((N, L, D_BR), F32)] * 2,
        compiler_params=_cparams("parallel", "parallel"),
        name="sliding_window_attn",
    )(q, k, v, k, v)


def _attn_combine_kernel(*refs, tm):
    nd = len(C_CONFIGS)
    o_refs, lse_refs = refs[0:nd], refs[nd:2 * nd]
    g_ref, out_ref = refs[2 * nd], refs[2 * nd + 1]
    o_s, lse_s = refs[2 * nd + 2], refs[2 * nd + 3]
    os_, lses = [], []
    for ci, (_, dil) in enumerate(C_CONFIGS):
        if dil == 1:
            os_.append(o_refs[ci][0, 0])
            lses.append(lse_refs[ci][0, 0])
            continue
        nh = D_BR // LANES
        for rho in range(dil):
            for half in range(nh):
                ls = slice(half * LANES, (half + 1) * LANES)
                o_s[ci, half, pl.ds(rho, tm // dil, stride=dil), :] = o_refs[ci][0, rho][:, ls]
                lse_s[ci, half, pl.ds(rho, tm // dil, stride=dil), :] = lse_refs[ci][0, rho][:, ls]
        os_.append(jnp.concatenate([o_s[ci, half] for half in range(nh)], axis=1))
        lses.append(jnp.concatenate([lse_s[ci, half] for half in range(nh)], axis=1))
    top = functools.reduce(jnp.maximum, lses)
    ws = [jnp.exp(x - top) for x in lses]
    num = functools.reduce(lambda a, b: a + b, [w * o for w, o in zip(ws, os_)])
    den = functools.reduce(lambda a, b: a + b, ws)
    out_ref[0] = num / den * _silu(g_ref[0])


def _attn_combine(outs, lses, pc):
    B, T, _ = pc.shape
    tm = min(ROW_TILE, T)
    nd = len(C_CONFIGS)
    specs = [pl.BlockSpec((1, dil, tm // dil, D_BR), lambda b, i: (b, 0, i, 0)) for _, dil in C_CONFIGS]
    return pl.pallas_call(
        functools.partial(_attn_combine_kernel, tm=tm),
        grid=(B, T // tm),
        in_specs=specs + specs + [pl.BlockSpec((1, tm, D_BR), lambda b, i: (b, i, 3))],
        out_specs=pl.BlockSpec((1, tm, D_BR), lambda b, i: (b, i, 0)),
        out_shape=jax.ShapeDtypeStruct((B, T, D_BR), F32),
        scratch_shapes=[pltpu.VMEM((nd, D_BR // LANES, tm, LANES), F32)] * 2,
        compiler_params=_cparams("parallel", "parallel"),
        name="attn_combine",
    )(*outs, *lses, pc)


def _attn_seq(pc, cos, sin, qg, kg):
    B, T, _ = pc.shape
    assert T % (Q_BLOCK * max(d for _, d in C_CONFIGS)) == 0
    res = _qkv_prep(pc, cos, sin, qg, kg)
    outs, lses = [], []
    for ci, (w, dil) in enumerate(C_CONFIGS):
        assert w == Q_BLOCK * dil
        q, k, v = (x.reshape(B * dil, NH, T // dil, HD) for x in res[3 * ci:3 * ci + 3])
        o, lse = _swa(q, k, v)
        outs.append(o.reshape(B, dil, T // dil, D_BR))
        lses.append(lse.reshape(B, dil, T // dil, D_BR))
    return _attn_combine(outs, lses, pc), res[-1]


def _attn_step_kernel(pc_ref, cos_ref, sin_ref, qg_ref, kg_ref, mult_ref, kt_ref, vt_ref,
                      o_ref, knew_ref, vnew_ref, *, bb):
    pc = pc_ref[...]
    cos = cos_ref[...]
    sin = sin_ref[...]
    q = _rope(_head_rms(pc[:, 0:D_BR], qg_ref[...]), cos, sin) * (HD ** -0.5)
    k = _rope(_head_rms(pc[:, D_BR:2 * D_BR], kg_ref[...]), cos, sin)
    v = pc[:, 2 * D_BR:3 * D_BR]
    knew_ref[...] = k
    vnew_ref[...] = v
    mult = mult_ref[...]
    live = mult > 0.0
    n_self = float(len(C_CONFIGS))
    eye = _eye(HD)
    for b in range(bb):
        parts = []
        for h in range(NH):
            sl = slice(h * HD, (h + 1) * HD)
            qr, kr, vr = q[b:b + 1, sl], k[b:b + 1, sl], v[b:b + 1, sl]
            s_self = jnp.sum(qr * kr, axis=1, keepdims=True)
            s = jnp.sum(kt_ref[b, h] * _to_col(qr, eye), axis=0, keepdims=True)
            s = jnp.where(live, s, MASK_VALUE)
            m = jnp.maximum(jnp.max(s, axis=1, keepdims=True), s_self)
            p = jnp.exp(s - m) * mult
            p_self = n_self * jnp.exp(s_self - m)
            den = jnp.sum(p, axis=1, keepdims=True) + p_self
            acc = _to_row(jnp.sum(vt_ref[b, h] * p, axis=1, keepdims=True), eye) + p_self * vr
            parts.append(acc / den)
        o_ref[b:b + 1, :] = jnp.concatenate(parts, axis=1)
    o_ref[...] = o_ref[...] * _silu(pc[:, 3 * D_BR:4 * D_BR])


def _attn_step(pc, cos, sin, qg, kg, cache_kt, cache_vt, layer):
    B = pc.shape[0]
    P = cache_kt.shape[-1]
    bb = min(ATTN_STEP_ROWS, B)
    dist = P - jnp.arange(P)
    mult = jnp.zeros((P,), F32)
    for w, dil in C_CONFIGS:
        mult = mult + ((dist <= w) & (dist % dil == 0)).astype(F32)
    vec = pl.BlockSpec((1, D_BR), lambda i: (0, 0))
    row = pl.BlockSpec((None, bb, D_BR), lambda i: (i, 0, 0))
    cache = pl.BlockSpec((None, bb, NH, HD, P), lambda i: (layer, i, 0, 0, 0))
    outs = pl.pallas_call(
        functools.partial(_attn_step_kernel, bb=bb),
        grid=(B // bb,),
        in_specs=[pl.BlockSpec((None, bb, C_W), lambda i: (i, 0, 0)), vec, vec, vec, vec,
                  pl.BlockSpec((1, P), lambda i: (0, 0)), cache, cache],
        out_specs=[row, row, row],
        out_shape=[jax.ShapeDtypeStruct((B // bb, bb, D_BR), F32)] * 3,
        compiler_params=_cparams("parallel"),
        name="dilated_attn_step",
    )(pc.reshape(B // bb, bb, C_W), cos, sin, qg, kg, mult.reshape(1, P), cache_kt, cache_vt)
    return [o.reshape(B, D_BR) for o in outs]


def _pool_mix(pooled, u, g, w_ref, scale_ref):
    mixed = _bdot(pooled - u, w_ref[...]) * scale_ref[...]
    return mixed * _silu(g)


def _pool_select(sums, cnts):
    lane = lax.broadcasted_iota(jnp.int32, sums[0].shape, 1) // (D_BR // len(POOL_WINDOWS))
    out = sums[-1] / cnts[-1]
    for gi in range(len(POOL_WINDOWS) - 2, -1, -1):
        out = jnp.where(lane == gi, sums[gi] / cnts[gi], out)
    return out


def _pool_seq_kernel(pd_ref, buf0_ref, w_ref, scale_ref, o_ref, carry_s, *, tb, t0):
    step = pl.program_id(1)

    @pl.when(step == 0)
    def _():
        carry_s[...] = buf0_ref[0]

    pd = pd_ref[0]
    u = pd[:, 0:D_BR]
    ext = jnp.concatenate([carry_s[...], u], axis=0)
    carry_s[...] = u[tb - POOL_MAX:tb, :]
    pos = (t0 + step * tb + lax.broadcasted_iota(jnp.int32, (tb, 1), 0)).astype(F32)
    sums, cnts = [], []
    run = ext
    shift = 1
    for w in POOL_WINDOWS:
        while shift < w:
            run = run + pltpu.roll(run, shift, 0)
            shift *= 2
        sums.append(run[POOL_MAX:, :])
        cnts.append(jnp.minimum(float(w), pos + 1.0))
    o_ref[0] = _pool_mix(_pool_select(sums, cnts), u, pd[:, D_BR:2 * D_BR], w_ref, scale_ref)


def _pool_seq(pd, buf0, w_bd, scale, t0):
    B, T, _ = pd.shape
    tb = min(ROW_TILE, T)
    return pl.pallas_call(
        functools.partial(_pool_seq_kernel, tb=tb, t0=t0),
        grid=(B, T // tb),
        in_specs=[pl.BlockSpec((1, tb, D_W), lambda b, i: (b, i, 0)),
                  pl.BlockSpec((1, POOL_MAX, D_BR), lambda b, i: (b, 0, 0)),
                  pl.BlockSpec((D_BR, D_BR), lambda b, i: (0, 0)),
                  pl.BlockSpec((1, D_BR), lambda b, i: (0, 0))],
        out_specs=pl.BlockSpec((1, tb, D_BR), lambda b, i: (b, i, 0)),
        out_shape=jax.ShapeDtypeStruct((B, T, D_BR), F32),
        scratch_shapes=[pltpu.VMEM((POOL_MAX, D_BR), F32)],
        compiler_params=_cparams("parallel", "arbitrary"),
        name="pool_seq",
    )(pd, buf0, w_bd, scale)


def _pool_step_kernel(pd_ref, buf_ref, w_ref, scale_ref, o_ref, *, t0):
    pd = pd_ref[...]
    u = pd[:, 0:D_BR]
    nbuf = POOL_MAX - 1
    sums, cnts = [], []
    run = u
    taken = 1
    for w in POOL_WINDOWS:
        while taken < w:
            run = run + buf_ref[:, nbuf - taken, :]
            taken += 1
        sums.append(run)
        cnts.append(float(min(w, t0 + 1)))
    o_ref[...] = _pool_mix(_pool_select(sums, cnts), u, pd[:, D_BR:2 * D_BR], w_ref, scale_ref)


def _pool_step(pd, buf, w_bd, scale, t0):
    B = pd.shape[0]
    bb = min(POOL_STEP_ROWS, B)
    return pl.pallas_call(
        functools.partial(_pool_step_kernel, t0=t0),
        grid=(B // bb,),
        in_specs=[pl.BlockSpec((bb, D_W), lambda i: (i, 0)),
                  pl.BlockSpec((bb, POOL_MAX - 1, D_BR), lambda i: (i, 0, 0)),
                  pl.BlockSpec((D_BR, D_BR), lambda i: (0, 0)),
                  pl.BlockSpec((1, D_BR), lambda i: (0, 0))],
        out_specs=pl.BlockSpec((bb, D_BR), lambda i: (i, 0)),
        out_shape=jax.ShapeDtypeStruct((B, D_BR), F32),
        compiler_params=_cparams("parallel"),
        name="pool_step",
    )(pd, buf, w_bd, scale)


def _rope_tables(pos):
    half = HD // 2
    inv_freq = jnp.float32(ROPE_THETA) ** (-jnp.arange(half, dtype=F32) / half)
    ang = pos.astype(F32)[:, None] * inv_freq[None, :]
    cos = jnp.cos(ang)
    sin = jnp.sin(ang)
    cos_t = jnp.tile(jnp.concatenate([cos, cos], axis=-1), (1, NH))
    sin_t = jnp.tile(jnp.concatenate([-sin, sin], axis=-1), (1, NH))
    return cos_t, sin_t


def _block_diag(w):
    g, c, d = w.shape
    out = jnp.zeros((g * c, g * d), w.dtype)
    for i in range(g):
        out = out.at[i * c:(i + 1) * c, i * d:(i + 1) * d].set(w[i])
    return out


def _layer_params(l, W):
    row = lambda x: x[l].reshape(1, -1)
    w_in = W["w_in"][l].astype(BF16)
    cuts = (0, A_W, A_W + B_W, A_W + B_W + C_W, A_W + B_W + C_W + D_W)
    return dict(
        norm_g=W["norm_g"][l],
        w_parts=[w_in[:, cuts[i]:cuts[i + 1]] for i in range(4)],
        w_out=W["w_out"][l].astype(BF16),
        onorm_g=row(W["a_onorm_g"]),
        rwkv=[row(W["b_mu"]), row(W["b_w0"]), W["b_w2"][l], row(W["b_a0"]), W["b_a2"][l],
              row(W["b_k_k"]), row(W["b_k_a"]), row(W["b_r_k"]), row(W["b_gn_w"]), row(W["b_gn_b"])],
        qg=jnp.tile(W["c_qnorm_g"][l], NH).reshape(1, D_BR),
        kg=jnp.tile(W["c_knorm_g"][l], NH).reshape(1, D_BR),
        w_pool=_block_diag(W["d_w_pool"][l]),
        p_scale=row(W["d_scale"]),
    )


def _run_prompt(x, mods, layers, log_lb, log1m_lb):
    B, T, _ = x.shape
    cos, sin = _rope_tables(jnp.arange(T))
    c_keep = min(C_WIN_MAX, T)
    d_keep = min(POOL_MAX - 1, T)
    zero_state = jnp.zeros((B, NH, HD, HD), F32)
    zero_shift = jnp.zeros((B, 1, B_SHIFT_W), F32)
    zero_pool = jnp.zeros((B, POOL_MAX, D_BR), F32)
    nA, nBw, nBs, nK, nV, nD = [], [], [], [], [], []
    for l, L in enumerate(layers):
        shift, scale, gate = (mods[l][:, None, i * D_MODEL:(i + 1) * D_MODEL] for i in range(3))
        pa, pb, pc, pd = _in_proj(x, shift, scale, L["norm_g"], L["w_parts"])
        oa, sa_t = _hgrn2_seq(pa, zero_state, log_lb[l], log1m_lb[l], L["onorm_g"])
        ob, sb, sbs = _rwkv_seq(pb, zero_shift, zero_state, L["rwkv"])
        oc, k_rot = _attn_seq(pc, cos, sin, L["qg"], L["kg"])
        od = _pool_seq(pd, zero_pool, L["w_pool"], L["p_scale"], 0)
        x = _out_proj(oa, ob, oc, od, x, gate, L["w_out"])
        nA.append(jnp.swapaxes(sa_t, -1, -2))
        nBw.append(sb)
        nBs.append(sbs[:, 0])
        nK.append(k_rot[:, T - c_keep:].reshape(B, c_keep, NH, HD))
        nV.append(pc[:, T - c_keep:, 2 * D_BR:3 * D_BR].reshape(B, c_keep, NH, HD))
        nD.append(pd[:, T - d_keep:, 0:D_BR])
    return (x,) + tuple(jnp.stack(a) for a in (nA, nBw, nBs, nK, nV, nD))


def _run_sample(x, mods, layers, log_lb, log1m_lb, sA, sBw, sBs, ck, cv, cd):
    B, T, _ = x.shape
    assert T == 1 and ck.shape[2] == PAST_LEN == C_WIN_MAX
    cos, sin = _rope_tables(jnp.full((1,), PAST_LEN))
    ck_t = jnp.transpose(ck, (0, 1, 3, 4, 2))
    cv_t = jnp.transpose(cv, (0, 1, 3, 4, 2))
    xr = x.reshape(1, B, D_MODEL)
    nA, nBw, nBs, nK, nV, nD = [], [], [], [], [], []
    for l, L in enumerate(layers):
        shift, scale, gate = (mods[l][None, :, i * D_MODEL:(i + 1) * D_MODEL] for i in range(3))
        pa, pb, pc, pd = (p[0] for p in _in_proj(xr, shift, scale, L["norm_g"], L["w_parts"]))
        oa, sa = _hgrn2_step(pa, sA[l], log_lb[l], log1m_lb[l], L["onorm_g"])
        ob, sb = _rwkv_step(pb, sBs[l], sBw[l], L["rwkv"])
        oc, k_new, v_new = _attn_step(pc, cos, sin, L["qg"], L["kg"], ck_t, cv_t, l)
        od = _pool_step(pd, cd[l], L["w_pool"], L["p_scale"], PAST_LEN)
        xr = _out_proj(oa[None], ob[None], oc[None], od[None], xr, gate, L["w_out"])
        nA.append(sa)
        nBw.append(sb)
        nBs.append(pb[:, 0:B_SHIFT_W])
        nK.append(k_new.reshape(B, 1, NH, HD))
        nV.append(v_new.reshape(B, 1, NH, HD))
        nD.append(pd[:, None, 0:D_BR])
    return (xr.reshape(B, 1, D_MODEL),) + tuple(jnp.stack(a) for a in (nA, nBw, nBs, nK, nV, nD))


def kernel(x_prompt, x_sample, state_A, state_B_wkv, state_B_shift, cache_C_k, cache_C_v, cache_D_pool, c_prompt, c_sample, ada_w, ada_b, norm_g, w_in, w_out, a_lb_logits, a_onorm_g, b_mu, b_w0, b_w2, b_a0, b_a2, b_k_k, b_k_a, b_r_k, b_gn_w, b_gn_b, c_qnorm_g, c_knorm_g, d_w_pool, d_scale):
    W = dict(norm_g=norm_g, w_in=w_in, w_out=w_out, a_onorm_g=a_onorm_g, b_mu=b_mu, b_w0=b_w0,
             b_w2=b_w2, b_a0=b_a0, b_a2=b_a2, b_k_k=b_k_k, b_k_a=b_k_a, b_r_k=b_r_k,
             b_gn_w=b_gn_w, b_gn_b=b_gn_b, c_qnorm_g=c_qnorm_g, c_knorm_g=c_knorm_g,
             d_w_pool=d_w_pool, d_scale=d_scale)
    layers = [_layer_params(l, W) for l in range(DEPTH)]
    lb_sm = jax.nn.softmax(a_lb_logits.astype(F32), axis=0)
    lower = jnp.cumsum(lb_sm, axis=0) - lb_sm[0:1]
    log_lb = jnp.log(jnp.maximum(lower, LB_FLOOR))[:, None, :]
    log1m_lb = jnp.log1p(-lower)[:, None, :]

    nb = x_prompt.shape[0]
    mods = _modulation(jnp.concatenate([c_prompt, c_sample], axis=0), ada_w, ada_b)
    y_p, pA, pBw, pBs, pK, pV, pD = _run_prompt(x_prompt, mods[:, :nb], layers, log_lb, log1m_lb)
    y_s, sA, sBw, sBs, sK, sV, sD = _run_sample(
        x_sample, mods[:, nb:], layers, log_lb, log1m_lb,
        state_A, state_B_wkv, state_B_shift, cache_C_k, cache_C_v, cache_D_pool)
    return (y_p, y_s, pA, sA, pBw, sBw, pBs, sBs, pK, sK, pV, sV, pD, sD)
```

```python
import functools

import jax
import jax.numpy as jnp
from jax import lax
from jax.experimental import pallas as pl
from jax.experimental.pallas import tpu as pltpu

F32 = jnp.float32
BF16 = jnp.bfloat16

D_MODEL = 1024
DEPTH = 4
PAST_LEN = 2048
D_BR = 256
HD = 64
NH = D_BR // HD
LB_FLOOR = 1e-30
LORA = 32
RWKV_GN_EPS = 64e-5
C_CONFIGS = ((128, 1), (512, 4), (2048, 16))
C_WIN_MAX = 2048
MASK_VALUE = -1e30
ROPE_THETA = 10000.0
POOL_WINDOWS = (2, 4, 8, 16)
POOL_MAX = 16
EPS = 1e-6
A_W = 4 * D_BR
B_SHIFT_W = 3 * D_BR + 2 * LORA
B_W = B_SHIFT_W + D_BR
C_W = 4 * D_BR
D_W = 2 * D_BR

LANES = 128

ROW_TILE = 512
MIX_TILE = 256
CHUNK_A = 8
CHUNKS_PER_IT_A = 8
CHUNK_B = 64
INV_BLOCK = 16
CHUNKS_PER_PASS = 4
Q_BLOCK = 128
ATTN_TILE = 512
STEP_ROWS = 8
ATTN_STEP_ROWS = 2
POOL_STEP_ROWS = 32
VMEM_LIMIT = 56 * 1024 * 1024


def _cparams(*sem):
    return pltpu.CompilerParams(dimension_semantics=sem, vmem_limit_bytes=VMEM_LIMIT)


def _sigmoid(x):
    return 1.0 / (1.0 + jnp.exp(-x))


def _silu(x):
    return x * _sigmoid(x)


def _softplus(x):
    return jnp.maximum(x, 0.0) + jnp.log1p(jnp.exp(-jnp.abs(x)))


def _log_sigmoid(x):
    return -_softplus(-x)


def _logaddexp(a, b):
    return jnp.maximum(a, b) + jnp.log1p(jnp.exp(-jnp.abs(a - b)))


def _bdot(a, b):
    return jnp.dot(a.astype(BF16), b.astype(BF16), preferred_element_type=F32)


def _bdot_nt(a, b):
    return lax.dot_general(a.astype(BF16), b.astype(BF16), (((1,), (1,)), ((), ())),
                           preferred_element_type=F32)


def _bdot_tn(a, b):
    return lax.dot_general(a.astype(BF16), b.astype(BF16), (((0,), (0,)), ((), ())),
                           preferred_element_type=F32)


def _split3(x):
    h1 = x.astype(BF16)
    r1 = x - h1.astype(F32)
    h2 = r1.astype(BF16)
    r2 = r1 - h2.astype(F32)
    return h1, h2, r2.astype(BF16)


def _seg_sum(x):
    n = x.shape[-1]
    r = lax.broadcasted_iota(jnp.int32, (n, n), 0) // HD
    c = lax.broadcasted_iota(jnp.int32, (n, n), 1) // HD
    ones = jnp.where(r == c, 1.0, 0.0).astype(BF16)
    h1, h2, h3 = _split3(x)
    d = lambda h: jnp.dot(h, ones, preferred_element_type=F32)
    return d(h1) + d(h2) + d(h3)


def _chunk_cumsum(x, chunk):
    t = x.shape[0]
    r = lax.broadcasted_iota(jnp.int32, (t, t), 0)
    c = lax.broadcasted_iota(jnp.int32, (t, t), 1)
    tri = jnp.where((r // chunk == c // chunk) & (c <= r), 1.0, 0.0).astype(BF16)
    h1, h2, h3 = _split3(x)
    d = lambda h: jnp.dot(tri, h, preferred_element_type=F32)
    return d(h1) + d(h2) + d(h3)


def _head_rms(x, gamma):
    ms = _seg_sum(x * x) * (1.0 / HD)
    return x * lax.rsqrt(ms + EPS) * gamma


def _rope(x, cos, sin_signed):
    lane = lax.broadcasted_iota(jnp.int32, x.shape, 1) % HD
    partner = jnp.where(lane < HD // 2,
                        pltpu.roll(x, D_BR - HD // 2, 1),
                        pltpu.roll(x, HD // 2, 1))
    return x * cos + partner * sin_signed


def _eye(n):
    r = lax.broadcasted_iota(jnp.int32, (n, n), 0)
    c = lax.broadcasted_iota(jnp.int32, (n, n), 1)
    return jnp.where(r == c, 1.0, 0.0).astype(F32)


def _to_col(row, eye):
    return jnp.sum(eye * row, axis=1, keepdims=True)


def _to_row(col, eye):
    return jnp.sum(eye * col, axis=0, keepdims=True)


def _mod_kernel(c_ref, w_ref, b_ref, o_ref):
    c = _silu(c_ref[...])
    o_ref[0] = _bdot(c, w_ref[0]) + b_ref[0]


def _modulation(c_all, ada_w, ada_b):
    n = c_all.shape[0]
    tn = 512
    return pl.pallas_call(
        _mod_kernel,
        grid=(DEPTH, 3 * D_MODEL // tn),
        in_specs=[pl.BlockSpec((n, D_MODEL), lambda l, j: (0, 0)),
                  pl.BlockSpec((1, D_MODEL, tn), lambda l, j: (l, 0, j)),
                  pl.BlockSpec((1, 1, tn), lambda l, j: (l, 0, j))],
        out_specs=pl.BlockSpec((1, n, tn), lambda l, j: (l, 0, j)),
        out_shape=jax.ShapeDtypeStruct((DEPTH, n, 3 * D_MODEL), F32),
        compiler_params=_cparams("parallel", "parallel"),
        name="adaln_modulation",
    )(c_all, ada_w, ada_b.reshape(DEPTH, 1, 3 * D_MODEL))


def _in_proj_kernel(x_ref, shift_ref, scale_ref, g_ref, wa_ref, wb_ref, wc_ref, wd_ref,
                    pa_ref, pb_ref, pc_ref, pd_ref):
    x = x_ref[0]
    ms = jnp.mean(x * x, axis=-1, keepdims=True)
    h = x * lax.rsqrt(ms + EPS) * g_ref[...]
    h = (h * (1.0 + scale_ref[0]) + shift_ref[0]).astype(BF16)
    pa_ref[0] = jnp.dot(h, wa_ref[...], preferred_element_type=F32)
    pb_ref[0] = jnp.dot(h, wb_ref[...], preferred_element_type=F32)
    pc_ref[0] = jnp.dot(h, wc_ref[...], preferred_element_type=F32)
    pd_ref[0] = jnp.dot(h, wd_ref[...], preferred_element_type=F32)


def _in_proj(x, shift, scale, norm_g, w_parts):
    B, T, D = x.shape
    tm = min(ROW_TILE, T)
    mt = shift.shape[1]
    mod_block = (1, tm, D) if mt == T else (1, 1, D)
    mod_map = (lambda b, i: (b, i, 0)) if mt == T else (lambda b, i: (b, 0, 0))
    widths = [w.shape[1] for w in w_parts]
    return pl.pallas_call(
        _in_proj_kernel,
        grid=(B, T // tm),
        in_specs=[pl.BlockSpec((1, tm, D), lambda b, i: (b, i, 0)),
                  pl.BlockSpec(mod_block, mod_map),
                  pl.BlockSpec(mod_block, mod_map),
                  pl.BlockSpec((1, D), lambda b, i: (0, 0))]
                 + [pl.BlockSpec((D, w), lambda b, i: (0, 0)) for w in widths],
        out_specs=[pl.BlockSpec((1, tm, w), lambda b, i: (b, i, 0)) for w in widths],
        out_shape=[jax.ShapeDtypeStruct((B, T, w), F32) for w in widths],
        compiler_params=_cparams("parallel", "parallel"),
        name="norm_in_proj",
    )(x, shift, scale, norm_g.reshape(1, D), *w_parts)


def _in_proj_step_kernel(x_ref, shift_ref, scale_ref, g_ref, wat_ref, wbt_ref, wc_ref, wd_ref,
                         pat_ref, pbt_ref, pc_ref, pd_ref):
    x = x_ref[...]
    ms = jnp.mean(x * x, axis=-1, keepdims=True)
    h = x * lax.rsqrt(ms + EPS) * g_ref[...]
    h = (h * (1.0 + scale_ref[...]) + shift_ref[...]).astype(BF16)
    nt = lambda w: lax.dot_general(w, h, (((1,), (1,)), ((), ())), preferred_element_type=F32)
    pat_ref[...] = nt(wat_ref[...])
    pbt_ref[...] = nt(wbt_ref[...])
    pc_ref[...] = jnp.dot(h, wc_ref[...], preferred_element_type=F32)
    pd_ref[...] = jnp.dot(h, wd_ref[...], preferred_element_type=F32)


def _in_proj_step(x, shift, scale, norm_g, wa_t, wb_t, wc, wd):
    B, D = x.shape
    ins = (x, shift, scale, norm_g.reshape(1, D), wa_t, wb_t, wc, wd)
    shapes = [(wa_t.shape[0], B), (wb_t.shape[0], B), (B, wc.shape[1]), (B, wd.shape[1])]
    return pl.pallas_call(
        _in_proj_step_kernel,
        grid=(1,),
        in_specs=[pl.BlockSpec(a.shape, lambda i: (0, 0)) for a in ins],
        out_specs=[pl.BlockSpec(s, lambda i: (0, 0)) for s in shapes],
        out_shape=[jax.ShapeDtypeStruct(s, F32) for s in shapes],
        compiler_params=_cparams("arbitrary"),
        name="norm_in_proj_step",
    )(*ins)


def _out_proj_step_kernel(oat_ref, obt_ref, oc_ref, od_ref, x_ref, gate_ref, w_ref, y_ref):
    tn = lambda a, w: lax.dot_general(a.astype(BF16), w, (((0,), (0,)), ((), ())), preferred_element_type=F32)
    nn = lambda a, w: jnp.dot(a.astype(BF16), w, preferred_element_type=F32)
    res = (tn(oat_ref[...], w_ref[0:D_BR, :]) + tn(obt_ref[...], w_ref[D_BR:2 * D_BR, :])
           + nn(oc_ref[...], w_ref[2 * D_BR:3 * D_BR, :]) + nn(od_ref[...], w_ref[3 * D_BR:4 * D_BR, :]))
    y_ref[...] = x_ref[...] + gate_ref[...] * res


def _out_proj_step(oa_t, ob_t, oc, od, x, gate, w_out):
    ins = (oa_t, ob_t, oc, od, x, gate, w_out)
    return pl.pallas_call(
        _out_proj_step_kernel,
        grid=(1,),
        in_specs=[pl.BlockSpec(a.shape, lambda i: (0, 0)) for a in ins],
        out_specs=pl.BlockSpec(x.shape, lambda i: (0, 0)),
        out_shape=jax.ShapeDtypeStruct(x.shape, F32),
        compiler_params=_cparams("arbitrary"),
        name="out_proj_residual_step",
    )(*ins)


def _out_proj_kernel(oa_ref, ob_ref, oc_ref, od_ref, x_ref, gate_ref, w_ref, y_ref):
    mix = jnp.concatenate([oa_ref[0], ob_ref[0], oc_ref[0], od_ref[0]], axis=-1).astype(BF16)
    res = jnp.dot(mix, w_ref[...], preferred_element_type=F32)
    y_ref[0] = x_ref[0] + gate_ref[0] * res


def _out_proj(oa, ob, oc, od, x, gate, w_out):
    B, T, D = x.shape
    tm = min(ROW_TILE, T)
    mt = gate.shape[1]
    mod_block = (1, tm, D) if mt == T else (1, 1, D)
    mod_map = (lambda b, i: (b, i, 0)) if mt == T else (lambda b, i: (b, 0, 0))
    br = pl.BlockSpec((1, tm, D_BR), lambda b, i: (b, i, 0))
    return pl.pallas_call(
        _out_proj_kernel,
        grid=(B, T // tm),
        in_specs=[br, br, br, br,
                  pl.BlockSpec((1, tm, D), lambda b, i: (b, i, 0)),
                  pl.BlockSpec(mod_block, mod_map),
                  pl.BlockSpec((D, D), lambda b, i: (0, 0))],
        out_specs=pl.BlockSpec((1, tm, D), lambda b, i: (b, i, 0)),
        out_shape=jax.ShapeDtypeStruct((B, T, D), F32),
        compiler_params=_cparams("parallel", "parallel"),
        name="out_proj_residual",
    )(oa, ob, oc, od, x, gate, w_out)


def _hgrn2_gates(pa, log_lb, log1m_lb):
    q = _silu(pa[:, 0:D_BR])
    fl = pa[:, D_BR:2 * D_BR]
    v = pa[:, 2 * D_BR:3 * D_BR]
    ls_neg = _log_sigmoid(-fl)
    logf = _logaddexp(_log_sigmoid(fl), log_lb + ls_neg)
    k = jnp.exp(log1m_lb + ls_neg)
    return q, logf, k, v


def _hgrn2_seq_kernel(pa_ref, s0_ref, loglb_ref, log1mlb_ref, og_ref, o_ref, sfin_ref,
                      q_s, k_s, v_s, g_s, o_s, st_s, *, tb, chunk, per_it):
    step = pl.program_id(1)
    hpl = LANES // HD
    npair = NH // hpl
    rl = lax.broadcasted_iota(jnp.int32, (LANES, LANES), 0) // HD
    cl = lax.broadcasted_iota(jnp.int32, (LANES, LANES), 1) // HD
    same_head = rl == cl

    @pl.when(step == 0)
    def _():
        for p in range(npair):
            blocks = [jnp.concatenate([s0_ref[0, p * hpl + a] if a == b else jnp.zeros((HD, HD), F32)
                                       for b in range(hpl)], axis=1) for a in range(hpl)]
            st_s[p] = jnp.concatenate(blocks, axis=0)

    q, logf, k, v = _hgrn2_gates(pa_ref[0], loglb_ref[...], log1mlb_ref[...])
    gcum = _chunk_cumsum(logf, chunk)
    for p in range(npair):
        sl = slice(p * LANES, (p + 1) * LANES)
        q_s[p] = q[:, sl]
        k_s[p] = k[:, sl]
        v_s[p] = v[:, sl]
        g_s[p] = gcum[:, sl]

    rows = lax.broadcasted_iota(jnp.int32, (chunk, 1), 0)
    lane_head = lax.broadcasted_iota(jnp.int32, (chunk, LANES), 1) // HD

    def body(it, carry):
        units = [(j, p) for j in range(per_it) for p in range(npair)]
        starts = [pl.multiple_of((it * per_it + j) * chunk, chunk) for j in range(per_it)]
        ld = lambda ref: [ref[p, pl.ds(starts[j], chunk), :] for j, p in units]
        qc, kc, vc, gc = ld(q_s), ld(k_s), ld(v_s), ld(g_s)
        glast = [g[chunk - 1:chunk, :] for g in gc]
        upd = [jnp.where(same_head, _bdot_tn(v_, k_ * jnp.exp(gl - g)), 0.0)
               for v_, k_, gl, g in zip(vc, kc, glast, gc)]
        decay = [jnp.exp(gl) for gl in glast]
        qe = [q_ * jnp.exp(g) for q_, g in zip(qc, gc)]
        o_acc = [jnp.zeros((chunk, LANES), F32) for _ in units]
        for t in range(chunk):
            x = [jnp.exp(jnp.minimum(g[t:t + 1, :] - g, 0.0)) * k_ * q_[t:t + 1, :]
                 for g, k_, q_ in zip(gc, kc, qc)]
            sel = [(lane_head == a) & (rows <= t) for a in range(hpl)]
            att = []
            for xi in x:
                sums = [jnp.sum(jnp.where(m, xi, 0.0), axis=1, keepdims=True) for m in sel]
                a_ = sums[-1]
                for a in range(hpl - 2, -1, -1):
                    a_ = jnp.where(lane_head == a, sums[a], a_)
                att.append(a_)
            ot = [jnp.sum(a * v_, axis=0, keepdims=True) for a, v_ in zip(att, vc)]
            o_acc = [jnp.where(rows == t, r, o) for o, r in zip(o_acc, ot)]
        st = [st_s[p] for p in range(npair)]
        for i, (j, p) in enumerate(units):
            o_s[pl.ds(starts[j], chunk), p * LANES:(p + 1) * LANES] = o_acc[i] + _bdot_nt(qe[i], st[p])
            st[p] = st[p] * decay[i] + upd[i]
        for p in range(npair):
            st_s[p] = st[p]
        return carry

    lax.fori_loop(0, tb // (chunk * per_it), body, 0)

    o = _head_rms(o_s[...], og_ref[...])
    o_ref[0] = o * _silu(pa_ref[0][:, 3 * D_BR:4 * D_BR])

    @pl.when(step == pl.num_programs(1) - 1)
    def _():
        for h in range(NH):
            a = (h % hpl) * HD
            sfin_ref[0, h] = st_s[h // hpl][a:a + HD, a:a + HD]


def _hgrn2_seq(pa, s0_t, log_lb, log1m_lb, onorm_g):
    B, T, _ = pa.shape
    tb = min(MIX_TILE, T)
    vec = pl.BlockSpec((1, D_BR), lambda b, i: (0, 0))
    st = pl.BlockSpec((1, NH, HD, HD), lambda b, i: (b, 0, 0, 0))
    return pl.pallas_call(
        functools.partial(_hgrn2_seq_kernel, tb=tb, chunk=CHUNK_A, per_it=CHUNKS_PER_IT_A),
        grid=(B, T // tb),
        in_specs=[pl.BlockSpec((1, tb, A_W), lambda b, i: (b, i, 0)), st, vec, vec, vec],
        out_specs=[pl.BlockSpec((1, tb, D_BR), lambda b, i: (b, i, 0)), st],
        out_shape=[jax.ShapeDtypeStruct((B, T, D_BR), F32),
                   jax.ShapeDtypeStruct((B, NH, HD, HD), F32)],
        scratch_shapes=[pltpu.VMEM((D_BR // LANES, tb, LANES), F32)] * 4
                       + [pltpu.VMEM((tb, D_BR), F32), pltpu.VMEM((D_BR // LANES, LANES, LANES), F32)],
        compiler_params=_cparams("parallel", "arbitrary"),
        name="hgrn2_seq",
    )(pa, s0_t, log_lb, log1m_lb, onorm_g)


def _hgrn2_step_kernel(q_ref, f_ref, i_ref, g_ref, s_ref, loglb_ref, log1mlb_ref, og_ref, o_ref, snew_ref):
    q = _silu(q_ref[...])
    fl = f_ref[...]
    v = i_ref[...]
    ls_neg = _log_sigmoid(-fl)
    f = jnp.exp(_logaddexp(_log_sigmoid(fl), loglb_ref[...] + ls_neg))
    k = jnp.exp(log1mlb_ref[...] + ls_neg)
    o = jnp.zeros_like(v)
    for kk in range(HD):
        s_new = f[kk:kk + 1, :] * s_ref[kk] + k[kk:kk + 1, :] * v
        snew_ref[kk] = s_new
        o = o + q[kk:kk + 1, :] * s_new
    ms = jnp.mean(o * o, axis=0, keepdims=True)
    o_ref[...] = o * lax.rsqrt(ms + EPS) * og_ref[...] * _silu(g_ref[...])


def _col_params(x):
    return x.reshape(NH, HD, 1)


def _hgrn2_step(pa_t, state_t, layer, log_lb, log1m_lb, onorm_g):
    B = pa_t.shape[1]
    parts = pa_t.reshape(A_W // HD, HD, B)
    part = lambda j: pl.BlockSpec((None, HD, B), lambda h: (j * NH + h, 0, 0))
    col = pl.BlockSpec((None, HD, 1), lambda h: (h, 0, 0))
    return pl.pallas_call(
        _hgrn2_step_kernel,
        grid=(NH,),
        in_specs=[part(0), part(1), part(2), part(3),
                  pl.BlockSpec((None, None, HD, HD, B), lambda h: (layer, h, 0, 0, 0)), col, col, col],
        out_specs=[pl.BlockSpec((HD, B), lambda h: (h, 0)),
                   pl.BlockSpec((None, HD, HD, B), lambda h: (h, 0, 0, 0))],
        out_shape=[jax.ShapeDtypeStruct((D_BR, B), F32),
                   jax.ShapeDtypeStruct((NH, HD, HD, B), F32)],
        compiler_params=_cparams("parallel"),
        name="hgrn2_step",
    )(parts, parts, parts, parts, state_t, _col_params(log_lb), _col_params(log1m_lb), _col_params(onorm_g))


def _rwkv_prep(xs, prev, p):
    xm = xs + (prev - xs) * p["mu"]
    r = xm[:, 0:D_BR]
    k = xm[:, D_BR:2 * D_BR]
    v = xm[:, 2 * D_BR:3 * D_BR]
    xw = xm[:, 3 * D_BR:3 * D_BR + LORA]
    xa = xm[:, 3 * D_BR + LORA:3 * D_BR + 2 * LORA]
    w = -_softplus(-(p["w0"] + _bdot(jnp.tanh(xw), p["w2"]))) - 0.5
    logw = -jnp.exp(w)
    a = _sigmoid(p["a0"] + _bdot(xa, p["a2"]))
    kk = k * p["k_k"]
    kk = kk / jnp.maximum(jnp.sqrt(_seg_sum(kk * kk)), 1e-12)
    k = k * (1.0 + (a - 1.0) * p["k_a"])
    bonus = _seg_sum(r * k * p["r_k"]) * v
    return r, logw, k, v, -kk, kk * a, bonus


def _rwkv_post(y, bonus, g, p):
    mean = _seg_sum(y) * (1.0 / HD)
    yc = y - mean
    var = _seg_sum(yc * yc) * (1.0 / HD)
    yn = yc * lax.rsqrt(var + RWKV_GN_EPS) * p["gn_w"] + p["gn_b"]
    return (yn + bonus) * _silu(g)


def _unit_lower_inverse(m, m_t, n):
    nb = INV_BLOCK
    rows = lax.broadcasted_iota(jnp.int32, (nb, 1), 0)
    eye = _eye(nb)
    nm = len(m)
    per = n // nb
    group = lax.broadcasted_iota(jnp.int32, (nb, n), 1) // nb
    diag_t = []
    for q in range(nm):
        dq = m_t[q][0:nb, :]
        for b in range(1, per):
            dq = jnp.where(group == b, m_t[q][b * nb:(b + 1) * nb, :], dq)
        diag_t.append(dq)
    qpl = LANES // n
    tiles = [jnp.concatenate(diag_t[t * qpl:(t + 1) * qpl], axis=1) for t in range(nm // qpl)]
    lane = lax.broadcasted_iota(jnp.int32, (nb, LANES), 1)
    base = lane - lane % nb
    ds = [jnp.where(lane % nb == rows, 1.0, 0.0).astype(F32)] * len(tiles)
    for i in range(1, nb):
        coef = [jnp.take_along_axis(t, base + i, axis=1) for t in tiles]
        rws = [jnp.sum(cf * d, axis=0, keepdims=True) for cf, d in zip(coef, ds)]
        ds = [d + jnp.where(rows == i, rw, 0.0) for d, rw in zip(ds, rws)]
    blocks = [[ds[q // qpl][:, (q % qpl) * n + b * nb:(q % qpl) * n + (b + 1) * nb] for b in range(per)]
              for q in range(nm)]
    size = nb
    while size < n:
        pairs = [(q, j) for q in range(nm) for j in range(0, len(blocks[q]), 2)]
        inner = [_rw_inv(m[q][(j + 1) * size:(j + 2) * size, j * size:(j + 1) * size], blocks[q][j])
                 for q, j in pairs]
        cross = [_rw_inv(blocks[q][j + 1], t) for (q, j), t in zip(pairs, inner)]
        merged = [[] for _ in range(nm)]
        for (q, j), cr in zip(pairs, cross):
            top = jnp.concatenate([blocks[q][j], jnp.zeros((size, size), F32)], axis=1)
            bot = jnp.concatenate([cr, blocks[q][j + 1]], axis=1)
            merged[q].append(jnp.concatenate([top, bot], axis=0))
        blocks = merged
        size *= 2
    return [b[0] for b in blocks]


_rw_score = _bdot_nt
_rw_score_out = _bdot_nt
_rw_inv = _bdot
_rw_solve = _bdot
_rw_read = _bdot_nt
_rw_read_out = _bdot_nt
_rw_apply = _bdot
_rw_apply_out = _bdot
_rw_update = _bdot_tn
_rw_state = _bdot

_RWKV_PARAMS = ("mu", "w0", "w2", "a0", "a2", "k_k", "k_a", "r_k", "gn_w", "gn_b")


def _rwkv_seq_kernel(pb_ref, shift0_ref, s0_ref, *rest, tb, chunk):
    np_ = len(_RWKV_PARAMS)
    p = {n: r[...] for n, r in zip(_RWKV_PARAMS, rest[:np_])}
    o_ref, sfin_ref, shift_ref = rest[np_:np_ + 3]
    r_s, k_s, v_s, a_s, b_s, w_s, g_s, y_s, bon_s, st_s, carry_s = rest[np_ + 3:]
    step = pl.program_id(1)

    @pl.when(step == 0)
    def _():
        st_s[...] = s0_ref[0]
        carry_s[...] = shift0_ref[0]

    xs = pb_ref[0][:, 0:B_SHIFT_W]
    row = lax.broadcasted_iota(jnp.int32, (tb, 1), 0)
    prev = jnp.where(row == 0, carry_s[...], pltpu.roll(xs, 1, 0))
    carry_s[...] = xs[tb - 1:tb, :]
    r, logw, k, v, av, bv, bonus = _rwkv_prep(xs, prev, p)
    gcum = _chunk_cumsum(logw, chunk)
    bon_s[...] = bonus
    for h in range(NH):
        sl = slice(h * HD, (h + 1) * HD)
        r_s[h] = r[:, sl]
        k_s[h] = k[:, sl]
        v_s[h] = v[:, sl]
        a_s[h] = av[:, sl]
        b_s[h] = bv[:, sl]
        w_s[h] = logw[:, sl]
        g_s[h] = gcum[:, sl]

    ri = lax.broadcasted_iota(jnp.int32, (chunk, chunk), 0)
    ci = lax.broadcasted_iota(jnp.int32, (chunk, chunk), 1)
    strict = ci < ri
    incl = ci <= ri

    def run_chunks(chunks):
        units = [(c, h) for c in chunks for h in range(NH)]
        hs = range(len(units))
        ld = lambda ref: [ref[h, c * chunk:(c + 1) * chunk, :] for c, h in units]
        rc, kc, vc, ac, bc, wc, gc = (ld(x) for x in (r_s, k_s, v_s, a_s, b_s, w_s, g_s))
        glast = [g[chunk - 1:chunk, :] for g in gc]
        einv = [jnp.exp(-g) for g in gc]
        etail = [jnp.exp(gl - g) for gl, g in zip(glast, gc)]
        a_t = [a * jnp.exp(g - w) for a, g, w in zip(ac, gc, wc)]
        r_t = [r * jnp.exp(g) for r, g in zip(rc, gc)]
        b_t = [b * e for b, e in zip(bc, einv)]
        k_t = [k * e for k, e in zip(kc, einv)]
        b_h = [b * e for b, e in zip(bc, etail)]
        k_h = [k * e for k, e in zip(kc, etail)]
        m_ab = [jnp.where(strict, _rw_score(a_t[h], b_t[h]), 0.0) for h in hs]
        m_ab_t = [jnp.where(ri < ci, _rw_score(b_t[h], a_t[h]), 0.0) for h in hs]
        m_ak = [jnp.where(strict, _rw_score(a_t[h], k_t[h]), 0.0) for h in hs]
        n_rb = [jnp.where(incl, _rw_score_out(r_t[h], b_t[h]), 0.0) for h in hs]
        n_rk = [jnp.where(incl, _rw_score_out(r_t[h], k_t[h]), 0.0) for h in hs]
        inv = _unit_lower_inverse(m_ab, m_ab_t, chunk)
        x = [_rw_apply(m_ak[h], vc[h]) for h in hs]
        a_hat = [_rw_solve(inv[h], a_t[h]) for h in hs]
        u = [_rw_solve(inv[h], x[h]) for h in hs]
        r_hat = [r_t[h] + _rw_apply_out(n_rb[h], a_hat[h]) for h in hs]
        y0 = [_rw_apply_out(n_rb[h], u[h]) + _rw_apply_out(n_rk[h], vc[h]) for h in hs]
        w_kk = [_rw_update(a_hat[h], b_h[h]) for h in hs]
        z = [_rw_update(u[h], b_h[h]) + _rw_update(vc[h], k_h[h]) for h in hs]
        decay = [jnp.exp(gl) for gl in glast]
        s = [st_s[h] for h in range(NH)]
        for i, (c, h) in enumerate(units):
            y_s[c * chunk:(c + 1) * chunk, h * HD:(h + 1) * HD] = _rw_read_out(r_hat[i], s[h]) + y0[i]
            s[h] = s[h] * decay[i] + _rw_state(s[h], w_kk[i]) + z[i]
        for h in range(NH):
            st_s[h] = s[h]

    n_chunks = tb // chunk
    for c0 in range(0, n_chunks, CHUNKS_PER_PASS):
        run_chunks(range(c0, min(c0 + CHUNKS_PER_PASS, n_chunks)))

    o_ref[0] = _rwkv_post(y_s[...], bon_s[...], pb_ref[0][:, B_SHIFT_W:B_W], p)

    @pl.when(step == pl.num_programs(1) - 1)
    def _():
        sfin_ref[0] = st_s[...]
        shift_ref[0] = carry_s[...]


def _rwkv_param_specs(params, nargs):
    zero2 = (lambda b, i: (0, 0)) if nargs == 2 else (lambda i: (0, 0))
    return [pl.BlockSpec(x.shape, zero2) for x in params]


def _rwkv_seq(pb, shift0, s0, params):
    B, T, _ = pb.shape
    tb = min(MIX_TILE, T)
    st = pl.BlockSpec((1, NH, HD, HD), lambda b, i: (b, 0, 0, 0))
    sh = pl.BlockSpec((1, 1, B_SHIFT_W), lambda b, i: (b, 0, 0))
    return pl.pallas_call(
        functools.partial(_rwkv_seq_kernel, tb=tb, chunk=CHUNK_B),
        grid=(B, T // tb),
        in_specs=[pl.BlockSpec((1, tb, B_W), lambda b, i: (b, i, 0)), sh, st]
                 + _rwkv_param_specs(params, 2),
        out_specs=[pl.BlockSpec((1, tb, D_BR), lambda b, i: (b, i, 0)), st, sh],
        out_shape=[jax.ShapeDtypeStruct((B, T, D_BR), F32),
                   jax.ShapeDtypeStruct((B, NH, HD, HD), F32),
                   jax.ShapeDtypeStruct((B, 1, B_SHIFT_W), F32)],
        scratch_shapes=[pltpu.VMEM((NH, tb, HD), F32)] * 7
                       + [pltpu.VMEM((tb, D_BR), F32), pltpu.VMEM((tb, D_BR), F32),
                          pltpu.VMEM((NH, HD, HD), F32), pltpu.VMEM((1, B_SHIFT_W), F32)],
        compiler_params=_cparams("parallel", "arbitrary"),
        name="rwkv7_seq",
    )(pb, shift0, s0, *params)


def _rwkv_step_kernel(r_ref, k_ref, v_ref, lr_ref, g_ref, pr_ref, pk_ref, pv_ref, plr_ref,
                      mr_ref, mk_ref, mv_ref, mlr_ref, w0_ref, w2_ref, a0_ref, a2_ref,
                      kk_ref, ka_ref, rk_ref, gw_ref, gb_ref, s_ref, o_ref, snew_ref, y_s):
    lerp = lambda x, prev, mu: x[...] + (prev[...] - x[...]) * mu[...]
    r = lerp(r_ref, pr_ref, mr_ref)
    k = lerp(k_ref, pk_ref, mk_ref)
    v = lerp(v_ref, pv_ref, mv_ref)
    lr = lerp(lr_ref, plr_ref, mlr_ref)
    w = -_softplus(-(w0_ref[...] + _bdot(w2_ref[...], jnp.tanh(lr[0:LORA])))) - 0.5
    dec = jnp.exp(-jnp.exp(w))
    a = _sigmoid(a0_ref[...] + _bdot(a2_ref[...], lr[LORA:2 * LORA]))
    kk = k * kk_ref[...]
    kk = kk / jnp.maximum(jnp.sqrt(jnp.sum(kk * kk, axis=0, keepdims=True)), 1e-12)
    k = k * (1.0 + (a - 1.0) * ka_ref[...])
    bonus = jnp.sum(r * k * rk_ref[...], axis=0, keepdims=True) * v
    av = -kk
    bv = kk * a
    for i in range(HD):
        s = s_ref[i]
        sa = jnp.sum(s * av, axis=0, keepdims=True)
        s_new = s * dec + sa * bv + v[i:i + 1, :] * k
        snew_ref[i] = s_new
        y_s[i:i + 1, :] = jnp.sum(s_new * r, axis=0, keepdims=True)
    y = y_s[...]
    mean = jnp.mean(y, axis=0, keepdims=True)
    yc = y - mean
    var = jnp.mean(yc * yc, axis=0, keepdims=True)
    yn = yc * lax.rsqrt(var + RWKV_GN_EPS) * gw_ref[...] + gb_ref[...]
    o_ref[...] = (yn + bonus) * _silu(g_ref[...])


def _rwkv_step(pb_t, shift_t, state_t, layer, params):
    B = pb_t.shape[1]
    p = dict(zip(_RWKV_PARAMS, params))
    parts = pb_t.reshape(B_W // HD, HD, B)
    prev = shift_t.reshape(DEPTH, B_SHIFT_W // HD, HD, B)
    mu = p["mu"].reshape(B_SHIFT_W // HD, HD, 1)
    lr_row = 3 * NH
    part = lambda j: pl.BlockSpec((None, HD, B), lambda h: (j * NH + h, 0, 0))
    pprev = lambda j: pl.BlockSpec((None, None, HD, B), lambda h: (layer, j * NH + h, 0, 0))
    pmu = lambda j: pl.BlockSpec((None, HD, 1), lambda h: (j * NH + h, 0, 0))
    col = pl.BlockSpec((None, HD, 1), lambda h: (h, 0, 0))
    lora = pl.BlockSpec((HD, LORA), lambda h: (h, 0))
    cols = [_col_params(p[n]) for n in ("k_k", "k_a", "r_k", "gn_w", "gn_b")]
    return pl.pallas_call(
        _rwkv_step_kernel,
        grid=(NH,),
        in_specs=[part(0), part(1), part(2),
                  pl.BlockSpec((None, HD, B), lambda h: (lr_row, 0, 0)),
                  pl.BlockSpec((None, HD, B), lambda h: (lr_row + 1 + h, 0, 0)),
                  pprev(0), pprev(1), pprev(2),
                  pl.BlockSpec((None, None, HD, B), lambda h: (layer, lr_row, 0, 0)),
                  pmu(0), pmu(1), pmu(2),
                  pl.BlockSpec((None, HD, 1), lambda h: (lr_row, 0, 0)),
                  col, lora, col, lora, col, col, col, col, col,
                  pl.BlockSpec((None, None, HD, HD, B), lambda h: (layer, h, 0, 0, 0))],
        out_specs=[pl.BlockSpec((HD, B), lambda h: (h, 0)),
                   pl.BlockSpec((None, HD, HD, B), lambda h: (h, 0, 0, 0))],
        out_shape=[jax.ShapeDtypeStruct((D_BR, B), F32),
                   jax.ShapeDtypeStruct((NH, HD, HD, B), F32)],
        scratch_shapes=[pltpu.VMEM((HD, B), F32)],
        compiler_params=_cparams("parallel"),
        name="rwkv7_step",
    )(parts, parts, parts, parts, parts, prev, prev, prev, prev, mu, mu, mu, mu,
      _col_params(p["w0"]), p["w2"].T, _col_params(p["a0"]), p["a2"].T, *cols, state_t)


def _qkv_prep_kernel(pc_ref, cos_ref, sin_ref, qg_ref, kg_ref, *rest, tm):
    nd = len(C_CONFIGS)
    outs = rest[:3 * nd]
    kc_ref = rest[3 * nd]
    scr = rest[3 * nd + 1:]
    pc = pc_ref[0]
    cos = cos_ref[...]
    sin = sin_ref[...]
    q = _rope(_head_rms(pc[:, 0:D_BR], qg_ref[...]), cos, sin) * (HD ** -0.5)
    k = _rope(_head_rms(pc[:, D_BR:2 * D_BR], kg_ref[...]), cos, sin)
    kc_ref[0] = k
    for a, x in enumerate((q, k, pc[:, 2 * D_BR:3 * D_BR])):
        for half in range(D_BR // LANES):
            scr[a][half] = x[:, half * LANES:(half + 1) * LANES]
    hpl = LANES // HD
    for ci, (_, dil) in enumerate(C_CONFIGS):
        for a in range(3):
            out = outs[3 * ci + a]
            for rho in range(dil):
                for half in range(D_BR // LANES):
                    rows = scr[a][half, pl.ds(rho, tm // dil, stride=dil), :]
                    for j in range(hpl):
                        out[0, rho, half * hpl + j] = rows[:, j * HD:(j + 1) * HD].astype(BF16)


def _qkv_prep(pc, cos, sin, qg, kg):
    B, T, _ = pc.shape
    tm = min(ROW_TILE, T)
    vec = pl.BlockSpec((1, D_BR), lambda b, i: (0, 0))
    tab = pl.BlockSpec((tm, D_BR), lambda b, i: (i, 0))
    out_specs, out_shape = [], []
    for _, dil in C_CONFIGS:
        for _ in range(3):
            out_specs.append(pl.BlockSpec((1, dil, NH, tm // dil, HD), lambda b, i: (b, 0, 0, i, 0)))
            out_shape.append(jax.ShapeDtypeStruct((B, dil, NH, T // dil, HD), BF16))
    out_specs.append(pl.BlockSpec((1, tm, D_BR), lambda b, i: (b, i, 0)))
    out_shape.append(jax.ShapeDtypeStruct((B, T, D_BR), F32))
    return pl.pallas_call(
        functools.partial(_qkv_prep_kernel, tm=tm),
        grid=(B, T // tm),
        in_specs=[pl.BlockSpec((1, tm, C_W), lambda b, i: (b, i, 0)), tab, tab, vec, vec],
        out_specs=out_specs,
        out_shape=out_shape,
        scratch_shapes=[pltpu.VMEM((D_BR // LANES, tm, LANES), F32)] * 3,
        compiler_params=_cparams("parallel", "parallel"),
        name="attn_qkv_prep",
    )(pc, cos, sin, qg, kg)


def _swa_kernel(q_ref, kc_ref, vc_ref, kp_ref, vp_ref, o_ref, lse_ref, *, qt):
    i = pl.program_id(1)
    n = Q_BLOCK
    r = lax.broadcasted_iota(jnp.int32, (n, 2 * n), 0)
    c = lax.broadcasted_iota(jnp.int32, (n, 2 * n), 1)
    band = (c >= r) & (c <= r + n)
    first = band & (c >= jnp.where(i > 0, 0, n))
    units = [(u, h) for u in range(qt // n) for h in range(NH)]

    def window(cur_ref, prev_ref, u, h):
        if u == 0:
            return jnp.concatenate([prev_ref[0, h], cur_ref[0, h, 0:n, :]], axis=0)
        return cur_ref[0, h, (u - 1) * n:(u + 1) * n, :]

    s = [lax.dot_general(q_ref[0, h, u * n:(u + 1) * n, :], window(kc_ref, kp_ref, u, h),
                         (((1,), (1,)), ((), ())), preferred_element_type=F32) for u, h in units]
    s = [jnp.where(first if u == 0 else band, x, MASK_VALUE) for (u, h), x in zip(units, s)]
    m = [jnp.max(x, axis=1, keepdims=True) for x in s]
    p = [jnp.exp(x - mx) for x, mx in zip(s, m)]
    l = [jnp.sum(x, axis=1, keepdims=True) for x in p]
    o = [jnp.dot(x.astype(BF16), window(vc_ref, vp_ref, u, h), preferred_element_type=F32) / lx
         for (u, h), x, lx in zip(units, p, l)]
    lse = [jnp.broadcast_to(mx + jnp.log(lx), (n, HD)) for mx, lx in zip(m, l)]
    for u in range(qt // n):
        o_ref[0, u * n:(u + 1) * n, :] = jnp.concatenate(o[u * NH:(u + 1) * NH], axis=1)
        lse_ref[0, u * n:(u + 1) * n, :] = jnp.concatenate(lse[u * NH:(u + 1) * NH], axis=1)


def _swa(q, k, v):
    N, _, L, _ = q.shape
    n = Q_BLOCK
    qt = min(ATTN_TILE, L)
    cur = pl.BlockSpec((1, NH, qt, HD), lambda s, i: (s, 0, i, 0))
    prev = pl.BlockSpec((1, NH, n, HD), lambda s, i: (s, 0, jnp.maximum(i * (qt // n) - 1, 0), 0))
    out = pl.BlockSpec((1, qt, D_BR), lambda s, i: (s, i, 0))
    return pl.pallas_call(
        functools.partial(_swa_kernel, qt=qt),
        grid=(N, L // qt),
        in_specs=[cur, cur, cur, prev, prev],
        out_specs=[out, out],
        out_shape=[jax.ShapeDtypeStruct((N, L, D_BR), F32)] * 2,
        compiler_params=_cparams("parallel", "parallel"),
        name="sliding_window_attn",
    )(q, k, v, k, v)


def _attn_combine_kernel(*refs, tm):
    nd = len(C_CONFIGS)
    o_refs, lse_refs = refs[0:nd], refs[nd:2 * nd]
    g_ref, out_ref = refs[2 * nd], refs[2 * nd + 1]
    o_s, lse_s = refs[2 * nd + 2], refs[2 * nd + 3]
    os_, lses = [], []
    for ci, (_, dil) in enumerate(C_CONFIGS):
        if dil == 1:
            os_.append(o_refs[ci][0, 0])
            lses.append(lse_refs[ci][0, 0])
            continue
        nh = D_BR // LANES
        for rho in range(dil):
            for half in range(nh):
                ls = slice(half * LANES, (half + 1) * LANES)
                o_s[ci, half, pl.ds(rho, tm // dil, stride=dil), :] = o_refs[ci][0, rho][:, ls]
                lse_s[ci, half, pl.ds(rho, tm // dil, stride=dil), :] = lse_refs[ci][0, rho][:, ls]
        os_.append(jnp.concatenate([o_s[ci, half] for half in range(nh)], axis=1))
        lses.append(jnp.concatenate([lse_s[ci, half] for half in range(nh)], axis=1))
    top = functools.reduce(jnp.maximum, lses)
    ws = [jnp.exp(x - top) for x in lses]
    num = functools.reduce(lambda a, b: a + b, [w * o for w, o in zip(ws, os_)])
    den = functools.reduce(lambda a, b: a + b, ws)
    out_ref[0] = num / den * _silu(g_ref[0])


def _attn_combine(outs, lses, pc):
    B, T, _ = pc.shape
    tm = min(ROW_TILE, T)
    nd = len(C_CONFIGS)
    specs = [pl.BlockSpec((1, dil, tm // dil, D_BR), lambda b, i: (b, 0, i, 0)) for _, dil in C_CONFIGS]
    return pl.pallas_call(
        functools.partial(_attn_combine_kernel, tm=tm),
        grid=(B, T // tm),
        in_specs=specs + specs + [pl.BlockSpec((1, tm, D_BR), lambda b, i: (b, i, 3))],
        out_specs=pl.BlockSpec((1, tm, D_BR), lambda b, i: (b, i, 0)),
        out_shape=jax.ShapeDtypeStruct((B, T, D_BR), F32),
        scratch_shapes=[pltpu.VMEM((nd, D_BR // LANES, tm, LANES), F32)] * 2,
        compiler_params=_cparams("parallel", "parallel"),
        name="attn_combine",
    )(*outs, *lses, pc)


def _attn_seq(pc, cos, sin, qg, kg):
    B, T, _ = pc.shape
    assert T % (Q_BLOCK * max(d for _, d in C_CONFIGS)) == 0
    res = _qkv_prep(pc, cos, sin, qg, kg)
    outs, lses = [], []
    for ci, (w, dil) in enumerate(C_CONFIGS):
        assert w == Q_BLOCK * dil
        q, k, v = (x.reshape(B * dil, NH, T // dil, HD) for x in res[3 * ci:3 * ci + 3])
        o, lse = _swa(q, k, v)
        outs.append(o.reshape(B, dil, T // dil, D_BR))
        lses.append(lse.reshape(B, dil, T // dil, D_BR))
    return _attn_combine(outs, lses, pc), res[-1]


def _attn_step_kernel(pc_ref, cos_ref, sin_ref, qg_ref, kg_ref, mult_ref, kt_ref, vt_ref,
                      o_ref, knew_ref, vnew_ref, *, bb):
    pc = pc_ref[...]
    cos = cos_ref[...]
    sin = sin_ref[...]
    q = _rope(_head_rms(pc[:, 0:D_BR], qg_ref[...]), cos, sin) * (HD ** -0.5)
    k = _rope(_head_rms(pc[:, D_BR:2 * D_BR], kg_ref[...]), cos, sin)
    v = pc[:, 2 * D_BR:3 * D_BR]
    knew_ref[...] = k
    vnew_ref[...] = v
    mult = mult_ref[...]
    live = mult > 0.0
    n_self = float(len(C_CONFIGS))
    eye = _eye(HD)
    for b in range(bb):
        parts = []
        for h in range(NH):
            sl = slice(h * HD, (h + 1) * HD)
            qr, kr, vr = q[b:b + 1, sl], k[b:b + 1, sl], v[b:b + 1, sl]
            s_self = jnp.sum(qr * kr, axis=1, keepdims=True)
            s = jnp.sum(kt_ref[b, h] * _to_col(qr, eye), axis=0, keepdims=True)
            s = jnp.where(live, s, MASK_VALUE)
            m = jnp.maximum(jnp.max(s, axis=1, keepdims=True), s_self)
            p = jnp.exp(s - m) * mult
            p_self = n_self * jnp.exp(s_self - m)
            den = jnp.sum(p, axis=1, keepdims=True) + p_self
            acc = _to_row(jnp.sum(vt_ref[b, h] * p, axis=1, keepdims=True), eye) + p_self * vr
            parts.append(acc / den)
        o_ref[b:b + 1, :] = jnp.concatenate(parts, axis=1)
    o_ref[...] = o_ref[...] * _silu(pc[:, 3 * D_BR:4 * D_BR])


def _attn_step(pc, cos, sin, qg, kg, cache_kt, cache_vt, layer):
    B = pc.shape[0]
    P = cache_kt.shape[-1]
    bb = min(ATTN_STEP_ROWS, B)
    dist = P - jnp.arange(P)
    mult = jnp.zeros((P,), F32)
    for w, dil in C_CONFIGS:
        mult = mult + ((dist <= w) & (dist % dil == 0)).astype(F32)
    vec = pl.BlockSpec((1, D_BR), lambda i: (0, 0))
    row = pl.BlockSpec((None, bb, D_BR), lambda i: (i, 0, 0))
    cache = pl.BlockSpec((None, bb, NH, HD, P), lambda i: (layer, i, 0, 0, 0))
    outs = pl.pallas_call(
        functools.partial(_attn_step_kernel, bb=bb),
        grid=(B // bb,),
        in_specs=[pl.BlockSpec((None, bb, C_W), lambda i: (i, 0, 0)), vec, vec, vec, vec,
                  pl.BlockSpec((1, P), lambda i: (0, 0)), cache, cache],
        out_specs=[row, row, row],
        out_shape=[jax.ShapeDtypeStruct((B // bb, bb, D_BR), F32)] * 3,
        compiler_params=_cparams("parallel"),
        name="dilated_attn_step",
    )(pc.reshape(B // bb, bb, C_W), cos, sin, qg, kg, mult.reshape(1, P), cache_kt, cache_vt)
    return [o.reshape(B, D_BR) for o in outs]


def _pool_mix(pooled, u, g, w_ref, scale_ref):
    mixed = _bdot(pooled - u, w_ref[...]) * scale_ref[...]
    return mixed * _silu(g)


def _pool_select(sums, cnts):
    lane = lax.broadcasted_iota(jnp.int32, sums[0].shape, 1) // (D_BR // len(POOL_WINDOWS))
    out = sums[-1] / cnts[-1]
    for gi in range(len(POOL_WINDOWS) - 2, -1, -1):
        out = jnp.where(lane == gi, sums[gi] / cnts[gi], out)
    return out


def _pool_seq_kernel(pd_ref, buf0_ref, w_ref, scale_ref, o_ref, carry_s, *, tb, t0):
    step = pl.program_id(1)

    @pl.when(step == 0)
    def _():
        carry_s[...] = buf0_ref[0]

    pd = pd_ref[0]
    u = pd[:, 0:D_BR]
    ext = jnp.concatenate([carry_s[...], u], axis=0)
    carry_s[...] = u[tb - POOL_MAX:tb, :]
    pos = (t0 + step * tb + lax.broadcasted_iota(jnp.int32, (tb, 1), 0)).astype(F32)
    sums, cnts = [], []
    run = ext
    shift = 1
    for w in POOL_WINDOWS:
        while shift < w:
            run = run + pltpu.roll(run, shift, 0)
            shift *= 2
        sums.append(run[POOL_MAX:, :])
        cnts.append(jnp.minimum(float(w), pos + 1.0))
    o_ref[0] = _pool_mix(_pool_select(sums, cnts), u, pd[:, D_BR:2 * D_BR], w_ref, scale_ref)


def _pool_seq(pd, buf0, w_bd, scale, t0):
    B, T, _ = pd.shape
    tb = min(ROW_TILE, T)
    return pl.pallas_call(
        functools.partial(_pool_seq_kernel, tb=tb, t0=t0),
        grid=(B, T // tb),
        in_specs=[pl.BlockSpec((1, tb, D_W), lambda b, i: (b, i, 0)),
                  pl.BlockSpec((1, POOL_MAX, D_BR), lambda b, i: (b, 0, 0)),
                  pl.BlockSpec((D_BR, D_BR), lambda b, i: (0, 0)),
                  pl.BlockSpec((1, D_BR), lambda b, i: (0, 0))],
        out_specs=pl.BlockSpec((1, tb, D_BR), lambda b, i: (b, i, 0)),
        out_shape=jax.ShapeDtypeStruct((B, T, D_BR), F32),
        scratch_shapes=[pltpu.VMEM((POOL_MAX, D_BR), F32)],
        compiler_params=_cparams("parallel", "arbitrary"),
        name="pool_seq",
    )(pd, buf0, w_bd, scale)


def _pool_step_kernel(pd_ref, buf_ref, w_ref, scale_ref, o_ref, *, t0):
    pd = pd_ref[...]
    u = pd[:, 0:D_BR]
    nbuf = POOL_MAX - 1
    sums, cnts = [], []
    run = u
    taken = 1
    for w in POOL_WINDOWS:
        while taken < w:
            run = run + buf_ref[nbuf - taken]
            taken += 1
        sums.append(run)
        cnts.append(float(min(w, t0 + 1)))
    o_ref[...] = _pool_mix(_pool_select(sums, cnts), u, pd[:, D_BR:2 * D_BR], w_ref, scale_ref)


def _pool_step(pd, buf, layer, w_bd, scale, t0):
    B = pd.shape[0]
    bb = min(POOL_STEP_ROWS, B)
    return pl.pallas_call(
        functools.partial(_pool_step_kernel, t0=t0),
        grid=(B // bb,),
        in_specs=[pl.BlockSpec((bb, D_W), lambda i: (i, 0)),
                  pl.BlockSpec((None, POOL_MAX - 1, bb, D_BR), lambda i: (layer, 0, i, 0)),
                  pl.BlockSpec((D_BR, D_BR), lambda i: (0, 0)),
                  pl.BlockSpec((1, D_BR), lambda i: (0, 0))],
        out_specs=pl.BlockSpec((bb, D_BR), lambda i: (i, 0)),
        out_shape=jax.ShapeDtypeStruct((B, D_BR), F32),
        compiler_params=_cparams("parallel"),
        name="pool_step",
    )(pd, buf, w_bd, scale)


def _rope_tables(pos):
    half = HD // 2
    inv_freq = jnp.float32(ROPE_THETA) ** (-jnp.arange(half, dtype=F32) / half)
    ang = pos.astype(F32)[:, None] * inv_freq[None, :]
    cos = jnp.cos(ang)
    sin = jnp.sin(ang)
    cos_t = jnp.tile(jnp.concatenate([cos, cos], axis=-1), (1, NH))
    sin_t = jnp.tile(jnp.concatenate([-sin, sin], axis=-1), (1, NH))
    return cos_t, sin_t


def _block_diag(w):
    g, c, d = w.shape
    out = jnp.zeros((g * c, g * d), w.dtype)
    for i in range(g):
        out = out.at[i * c:(i + 1) * c, i * d:(i + 1) * d].set(w[i])
    return out


def _layer_params(l, W):
    row = lambda x: x[l].reshape(1, -1)
    w_in = W["w_in"][l].astype(BF16)
    w_in_t = jnp.swapaxes(W["w_in"], 1, 2)[l].astype(BF16)
    cuts = (0, A_W, A_W + B_W, A_W + B_W + C_W, A_W + B_W + C_W + D_W)
    return dict(
        norm_g=W["norm_g"][l],
        w_parts=[w_in[:, cuts[i]:cuts[i + 1]] for i in range(4)],
        w_parts_t=[w_in_t[cuts[i]:cuts[i + 1], :] for i in range(2)],
        w_out=W["w_out"][l].astype(BF16),
        onorm_g=row(W["a_onorm_g"]),
        rwkv=[row(W["b_mu"]), row(W["b_w0"]), W["b_w2"][l], row(W["b_a0"]), W["b_a2"][l],
              row(W["b_k_k"]), row(W["b_k_a"]), row(W["b_r_k"]), row(W["b_gn_w"]), row(W["b_gn_b"])],
        qg=jnp.tile(W["c_qnorm_g"][l], NH).reshape(1, D_BR),
        kg=jnp.tile(W["c_knorm_g"][l], NH).reshape(1, D_BR),
        w_pool=_block_diag(W["d_w_pool"][l]),
        p_scale=row(W["d_scale"]),
    )


def _run_prompt(x, mods, layers, log_lb, log1m_lb):
    B, T, _ = x.shape
    cos, sin = _rope_tables(jnp.arange(T))
    c_keep = min(C_WIN_MAX, T)
    d_keep = min(POOL_MAX - 1, T)
    zero_state = jnp.zeros((B, NH, HD, HD), F32)
    zero_shift = jnp.zeros((B, 1, B_SHIFT_W), F32)
    zero_pool = jnp.zeros((B, POOL_MAX, D_BR), F32)
    nA, nBw, nBs, nK, nV, nD = [], [], [], [], [], []
    for l, L in enumerate(layers):
        shift, scale, gate = (mods[l][:, None, i * D_MODEL:(i + 1) * D_MODEL] for i in range(3))
        pa, pb, pc, pd = _in_proj(x, shift, scale, L["norm_g"], L["w_parts"])
        oa, sa_t = _hgrn2_seq(pa, zero_state, log_lb[l], log1m_lb[l], L["onorm_g"])
        ob, sb, sbs = _rwkv_seq(pb, zero_shift, zero_state, L["rwkv"])
        oc, k_rot = _attn_seq(pc, cos, sin, L["qg"], L["kg"])
        od = _pool_seq(pd, zero_pool, L["w_pool"], L["p_scale"], 0)
        x = _out_proj(oa, ob, oc, od, x, gate, L["w_out"])
        nA.append(jnp.swapaxes(sa_t, -1, -2))
        nBw.append(sb)
        nBs.append(sbs[:, 0])
        nK.append(k_rot[:, T - c_keep:].reshape(B, c_keep, NH, HD))
        nV.append(pc[:, T - c_keep:, 2 * D_BR:3 * D_BR].reshape(B, c_keep, NH, HD))
        nD.append(pd[:, T - d_keep:, 0:D_BR])
    return (x,) + tuple(jnp.stack(a) for a in (nA, nBw, nBs, nK, nV, nD))


def _run_sample(x, mods, layers, log_lb, log1m_lb, sA, sBw, sBs, ck, cv, cd):
    B, T, _ = x.shape
    assert T == 1 and ck.shape[2] == PAST_LEN == C_WIN_MAX
    cos, sin = _rope_tables(jnp.full((1,), PAST_LEN))
    ck_t = jnp.transpose(ck, (0, 1, 3, 4, 2))
    cv_t = jnp.transpose(cv, (0, 1, 3, 4, 2))
    sA_t = jnp.transpose(sA, (0, 2, 3, 4, 1))
    sBw_t = jnp.transpose(sBw, (0, 2, 3, 4, 1))
    sBs_t = jnp.transpose(sBs, (0, 2, 1))
    cd_t = jnp.transpose(cd, (0, 2, 1, 3))
    xr = x.reshape(B, D_MODEL)
    nA, nBw, nBs, nK, nV, nD = [], [], [], [], [], []
    for l, L in enumerate(layers):
        shift, scale, gate = (mods[l][:, i * D_MODEL:(i + 1) * D_MODEL] for i in range(3))
        pa_t, pb_t, pc, pd = _in_proj_step(xr, shift, scale, L["norm_g"], *L["w_parts_t"], *L["w_parts"][2:])
        oa_t, sa = _hgrn2_step(pa_t, sA_t, l, log_lb[l], log1m_lb[l], L["onorm_g"])
        ob_t, sb = _rwkv_step(pb_t, sBs_t, sBw_t, l, L["rwkv"])
        oc, k_new, v_new = _attn_step(pc, cos, sin, L["qg"], L["kg"], ck_t, cv_t, l)
        od = _pool_step(pd, cd_t, l, L["w_pool"], L["p_scale"], PAST_LEN)
        xr = _out_proj_step(oa_t, ob_t, oc, od, xr, gate, L["w_out"])
        nA.append(sa)
        nBw.append(sb)
        nBs.append(pb_t[0:B_SHIFT_W])
        nK.append(k_new.reshape(B, 1, NH, HD))
        nV.append(v_new.reshape(B, 1, NH, HD))
        nD.append(pd[:, None, 0:D_BR])
    back = lambda a: jnp.transpose(jnp.stack(a), (0, 4, 1, 2, 3))
    return (xr.reshape(B, 1, D_MODEL), back(nA), back(nBw), jnp.transpose(jnp.stack(nBs), (0, 2, 1)),
            jnp.stack(nK), jnp.stack(nV), jnp.stack(nD))


def kernel(x_prompt, x_sample, state_A, state_B_wkv, state_B_shift, cache_C_k, cache_C_v, cache_D_pool, c_prompt, c_sample, ada_w, ada_b, norm_g, w_in, w_out, a_lb_logits, a_onorm_g, b_mu, b_w0, b_w2, b_a0, b_a2, b_k_k, b_k_a, b_r_k, b_gn_w, b_gn_b, c_qnorm_g, c_knorm_g, d_w_pool, d_scale):
    W = dict(norm_g=norm_g, w_in=w_in, w_out=w_out, a_onorm_g=a_onorm_g, b_mu=b_mu, b_w0=b_w0,
             b_w2=b_w2, b_a0=b_a0, b_a2=b_a2, b_k_k=b_k_k, b_k_a=b_k_a, b_r_k=b_r_k,
             b_gn_w=b_gn_w, b_gn_b=b_gn_b, c_qnorm_g=c_qnorm_g, c_knorm_g=c_knorm_g,
             d_w_pool=d_w_pool, d_scale=d_scale)
    layers = [_layer_params(l, W) for l in range(DEPTH)]
    lb_sm = jax.nn.softmax(a_lb_logits.astype(F32), axis=0)
    lower = jnp.cumsum(lb_sm, axis=0) - lb_sm[0:1]
    log_lb = jnp.log(jnp.maximum(lower, LB_FLOOR))[:, None, :]
    log1m_lb = jnp.log1p(-lower)[:, None, :]

    nb = x_prompt.shape[0]
    mods = _modulation(jnp.concatenate([c_prompt, c_sample], axis=0), ada_w, ada_b)
    y_p, pA, pBw, pBs, pK, pV, pD = _run_prompt(x_prompt, mods[:, :nb], layers, log_lb, log1m_lb)
    y_s, sA, sBw, sBs, sK, sV, sD = _run_sample(
        x_sample, mods[:, nb:], layers, log_lb, log1m_lb,
        state_A, state_B_wkv, state_B_shift, cache_C_k, cache_C_v, cache_D_pool)
    return (y_p, y_s, pA, sA, pBw, sBw, pBs, sBs, pK, sK, pV, sV, pD, sD)
```

```python
import functools

import jax
import jax.numpy as jnp
from jax import lax
from jax.experimental import pallas as pl
from jax.experimental.pallas import tpu as pltpu

F32 = jnp.float32
BF16 = jnp.bfloat16

D_MODEL = 1024
DEPTH = 4
PAST_LEN = 2048
D_BR = 256
HD = 64
NH = D_BR // HD
LB_FLOOR = 1e-30
LORA = 32
RWKV_GN_EPS = 64e-5
C_CONFIGS = ((128, 1), (512, 4), (2048, 16))
C_WIN_MAX = 2048
MASK_VALUE = -1e30
ROPE_THETA = 10000.0
POOL_WINDOWS = (2, 4, 8, 16)
POOL_MAX = 16
EPS = 1e-6
A_W = 4 * D_BR
B_SHIFT_W = 3 * D_BR + 2 * LORA
B_W = B_SHIFT_W + D_BR
C_W = 4 * D_BR
D_W = 2 * D_BR

LANES = 128

ROW_TILE = 512
MIX_TILE = 256
CHUNK_A = 8
CHUNKS_PER_IT_A = 8
CHUNK_B = 64
INV_BLOCK = 16
CHUNKS_PER_PASS = 4
Q_BLOCK = 128
ATTN_TILE = 512
POOL_STEP_ROWS = 32
VMEM_LIMIT = 56 * 1024 * 1024


def _cparams(*sem):
    return pltpu.CompilerParams(dimension_semantics=sem, vmem_limit_bytes=VMEM_LIMIT)


def _sigmoid(x):
    return 1.0 / (1.0 + jnp.exp(-x))


def _silu(x):
    return x * _sigmoid(x)


def _softplus(x):
    return jnp.maximum(x, 0.0) + jnp.log1p(jnp.exp(-jnp.abs(x)))


def _log_sigmoid(x):
    return -_softplus(-x)


def _logaddexp(a, b):
    return jnp.maximum(a, b) + jnp.log1p(jnp.exp(-jnp.abs(a - b)))


def _bdot(a, b):
    return jnp.dot(a.astype(BF16), b.astype(BF16), preferred_element_type=F32)


def _bdot_nt(a, b):
    return lax.dot_general(a.astype(BF16), b.astype(BF16), (((1,), (1,)), ((), ())),
                           preferred_element_type=F32)


def _bdot_tn(a, b):
    return lax.dot_general(a.astype(BF16), b.astype(BF16), (((0,), (0,)), ((), ())),
                           preferred_element_type=F32)


def _split3(x):
    h1 = x.astype(BF16)
    r1 = x - h1.astype(F32)
    h2 = r1.astype(BF16)
    r2 = r1 - h2.astype(F32)
    return h1, h2, r2.astype(BF16)


def _seg_sum(x):
    n = x.shape[-1]
    r = lax.broadcasted_iota(jnp.int32, (n, n), 0) // HD
    c = lax.broadcasted_iota(jnp.int32, (n, n), 1) // HD
    ones = jnp.where(r == c, 1.0, 0.0).astype(BF16)
    h1, h2, h3 = _split3(x)
    d = lambda h: jnp.dot(h, ones, preferred_element_type=F32)
    return d(h1) + d(h2) + d(h3)


def _chunk_cumsum(x, chunk):
    t = x.shape[0]
    r = lax.broadcasted_iota(jnp.int32, (t, t), 0)
    c = lax.broadcasted_iota(jnp.int32, (t, t), 1)
    tri = jnp.where((r // chunk == c // chunk) & (c <= r), 1.0, 0.0).astype(BF16)
    h1, h2, h3 = _split3(x)
    d = lambda h: jnp.dot(tri, h, preferred_element_type=F32)
    return d(h1) + d(h2) + d(h3)


def _head_rms(x, gamma):
    ms = _seg_sum(x * x) * (1.0 / HD)
    return x * lax.rsqrt(ms + EPS) * gamma


def _rope(x, cos, sin_signed):
    lane = lax.broadcasted_iota(jnp.int32, x.shape, 1) % HD
    partner = jnp.where(lane < HD // 2,
                        pltpu.roll(x, D_BR - HD // 2, 1),
                        pltpu.roll(x, HD // 2, 1))
    return x * cos + partner * sin_signed


def _eye(n):
    r = lax.broadcasted_iota(jnp.int32, (n, n), 0)
    c = lax.broadcasted_iota(jnp.int32, (n, n), 1)
    return jnp.where(r == c, 1.0, 0.0).astype(F32)


def _to_col(row, eye):
    return jnp.sum(eye * row, axis=1, keepdims=True)


def _to_row(col, eye):
    return jnp.sum(eye * col, axis=0, keepdims=True)


def _mod_kernel(c_ref, w_ref, b_ref, o_ref):
    c = _silu(c_ref[...])
    o_ref[0] = _bdot(c, w_ref[0]) + b_ref[0]


def _modulation(c_all, ada_w, ada_b):
    n = c_all.shape[0]
    tn = 512
    return pl.pallas_call(
        _mod_kernel,
        grid=(DEPTH, 3 * D_MODEL // tn),
        in_specs=[pl.BlockSpec((n, D_MODEL), lambda l, j: (0, 0)),
                  pl.BlockSpec((1, D_MODEL, tn), lambda l, j: (l, 0, j)),
                  pl.BlockSpec((1, 1, tn), lambda l, j: (l, 0, j))],
        out_specs=pl.BlockSpec((1, n, tn), lambda l, j: (l, 0, j)),
        out_shape=jax.ShapeDtypeStruct((DEPTH, n, 3 * D_MODEL), F32),
        compiler_params=_cparams("parallel", "parallel"),
        name="adaln_modulation",
    )(c_all, ada_w, ada_b.reshape(DEPTH, 1, 3 * D_MODEL))


def _in_proj_kernel(x_ref, shift_ref, scale_ref, g_ref, wa_ref, wb_ref, wc_ref, wd_ref,
                    pa_ref, pb_ref, pc_ref, pd_ref):
    x = x_ref[0]
    ms = jnp.mean(x * x, axis=-1, keepdims=True)
    h = x * lax.rsqrt(ms + EPS) * g_ref[...]
    h = (h * (1.0 + scale_ref[0]) + shift_ref[0]).astype(BF16)
    pa_ref[0] = jnp.dot(h, wa_ref[...], preferred_element_type=F32)
    pb_ref[0] = jnp.dot(h, wb_ref[...], preferred_element_type=F32)
    pc_ref[0] = jnp.dot(h, wc_ref[...], preferred_element_type=F32)
    pd_ref[0] = jnp.dot(h, wd_ref[...], preferred_element_type=F32)


def _in_proj(x, shift, scale, norm_g, w_parts):
    B, T, D = x.shape
    tm = min(ROW_TILE, T)
    mt = shift.shape[1]
    mod_block = (1, tm, D) if mt == T else (1, 1, D)
    mod_map = (lambda b, i: (b, i, 0)) if mt == T else (lambda b, i: (b, 0, 0))
    widths = [w.shape[1] for w in w_parts]
    return pl.pallas_call(
        _in_proj_kernel,
        grid=(B, T // tm),
        in_specs=[pl.BlockSpec((1, tm, D), lambda b, i: (b, i, 0)),
                  pl.BlockSpec(mod_block, mod_map),
                  pl.BlockSpec(mod_block, mod_map),
                  pl.BlockSpec((1, D), lambda b, i: (0, 0))]
                 + [pl.BlockSpec((D, w), lambda b, i: (0, 0)) for w in widths],
        out_specs=[pl.BlockSpec((1, tm, w), lambda b, i: (b, i, 0)) for w in widths],
        out_shape=[jax.ShapeDtypeStruct((B, T, w), F32) for w in widths],
        compiler_params=_cparams("parallel", "parallel"),
        name="norm_in_proj",
    )(x, shift, scale, norm_g.reshape(1, D), *w_parts)


def _in_proj_step_kernel(x_ref, shift_ref, scale_ref, g_ref, wat_ref, wbt_ref, wc_ref, wd_ref,
                         pat_ref, pbt_ref, pc_ref, pd_ref):
    x = x_ref[...]
    ms = jnp.mean(x * x, axis=-1, keepdims=True)
    h = x * lax.rsqrt(ms + EPS) * g_ref[...]
    h = (h * (1.0 + scale_ref[...]) + shift_ref[...]).astype(BF16)
    nt = lambda w: lax.dot_general(w, h, (((1,), (1,)), ((), ())), preferred_element_type=F32)
    pat_ref[...] = nt(wat_ref[...])
    pbt_ref[...] = nt(wbt_ref[...])
    pc_ref[...] = jnp.dot(h, wc_ref[...], preferred_element_type=F32)
    pd_ref[...] = jnp.dot(h, wd_ref[...], preferred_element_type=F32)


def _in_proj_step(x, shift, scale, norm_g, wa_t, wb_t, wc, wd):
    B, D = x.shape
    ins = (x, shift, scale, norm_g.reshape(1, D), wa_t, wb_t, wc, wd)
    shapes = [(wa_t.shape[0], B), (wb_t.shape[0], B), (B, wc.shape[1]), (B, wd.shape[1])]
    return pl.pallas_call(
        _in_proj_step_kernel,
        grid=(1,),
        in_specs=[pl.BlockSpec(a.shape, lambda i: (0, 0)) for a in ins],
        out_specs=[pl.BlockSpec(s, lambda i: (0, 0)) for s in shapes],
        out_shape=[jax.ShapeDtypeStruct(s, F32) for s in shapes],
        compiler_params=_cparams("arbitrary"),
        name="norm_in_proj_step",
    )(*ins)


def _out_proj_step_kernel(oat_ref, obt_ref, oc_ref, od_ref, x_ref, gate_ref, w_ref, y_ref):
    tn = lambda a, w: lax.dot_general(a.astype(BF16), w, (((0,), (0,)), ((), ())), preferred_element_type=F32)
    nn = lambda a, w: jnp.dot(a.astype(BF16), w, preferred_element_type=F32)
    res = (tn(oat_ref[...], w_ref[0:D_BR, :]) + tn(obt_ref[...], w_ref[D_BR:2 * D_BR, :])
           + nn(oc_ref[...], w_ref[2 * D_BR:3 * D_BR, :]) + nn(od_ref[...], w_ref[3 * D_BR:4 * D_BR, :]))
    y_ref[...] = x_ref[...] + gate_ref[...] * res


def _out_proj_step(oa_t, ob_t, oc, od, x, gate, w_out):
    ins = (oa_t, ob_t, oc, od, x, gate, w_out)
    return pl.pallas_call(
        _out_proj_step_kernel,
        grid=(1,),
        in_specs=[pl.BlockSpec(a.shape, lambda i: (0, 0)) for a in ins],
        out_specs=pl.BlockSpec(x.shape, lambda i: (0, 0)),
        out_shape=jax.ShapeDtypeStruct(x.shape, F32),
        compiler_params=_cparams("arbitrary"),
        name="out_proj_residual_step",
    )(*ins)


def _out_proj_kernel(oa_ref, ob_ref, oc_ref, od_ref, x_ref, gate_ref, w_ref, y_ref):
    mix = jnp.concatenate([oa_ref[0], ob_ref[0], oc_ref[0], od_ref[0]], axis=-1).astype(BF16)
    res = jnp.dot(mix, w_ref[...], preferred_element_type=F32)
    y_ref[0] = x_ref[0] + gate_ref[0] * res


def _out_proj(oa, ob, oc, od, x, gate, w_out):
    B, T, D = x.shape
    tm = min(ROW_TILE, T)
    mt = gate.shape[1]
    mod_block = (1, tm, D) if mt == T else (1, 1, D)
    mod_map = (lambda b, i: (b, i, 0)) if mt == T else (lambda b, i: (b, 0, 0))
    br = pl.BlockSpec((1, tm, D_BR), lambda b, i: (b, i, 0))
    return pl.pallas_call(
        _out_proj_kernel,
        grid=(B, T // tm),
        in_specs=[br, br, br, br,
                  pl.BlockSpec((1, tm, D), lambda b, i: (b, i, 0)),
                  pl.BlockSpec(mod_block, mod_map),
                  pl.BlockSpec((D, D), lambda b, i: (0, 0))],
        out_specs=pl.BlockSpec((1, tm, D), lambda b, i: (b, i, 0)),
        out_shape=jax.ShapeDtypeStruct((B, T, D), F32),
        compiler_params=_cparams("parallel", "parallel"),
        name="out_proj_residual",
    )(oa, ob, oc, od, x, gate, w_out)


def _hgrn2_gates(pa, log_lb, log1m_lb):
    q = _silu(pa[:, 0:D_BR])
    fl = pa[:, D_BR:2 * D_BR]
    v = pa[:, 2 * D_BR:3 * D_BR]
    ls_neg = _log_sigmoid(-fl)
    logf = _logaddexp(_log_sigmoid(fl), log_lb + ls_neg)
    k = jnp.exp(log1m_lb + ls_neg)
    return q, logf, k, v


def _hgrn2_seq_kernel(pa_ref, s0_ref, loglb_ref, log1mlb_ref, og_ref, o_ref, sfin_ref,
                      q_s, k_s, v_s, g_s, o_s, st_s, *, tb, chunk, per_it):
    step = pl.program_id(1)
    hpl = LANES // HD
    npair = NH // hpl
    rl = lax.broadcasted_iota(jnp.int32, (LANES, LANES), 0) // HD
    cl = lax.broadcasted_iota(jnp.int32, (LANES, LANES), 1) // HD
    same_head = rl == cl

    @pl.when(step == 0)
    def _():
        for p in range(npair):
            blocks = [jnp.concatenate([s0_ref[0, p * hpl + a] if a == b else jnp.zeros((HD, HD), F32)
                                       for b in range(hpl)], axis=1) for a in range(hpl)]
            st_s[p] = jnp.concatenate(blocks, axis=0)

    q, logf, k, v = _hgrn2_gates(pa_ref[0], loglb_ref[...], log1mlb_ref[...])
    gcum = _chunk_cumsum(logf, chunk)
    for p in range(npair):
        sl = slice(p * LANES, (p + 1) * LANES)
        q_s[p] = q[:, sl]
        k_s[p] = k[:, sl]
        v_s[p] = v[:, sl]
        g_s[p] = gcum[:, sl]

    rows = lax.broadcasted_iota(jnp.int32, (chunk, 1), 0)
    lane_head = lax.broadcasted_iota(jnp.int32, (chunk, LANES), 1) // HD

    def body(it, carry):
        units = [(j, p) for j in range(per_it) for p in range(npair)]
        starts = [pl.multiple_of((it * per_it + j) * chunk, chunk) for j in range(per_it)]
        ld = lambda ref: [ref[p, pl.ds(starts[j], chunk), :] for j, p in units]
        qc, kc, vc, gc = ld(q_s), ld(k_s), ld(v_s), ld(g_s)
        glast = [g[chunk - 1:chunk, :] for g in gc]
        upd = [jnp.where(same_head, _bdot_tn(v_, k_ * jnp.exp(gl - g)), 0.0)
               for v_, k_, gl, g in zip(vc, kc, glast, gc)]
        decay = [jnp.exp(gl) for gl in glast]
        qe = [q_ * jnp.exp(g) for q_, g in zip(qc, gc)]
        o_acc = [jnp.zeros((chunk, LANES), F32) for _ in units]
        for t in range(chunk):
            x = [jnp.exp(jnp.minimum(g[t:t + 1, :] - g, 0.0)) * k_ * q_[t:t + 1, :]
                 for g, k_, q_ in zip(gc, kc, qc)]
            sel = [(lane_head == a) & (rows <= t) for a in range(hpl)]
            att = []
            for xi in x:
                sums = [jnp.sum(jnp.where(m, xi, 0.0), axis=1, keepdims=True) for m in sel]
                a_ = sums[-1]
                for a in range(hpl - 2, -1, -1):
                    a_ = jnp.where(lane_head == a, sums[a], a_)
                att.append(a_)
            ot = [jnp.sum(a * v_, axis=0, keepdims=True) for a, v_ in zip(att, vc)]
            o_acc = [jnp.where(rows == t, r, o) for o, r in zip(o_acc, ot)]
        st = [st_s[p] for p in range(npair)]
        for i, (j, p) in enumerate(units):
            o_s[pl.ds(starts[j], chunk), p * LANES:(p + 1) * LANES] = o_acc[i] + _bdot_nt(qe[i], st[p])
            st[p] = st[p] * decay[i] + upd[i]
        for p in range(npair):
            st_s[p] = st[p]
        return carry

    lax.fori_loop(0, tb // (chunk * per_it), body, 0)

    o = _head_rms(o_s[...], og_ref[...])
    o_ref[0] = o * _silu(pa_ref[0][:, 3 * D_BR:4 * D_BR])

    @pl.when(step == pl.num_programs(1) - 1)
    def _():
        for h in range(NH):
            a = (h % hpl) * HD
            sfin_ref[0, h] = st_s[h // hpl][a:a + HD, a:a + HD]


def _hgrn2_seq(pa, s0_t, log_lb, log1m_lb, onorm_g):
    B, T, _ = pa.shape
    tb = min(MIX_TILE, T)
    vec = pl.BlockSpec((1, D_BR), lambda b, i: (0, 0))
    st = pl.BlockSpec((1, NH, HD, HD), lambda b, i: (b, 0, 0, 0))
    return pl.pallas_call(
        functools.partial(_hgrn2_seq_kernel, tb=tb, chunk=CHUNK_A, per_it=CHUNKS_PER_IT_A),
        grid=(B, T // tb),
        in_specs=[pl.BlockSpec((1, tb, A_W), lambda b, i: (b, i, 0)), st, vec, vec, vec],
        out_specs=[pl.BlockSpec((1, tb, D_BR), lambda b, i: (b, i, 0)), st],
        out_shape=[jax.ShapeDtypeStruct((B, T, D_BR), F32),
                   jax.ShapeDtypeStruct((B, NH, HD, HD), F32)],
        scratch_shapes=[pltpu.VMEM((D_BR // LANES, tb, LANES), F32)] * 4
                       + [pltpu.VMEM((tb, D_BR), F32), pltpu.VMEM((D_BR // LANES, LANES, LANES), F32)],
        compiler_params=_cparams("parallel", "arbitrary"),
        name="hgrn2_seq",
    )(pa, s0_t, log_lb, log1m_lb, onorm_g)


def _hgrn2_step_kernel(q_ref, f_ref, i_ref, g_ref, s_ref, loglb_ref, log1mlb_ref, og_ref, o_ref, snew_ref):
    q = _silu(q_ref[...])
    fl = f_ref[...]
    v = i_ref[...]
    ls_neg = _log_sigmoid(-fl)
    f = jnp.exp(_logaddexp(_log_sigmoid(fl), loglb_ref[...] + ls_neg))
    k = jnp.exp(log1mlb_ref[...] + ls_neg)
    o = jnp.zeros_like(v)
    for kk in range(HD):
        s_new = f[kk:kk + 1, :] * s_ref[kk] + k[kk:kk + 1, :] * v
        snew_ref[kk] = s_new
        o = o + q[kk:kk + 1, :] * s_new
    ms = jnp.mean(o * o, axis=0, keepdims=True)
    o_ref[...] = o * lax.rsqrt(ms + EPS) * og_ref[...] * _silu(g_ref[...])


def _col_params(x):
    return x.reshape(NH, HD, 1)


def _hgrn2_step(pa_t, state_t, layer, log_lb, log1m_lb, onorm_g):
    B = pa_t.shape[1]
    parts = pa_t.reshape(A_W // HD, HD, B)
    part = lambda j: pl.BlockSpec((None, HD, B), lambda h: (j * NH + h, 0, 0))
    col = pl.BlockSpec((None, HD, 1), lambda h: (h, 0, 0))
    return pl.pallas_call(
        _hgrn2_step_kernel,
        grid=(NH,),
        in_specs=[part(0), part(1), part(2), part(3),
                  pl.BlockSpec((None, None, HD, HD, B), lambda h: (layer, h, 0, 0, 0)), col, col, col],
        out_specs=[pl.BlockSpec((HD, B), lambda h: (h, 0)),
                   pl.BlockSpec((None, HD, HD, B), lambda h: (h, 0, 0, 0))],
        out_shape=[jax.ShapeDtypeStruct((D_BR, B), F32),
                   jax.ShapeDtypeStruct((NH, HD, HD, B), F32)],
        compiler_params=_cparams("parallel"),
        name="hgrn2_step",
    )(parts, parts, parts, parts, state_t, _col_params(log_lb), _col_params(log1m_lb), _col_params(onorm_g))


def _rwkv_prep(xs, prev, p):
    xm = xs + (prev - xs) * p["mu"]
    r = xm[:, 0:D_BR]
    k = xm[:, D_BR:2 * D_BR]
    v = xm[:, 2 * D_BR:3 * D_BR]
    xw = xm[:, 3 * D_BR:3 * D_BR + LORA]
    xa = xm[:, 3 * D_BR + LORA:3 * D_BR + 2 * LORA]
    w = -_softplus(-(p["w0"] + _bdot(jnp.tanh(xw), p["w2"]))) - 0.5
    logw = -jnp.exp(w)
    a = _sigmoid(p["a0"] + _bdot(xa, p["a2"]))
    kk = k * p["k_k"]
    kk = kk / jnp.maximum(jnp.sqrt(_seg_sum(kk * kk)), 1e-12)
    k = k * (1.0 + (a - 1.0) * p["k_a"])
    bonus = _seg_sum(r * k * p["r_k"]) * v
    return r, logw, k, v, -kk, kk * a, bonus


def _rwkv_post(y, bonus, g, p):
    mean = _seg_sum(y) * (1.0 / HD)
    yc = y - mean
    var = _seg_sum(yc * yc) * (1.0 / HD)
    yn = yc * lax.rsqrt(var + RWKV_GN_EPS) * p["gn_w"] + p["gn_b"]
    return (yn + bonus) * _silu(g)


def _unit_lower_inverse(m, m_t, n):
    nb = INV_BLOCK
    rows = lax.broadcasted_iota(jnp.int32, (nb, 1), 0)
    nm = len(m)
    per = n // nb
    group = lax.broadcasted_iota(jnp.int32, (nb, n), 1) // nb
    diag_t = []
    for q in range(nm):
        dq = m_t[q][0:nb, :]
        for b in range(1, per):
            dq = jnp.where(group == b, m_t[q][b * nb:(b + 1) * nb, :], dq)
        diag_t.append(dq)
    qpl = LANES // n
    tiles = [jnp.concatenate(diag_t[t * qpl:(t + 1) * qpl], axis=1) for t in range(nm // qpl)]
    lane = lax.broadcasted_iota(jnp.int32, (nb, LANES), 1)
    base = lane - lane % nb
    ds = [jnp.where(lane % nb == rows, 1.0, 0.0).astype(F32)] * len(tiles)
    for i in range(1, nb):
        coef = [jnp.take_along_axis(t, base + i, axis=1) for t in tiles]
        rws = [jnp.sum(cf * d, axis=0, keepdims=True) for cf, d in zip(coef, ds)]
        ds = [d + jnp.where(rows == i, rw, 0.0) for d, rw in zip(ds, rws)]
    blocks = [[ds[q // qpl][:, (q % qpl) * n + b * nb:(q % qpl) * n + (b + 1) * nb] for b in range(per)]
              for q in range(nm)]
    size = nb
    while size < n:
        pairs = [(q, j) for q in range(nm) for j in range(0, len(blocks[q]), 2)]
        inner = [_rw_inv(m[q][(j + 1) * size:(j + 2) * size, j * size:(j + 1) * size], blocks[q][j])
                 for q, j in pairs]
        cross = [_rw_inv(blocks[q][j + 1], t) for (q, j), t in zip(pairs, inner)]
        merged = [[] for _ in range(nm)]
        for (q, j), cr in zip(pairs, cross):
            top = jnp.concatenate([blocks[q][j], jnp.zeros((size, size), F32)], axis=1)
            bot = jnp.concatenate([cr, blocks[q][j + 1]], axis=1)
            merged[q].append(jnp.concatenate([top, bot], axis=0))
        blocks = merged
        size *= 2
    return [b[0] for b in blocks]


_rw_score = _bdot_nt
_rw_score_out = _bdot_nt
_rw_inv = _bdot
_rw_solve = _bdot
_rw_read = _bdot_nt
_rw_read_out = _bdot_nt
_rw_apply = _bdot
_rw_apply_out = _bdot
_rw_update = _bdot_tn
_rw_state = _bdot

_RWKV_PARAMS = ("mu", "w0", "w2", "a0", "a2", "k_k", "k_a", "r_k", "gn_w", "gn_b")


def _rwkv_seq_kernel(pb_ref, shift0_ref, s0_ref, *rest, tb, chunk, attn_rows):
    np_ = len(_RWKV_PARAMS)
    p = {n: r[...] for n, r in zip(_RWKV_PARAMS, rest[:np_])}
    n_ai, n_ao = 8, 3
    attn_in = rest[np_:np_ + n_ai]
    o_ref, sfin_ref, shift_ref = rest[np_ + n_ai:np_ + n_ai + 3]
    attn_out = rest[np_ + n_ai + 3:np_ + n_ai + 3 + n_ao]
    r_s, k_s, v_s, a_s, b_s, w_s, g_s, y_s, bon_s, st_s, carry_s = rest[np_ + n_ai + 3 + n_ao:]
    step = pl.program_id(1)
    _attn_step_kernel(*attn_in, *attn_out, bb=attn_rows)

    @pl.when(step == 0)
    def _():
        st_s[...] = s0_ref[0]
        carry_s[...] = shift0_ref[0]

    xs = pb_ref[0][:, 0:B_SHIFT_W]
    row = lax.broadcasted_iota(jnp.int32, (tb, 1), 0)
    prev = jnp.where(row == 0, carry_s[...], pltpu.roll(xs, 1, 0))
    carry_s[...] = xs[tb - 1:tb, :]
    r, logw, k, v, av, bv, bonus = _rwkv_prep(xs, prev, p)
    gcum = _chunk_cumsum(logw, chunk)
    bon_s[...] = bonus
    for h in range(NH):
        sl = slice(h * HD, (h + 1) * HD)
        r_s[h] = r[:, sl]
        k_s[h] = k[:, sl]
        v_s[h] = v[:, sl]
        a_s[h] = av[:, sl]
        b_s[h] = bv[:, sl]
        w_s[h] = logw[:, sl]
        g_s[h] = gcum[:, sl]

    ri = lax.broadcasted_iota(jnp.int32, (chunk, chunk), 0)
    ci = lax.broadcasted_iota(jnp.int32, (chunk, chunk), 1)
    strict = ci < ri
    incl = ci <= ri

    def run_chunks(chunks):
        units = [(c, h) for c in chunks for h in range(NH)]
        hs = range(len(units))
        ld = lambda ref: [ref[h, c * chunk:(c + 1) * chunk, :] for c, h in units]
        rc, kc, vc, ac, bc, wc, gc = (ld(x) for x in (r_s, k_s, v_s, a_s, b_s, w_s, g_s))
        glast = [g[chunk - 1:chunk, :] for g in gc]
        einv = [jnp.exp(-g) for g in gc]
        etail = [jnp.exp(gl - g) for gl, g in zip(glast, gc)]
        a_t = [a * jnp.exp(g - w) for a, g, w in zip(ac, gc, wc)]
        r_t = [r * jnp.exp(g) for r, g in zip(rc, gc)]
        b_t = [b * e for b, e in zip(bc, einv)]
        k_t = [k * e for k, e in zip(kc, einv)]
        b_h = [b * e for b, e in zip(bc, etail)]
        k_h = [k * e for k, e in zip(kc, etail)]
        m_ab = [jnp.where(strict, _rw_score(a_t[h], b_t[h]), 0.0) for h in hs]
        m_ab_t = [jnp.where(ri < ci, _rw_score(b_t[h], a_t[h]), 0.0) for h in hs]
        m_ak = [jnp.where(strict, _rw_score(a_t[h], k_t[h]), 0.0) for h in hs]
        n_rb = [jnp.where(incl, _rw_score_out(r_t[h], b_t[h]), 0.0) for h in hs]
        n_rk = [jnp.where(incl, _rw_score_out(r_t[h], k_t[h]), 0.0) for h in hs]
        inv = _unit_lower_inverse(m_ab, m_ab_t, chunk)
        x = [_rw_apply(m_ak[h], vc[h]) for h in hs]
        a_hat = [_rw_solve(inv[h], a_t[h]) for h in hs]
        u = [_rw_solve(inv[h], x[h]) for h in hs]
        r_hat = [r_t[h] + _rw_apply_out(n_rb[h], a_hat[h]) for h in hs]
        y0 = [_rw_apply_out(n_rb[h], u[h]) + _rw_apply_out(n_rk[h], vc[h]) for h in hs]
        w_kk = [_rw_update(a_hat[h], b_h[h]) for h in hs]
        z = [_rw_update(u[h], b_h[h]) + _rw_update(vc[h], k_h[h]) for h in hs]
        decay = [jnp.exp(gl) for gl in glast]
        s = [st_s[h] for h in range(NH)]
        for i, (c, h) in enumerate(units):
            y_s[c * chunk:(c + 1) * chunk, h * HD:(h + 1) * HD] = _rw_read_out(r_hat[i], s[h]) + y0[i]
            s[h] = s[h] * decay[i] + _rw_state(s[h], w_kk[i]) + z[i]
        for h in range(NH):
            st_s[h] = s[h]

    n_chunks = tb // chunk
    for c0 in range(0, n_chunks, CHUNKS_PER_PASS):
        run_chunks(range(c0, min(c0 + CHUNKS_PER_PASS, n_chunks)))

    o_ref[0] = _rwkv_post(y_s[...], bon_s[...], pb_ref[0][:, B_SHIFT_W:B_W], p)

    @pl.when(step == pl.num_programs(1) - 1)
    def _():
        sfin_ref[0] = st_s[...]
        shift_ref[0] = carry_s[...]


def _rwkv_param_specs(params, nargs):
    zero2 = (lambda b, i: (0, 0)) if nargs == 2 else (lambda i: (0, 0))
    return [pl.BlockSpec(x.shape, zero2) for x in params]


def _rwkv_seq(pb, shift0, s0, params, attn_args):
    B, T, _ = pb.shape
    tb = min(MIX_TILE, T)
    nt = T // tb
    rows, a_in, a_in_specs, a_out_specs, a_out_shapes = _attn_step_operands(
        *attn_args, n_steps=B * nt, step_of=lambda b, i: b * nt + i)
    st = pl.BlockSpec((1, NH, HD, HD), lambda b, i: (b, 0, 0, 0))
    sh = pl.BlockSpec((1, 1, B_SHIFT_W), lambda b, i: (b, 0, 0))
    res = pl.pallas_call(
        functools.partial(_rwkv_seq_kernel, tb=tb, chunk=CHUNK_B, attn_rows=rows),
        grid=(B, nt),
        in_specs=[pl.BlockSpec((1, tb, B_W), lambda b, i: (b, i, 0)), sh, st]
                 + _rwkv_param_specs(params, 2) + a_in_specs,
        out_specs=[pl.BlockSpec((1, tb, D_BR), lambda b, i: (b, i, 0)), st, sh] + a_out_specs,
        out_shape=[jax.ShapeDtypeStruct((B, T, D_BR), F32),
                   jax.ShapeDtypeStruct((B, NH, HD, HD), F32),
                   jax.ShapeDtypeStruct((B, 1, B_SHIFT_W), F32)] + a_out_shapes,
        scratch_shapes=[pltpu.VMEM((NH, tb, HD), F32)] * 7
                       + [pltpu.VMEM((tb, D_BR), F32), pltpu.VMEM((tb, D_BR), F32),
                          pltpu.VMEM((NH, HD, HD), F32), pltpu.VMEM((1, B_SHIFT_W), F32)],
        compiler_params=_cparams("parallel", "arbitrary"),
        name="rwkv7_seq",
    )(pb, shift0, s0, *params, *a_in)
    return res[0], res[1], res[2], [a.reshape(-1, D_BR) for a in res[3:]]


def _rwkv_step_kernel(r_ref, k_ref, v_ref, lr_ref, g_ref, pr_ref, pk_ref, pv_ref, plr_ref,
                      mr_ref, mk_ref, mv_ref, mlr_ref, w0_ref, w2_ref, a0_ref, a2_ref,
                      kk_ref, ka_ref, rk_ref, gw_ref, gb_ref, s_ref, o_ref, snew_ref, y_s):
    lerp = lambda x, prev, mu: x[...] + (prev[...] - x[...]) * mu[...]
    r = lerp(r_ref, pr_ref, mr_ref)
    k = lerp(k_ref, pk_ref, mk_ref)
    v = lerp(v_ref, pv_ref, mv_ref)
    lr = lerp(lr_ref, plr_ref, mlr_ref)
    w = -_softplus(-(w0_ref[...] + _bdot(w2_ref[...], jnp.tanh(lr[0:LORA])))) - 0.5
    dec = jnp.exp(-jnp.exp(w))
    a = _sigmoid(a0_ref[...] + _bdot(a2_ref[...], lr[LORA:2 * LORA]))
    kk = k * kk_ref[...]
    kk = kk / jnp.maximum(jnp.sqrt(jnp.sum(kk * kk, axis=0, keepdims=True)), 1e-12)
    k = k * (1.0 + (a - 1.0) * ka_ref[...])
    bonus = jnp.sum(r * k * rk_ref[...], axis=0, keepdims=True) * v
    av = -kk
    bv = kk * a
    for i in range(HD):
        s = s_ref[i]
        sa = jnp.sum(s * av, axis=0, keepdims=True)
        s_new = s * dec + sa * bv + v[i:i + 1, :] * k
        snew_ref[i] = s_new
        y_s[i:i + 1, :] = jnp.sum(s_new * r, axis=0, keepdims=True)
    y = y_s[...]
    mean = jnp.mean(y, axis=0, keepdims=True)
    yc = y - mean
    var = jnp.mean(yc * yc, axis=0, keepdims=True)
    yn = yc * lax.rsqrt(var + RWKV_GN_EPS) * gw_ref[...] + gb_ref[...]
    o_ref[...] = (yn + bonus) * _silu(g_ref[...])


def _rwkv_step(pb_t, shift_t, state_t, layer, params):
    B = pb_t.shape[1]
    p = dict(zip(_RWKV_PARAMS, params))
    parts = pb_t.reshape(B_W // HD, HD, B)
    prev = shift_t.reshape(DEPTH, B_SHIFT_W // HD, HD, B)
    mu = p["mu"].reshape(B_SHIFT_W // HD, HD, 1)
    lr_row = 3 * NH
    part = lambda j: pl.BlockSpec((None, HD, B), lambda h: (j * NH + h, 0, 0))
    pprev = lambda j: pl.BlockSpec((None, None, HD, B), lambda h: (layer, j * NH + h, 0, 0))
    pmu = lambda j: pl.BlockSpec((None, HD, 1), lambda h: (j * NH + h, 0, 0))
    col = pl.BlockSpec((None, HD, 1), lambda h: (h, 0, 0))
    lora = pl.BlockSpec((HD, LORA), lambda h: (h, 0))
    cols = [_col_params(p[n]) for n in ("k_k", "k_a", "r_k", "gn_w", "gn_b")]
    return pl.pallas_call(
        _rwkv_step_kernel,
        grid=(NH,),
        in_specs=[part(0), part(1), part(2),
                  pl.BlockSpec((None, HD, B), lambda h: (lr_row, 0, 0)),
                  pl.BlockSpec((None, HD, B), lambda h: (lr_row + 1 + h, 0, 0)),
                  pprev(0), pprev(1), pprev(2),
                  pl.BlockSpec((None, None, HD, B), lambda h: (layer, lr_row, 0, 0)),
                  pmu(0), pmu(1), pmu(2),
                  pl.BlockSpec((None, HD, 1), lambda h: (lr_row, 0, 0)),
                  col, lora, col, lora, col, col, col, col, col,
                  pl.BlockSpec((None, None, HD, HD, B), lambda h: (layer, h, 0, 0, 0))],
        out_specs=[pl.BlockSpec((HD, B), lambda h: (h, 0)),
                   pl.BlockSpec((None, HD, HD, B), lambda h: (h, 0, 0, 0))],
        out_shape=[jax.ShapeDtypeStruct((D_BR, B), F32),
                   jax.ShapeDtypeStruct((NH, HD, HD, B), F32)],
        scratch_shapes=[pltpu.VMEM((HD, B), F32)],
        compiler_params=_cparams("parallel"),
        name="rwkv7_step",
    )(parts, parts, parts, parts, parts, prev, prev, prev, prev, mu, mu, mu, mu,
      _col_params(p["w0"]), p["w2"].T, _col_params(p["a0"]), p["a2"].T, *cols, state_t)


def _qkv_prep_kernel(pc_ref, cos_ref, sin_ref, qg_ref, kg_ref, *rest, tm):
    nd = len(C_CONFIGS)
    outs = rest[:3 * nd]
    kc_ref = rest[3 * nd]
    scr = rest[3 * nd + 1:]
    pc = pc_ref[0]
    cos = cos_ref[...]
    sin = sin_ref[...]
    q = _rope(_head_rms(pc[:, 0:D_BR], qg_ref[...]), cos, sin) * (HD ** -0.5)
    k = _rope(_head_rms(pc[:, D_BR:2 * D_BR], kg_ref[...]), cos, sin)
    kc_ref[0] = k
    for a, x in enumerate((q, k, pc[:, 2 * D_BR:3 * D_BR])):
        for half in range(D_BR // LANES):
            scr[a][half] = x[:, half * LANES:(half + 1) * LANES]
    hpl = LANES // HD
    for ci, (_, dil) in enumerate(C_CONFIGS):
        for a in range(3):
            out = outs[3 * ci + a]
            for rho in range(dil):
                for half in range(D_BR // LANES):
                    rows = scr[a][half, pl.ds(rho, tm // dil, stride=dil), :]
                    for j in range(hpl):
                        out[0, rho, half * hpl + j] = rows[:, j * HD:(j + 1) * HD].astype(BF16)


def _qkv_prep(pc, cos, sin, qg, kg):
    B, T, _ = pc.shape
    tm = min(ROW_TILE, T)
    vec = pl.BlockSpec((1, D_BR), lambda b, i: (0, 0))
    tab = pl.BlockSpec((tm, D_BR), lambda b, i: (i, 0))
    out_specs, out_shape = [], []
    for _, dil in C_CONFIGS:
        for _ in range(3):
            out_specs.append(pl.BlockSpec((1, dil, NH, tm // dil, HD), lambda b, i: (b, 0, 0, i, 0)))
            out_shape.append(jax.ShapeDtypeStruct((B, dil, NH, T // dil, HD), BF16))
    out_specs.append(pl.BlockSpec((1, tm, D_BR), lambda b, i: (b, i, 0)))
    out_shape.append(jax.ShapeDtypeStruct((B, T, D_BR), F32))
    return pl.pallas_call(
        functools.partial(_qkv_prep_kernel, tm=tm),
        grid=(B, T // tm),
        in_specs=[pl.BlockSpec((1, tm, C_W), lambda b, i: (b, i, 0)), tab, tab, vec, vec],
        out_specs=out_specs,
        out_shape=out_shape,
        scratch_shapes=[pltpu.VMEM((D_BR // LANES, tm, LANES), F32)] * 3,
        compiler_params=_cparams("parallel", "parallel"),
        name="attn_qkv_prep",
    )(pc, cos, sin, qg, kg)


def _swa_kernel(q_ref, kc_ref, vc_ref, kp_ref, vp_ref, o_ref, lse_ref, *, qt):
    i = pl.program_id(1)
    n = Q_BLOCK
    r = lax.broadcasted_iota(jnp.int32, (n, 2 * n), 0)
    c = lax.broadcasted_iota(jnp.int32, (n, 2 * n), 1)
    band = (c >= r) & (c <= r + n)
    first = band & (c >= jnp.where(i > 0, 0, n))
    units = [(u, h) for u in range(qt // n) for h in range(NH)]

    def window(cur_ref, prev_ref, u, h):
        if u == 0:
            return jnp.concatenate([prev_ref[0, h], cur_ref[0, h, 0:n, :]], axis=0)
        return cur_ref[0, h, (u - 1) * n:(u + 1) * n, :]

    s = [lax.dot_general(q_ref[0, h, u * n:(u + 1) * n, :], window(kc_ref, kp_ref, u, h),
                         (((1,), (1,)), ((), ())), preferred_element_type=F32) for u, h in units]
    s = [jnp.where(first if u == 0 else band, x, MASK_VALUE) for (u, h), x in zip(units, s)]
    m = [jnp.max(x, axis=1, keepdims=True) for x in s]
    p = [jnp.exp(x - mx) for x, mx in zip(s, m)]
    l = [jnp.sum(x, axis=1, keepdims=True) for x in p]
    o = [jnp.dot(x.astype(BF16), window(vc_ref, vp_ref, u, h), preferred_element_type=F32) / lx
         for (u, h), x, lx in zip(units, p, l)]
    lse = [jnp.broadcast_to(mx + jnp.log(lx), (n, HD)) for mx, lx in zip(m, l)]
    for u in range(qt // n):
        o_ref[0, u * n:(u + 1) * n, :] = jnp.concatenate(o[u * NH:(u + 1) * NH], axis=1)
        lse_ref[0, u * n:(u + 1) * n, :] = jnp.concatenate(lse[u * NH:(u + 1) * NH], axis=1)


def _swa(q, k, v):
    N, _, L, _ = q.shape
    n = Q_BLOCK
    qt = min(ATTN_TILE, L)
    cur = pl.BlockSpec((1, NH, qt, HD), lambda s, i: (s, 0, i, 0))
    prev = pl.BlockSpec((1, NH, n, HD), lambda s, i: (s, 0, jnp.maximum(i * (qt // n) - 1, 0), 0))
    out = pl.BlockSpec((1, qt, D_BR), lambda s, i: (s, i, 0))
    return pl.pallas_call(
        functools.partial(_swa_kernel, qt=qt),
        grid=(N, L // qt),
        in_specs=[cur, cur, cur, prev, prev],
        out_specs=[out, out],
        out_shape=[jax.ShapeDtypeStruct((N, L, D_BR), F32)] * 2,
        compiler_params=_cparams("parallel", "parallel"),
        name="sliding_window_attn",
    )(q, k, v, k, v)


def _attn_combine_kernel(*refs, tm):
    nd = len(C_CONFIGS)
    o_refs, lse_refs = refs[0:nd], refs[nd:2 * nd]
    g_ref, out_ref = refs[2 * nd], refs[2 * nd + 1]
    o_s, lse_s = refs[2 * nd + 2], refs[2 * nd + 3]
    os_, lses = [], []
    for ci, (_, dil) in enumerate(C_CONFIGS):
        if dil == 1:
            os_.append(o_refs[ci][0, 0])
            lses.append(lse_refs[ci][0, 0])
            continue
        nh = D_BR // LANES
        for rho in range(dil):
            for half in range(nh):
                ls = slice(half * LANES, (half + 1) * LANES)
                o_s[ci, half, pl.ds(rho, tm // dil, stride=dil), :] = o_refs[ci][0, rho][:, ls]
                lse_s[ci, half, pl.ds(rho, tm // dil, stride=dil), :] = lse_refs[ci][0, rho][:, ls]
        os_.append(jnp.concatenate([o_s[ci, half] for half in range(nh)], axis=1))
        lses.append(jnp.concatenate([lse_s[ci, half] for half in range(nh)], axis=1))
    top = functools.reduce(jnp.maximum, lses)
    ws = [jnp.exp(x - top) for x in lses]
    num = functools.reduce(lambda a, b: a + b, [w * o for w, o in zip(ws, os_)])
    den = functools.reduce(lambda a, b: a + b, ws)
    out_ref[0] = num / den * _silu(g_ref[0])


def _attn_combine(outs, lses, pc):
    B, T, _ = pc.shape
    tm = min(ROW_TILE, T)
    nd = len(C_CONFIGS)
    specs = [pl.BlockSpec((1, dil, tm // dil, D_BR), lambda b, i: (b, 0, i, 0)) for _, dil in C_CONFIGS]
    return pl.pallas_call(
        functools.partial(_attn_combine_kernel, tm=tm),
        grid=(B, T // tm),
        in_specs=specs + specs + [pl.BlockSpec((1, tm, D_BR), lambda b, i: (b, i, 3))],
        out_specs=pl.BlockSpec((1, tm, D_BR), lambda b, i: (b, i, 0)),
        out_shape=jax.ShapeDtypeStruct((B, T, D_BR), F32),
        scratch_shapes=[pltpu.VMEM((nd, D_BR // LANES, tm, LANES), F32)] * 2,
        compiler_params=_cparams("parallel", "parallel"),
        name="attn_combine",
    )(*outs, *lses, pc)


def _attn_seq(pc, cos, sin, qg, kg):
    B, T, _ = pc.shape
    assert T % (Q_BLOCK * max(d for _, d in C_CONFIGS)) == 0
    res = _qkv_prep(pc, cos, sin, qg, kg)
    outs, lses = [], []
    for ci, (w, dil) in enumerate(C_CONFIGS):
        assert w == Q_BLOCK * dil
        q, k, v = (x.reshape(B * dil, NH, T // dil, HD) for x in res[3 * ci:3 * ci + 3])
        o, lse = _swa(q, k, v)
        outs.append(o.reshape(B, dil, T // dil, D_BR))
        lses.append(lse.reshape(B, dil, T // dil, D_BR))
    return _attn_combine(outs, lses, pc), res[-1]


def _attn_step_kernel(pc_ref, cos_ref, sin_ref, qg_ref, kg_ref, mult_ref, kt_ref, vt_ref,
                      o_ref, knew_ref, vnew_ref, *, bb):
    pc = pc_ref[...]
    cos = cos_ref[...]
    sin = sin_ref[...]
    q = _rope(_head_rms(pc[:, 0:D_BR], qg_ref[...]), cos, sin) * (HD ** -0.5)
    k = _rope(_head_rms(pc[:, D_BR:2 * D_BR], kg_ref[...]), cos, sin)
    v = pc[:, 2 * D_BR:3 * D_BR]
    knew_ref[...] = k
    vnew_ref[...] = v
    mult = mult_ref[...]
    live = mult > 0.0
    n_self = float(len(C_CONFIGS))
    eye = _eye(HD)
    for b in range(bb):
        parts = []
        for h in range(NH):
            sl = slice(h * HD, (h + 1) * HD)
            qr, kr, vr = q[b:b + 1, sl], k[b:b + 1, sl], v[b:b + 1, sl]
            s_self = jnp.sum(qr * kr, axis=1, keepdims=True)
            s = jnp.sum(kt_ref[b, h] * _to_col(qr, eye), axis=0, keepdims=True)
            s = jnp.where(live, s, MASK_VALUE)
            m = jnp.maximum(jnp.max(s, axis=1, keepdims=True), s_self)
            p = jnp.exp(s - m) * mult
            p_self = n_self * jnp.exp(s_self - m)
            den = jnp.sum(p, axis=1, keepdims=True) + p_self
            acc = _to_row(jnp.sum(vt_ref[b, h] * p, axis=1, keepdims=True), eye) + p_self * vr
            parts.append(acc / den)
        o_ref[b:b + 1, :] = jnp.concatenate(parts, axis=1)
    o_ref[...] = o_ref[...] * _silu(pc[:, 3 * D_BR:4 * D_BR])


def _attn_step_operands(pc, cos, sin, qg, kg, cache_kt, cache_vt, layer, n_steps, step_of):
    B = pc.shape[0]
    P = cache_kt.shape[-1]
    bb = B // n_steps
    assert bb * n_steps == B
    dist = P - jnp.arange(P)
    mult = jnp.zeros((P,), F32)
    for w, dil in C_CONFIGS:
        mult = mult + ((dist <= w) & (dist % dil == 0)).astype(F32)
    fixed = lambda shape: pl.BlockSpec(shape, lambda *g: (0,) * len(shape))
    row = pl.BlockSpec((None, bb, D_BR), lambda *g: (step_of(*g), 0, 0))
    cache = pl.BlockSpec((None, bb, NH, HD, P), lambda *g: (layer, step_of(*g), 0, 0, 0))
    inputs = (pc.reshape(n_steps, bb, C_W), cos, sin, qg, kg, mult.reshape(1, P), cache_kt, cache_vt)
    in_specs = [pl.BlockSpec((None, bb, C_W), lambda *g: (step_of(*g), 0, 0)),
                fixed((1, D_BR)), fixed((1, D_BR)), fixed((1, D_BR)), fixed((1, D_BR)), fixed((1, P)),
                cache, cache]
    out_shapes = [jax.ShapeDtypeStruct((n_steps, bb, D_BR), F32)] * 3
    return bb, inputs, in_specs, [row, row, row], out_shapes


def _pool_mix(pooled, u, g, w_ref, scale_ref):
    mixed = _bdot(pooled - u, w_ref[...]) * scale_ref[...]
    return mixed * _silu(g)


def _pool_select(sums, cnts):
    lane = lax.broadcasted_iota(jnp.int32, sums[0].shape, 1) // (D_BR // len(POOL_WINDOWS))
    out = sums[-1] / cnts[-1]
    for gi in range(len(POOL_WINDOWS) - 2, -1, -1):
        out = jnp.where(lane == gi, sums[gi] / cnts[gi], out)
    return out


def _pool_seq_kernel(pd_ref, buf0_ref, w_ref, scale_ref, o_ref, carry_s, *, tb, t0):
    step = pl.program_id(1)

    @pl.when(step == 0)
    def _():
        carry_s[...] = buf0_ref[0]

    pd = pd_ref[0]
    u = pd[:, 0:D_BR]
    ext = jnp.concatenate([carry_s[...], u], axis=0)
    carry_s[...] = u[tb - POOL_MAX:tb, :]
    pos = (t0 + step * tb + lax.broadcasted_iota(jnp.int32, (tb, 1), 0)).astype(F32)
    sums, cnts = [], []
    run = ext
    shift = 1
    for w in POOL_WINDOWS:
        while shift < w:
            run = run + pltpu.roll(run, shift, 0)
            shift *= 2
        sums.append(run[POOL_MAX:, :])
        cnts.append(jnp.minimum(float(w), pos + 1.0))
    o_ref[0] = _pool_mix(_pool_select(sums, cnts), u, pd[:, D_BR:2 * D_BR], w_ref, scale_ref)


def _pool_seq(pd, buf0, w_bd, scale, t0):
    B, T, _ = pd.shape
    tb = min(ROW_TILE, T)
    return pl.pallas_call(
        functools.partial(_pool_seq_kernel, tb=tb, t0=t0),
        grid=(B, T // tb),
        in_specs=[pl.BlockSpec((1, tb, D_W), lambda b, i: (b, i, 0)),
                  pl.BlockSpec((1, POOL_MAX, D_BR), lambda b, i: (b, 0, 0)),
                  pl.BlockSpec((D_BR, D_BR), lambda b, i: (0, 0)),
                  pl.BlockSpec((1, D_BR), lambda b, i: (0, 0))],
        out_specs=pl.BlockSpec((1, tb, D_BR), lambda b, i: (b, i, 0)),
        out_shape=jax.ShapeDtypeStruct((B, T, D_BR), F32),
        scratch_shapes=[pltpu.VMEM((POOL_MAX, D_BR), F32)],
        compiler_params=_cparams("parallel", "arbitrary"),
        name="pool_seq",
    )(pd, buf0, w_bd, scale)


def _pool_step_kernel(pd_ref, buf_ref, w_ref, scale_ref, o_ref, *, t0):
    pd = pd_ref[...]
    u = pd[:, 0:D_BR]
    nbuf = POOL_MAX - 1
    sums, cnts = [], []
    run = u
    taken = 1
    for w in POOL_WINDOWS:
        while taken < w:
            run = run + buf_ref[nbuf - taken]
            taken += 1
        sums.append(run)
        cnts.append(float(min(w, t0 + 1)))
    o_ref[...] = _pool_mix(_pool_select(sums, cnts), u, pd[:, D_BR:2 * D_BR], w_ref, scale_ref)


def _pool_step(pd, buf, layer, w_bd, scale, t0):
    B = pd.shape[0]
    bb = min(POOL_STEP_ROWS, B)
    return pl.pallas_call(
        functools.partial(_pool_step_kernel, t0=t0),
        grid=(B // bb,),
        in_specs=[pl.BlockSpec((bb, D_W), lambda i: (i, 0)),
                  pl.BlockSpec((None, POOL_MAX - 1, bb, D_BR), lambda i: (layer, 0, i, 0)),
                  pl.BlockSpec((D_BR, D_BR), lambda i: (0, 0)),
                  pl.BlockSpec((1, D_BR), lambda i: (0, 0))],
        out_specs=pl.BlockSpec((bb, D_BR), lambda i: (i, 0)),
        out_shape=jax.ShapeDtypeStruct((B, D_BR), F32),
        compiler_params=_cparams("parallel"),
        name="pool_step",
    )(pd, buf, w_bd, scale)


def _rope_tables(pos):
    half = HD // 2
    inv_freq = jnp.float32(ROPE_THETA) ** (-jnp.arange(half, dtype=F32) / half)
    ang = pos.astype(F32)[:, None] * inv_freq[None, :]
    cos = jnp.cos(ang)
    sin = jnp.sin(ang)
    cos_t = jnp.tile(jnp.concatenate([cos, cos], axis=-1), (1, NH))
    sin_t = jnp.tile(jnp.concatenate([-sin, sin], axis=-1), (1, NH))
    return cos_t, sin_t


def _block_diag(w):
    g, c, d = w.shape
    out = jnp.zeros((g * c, g * d), w.dtype)
    for i in range(g):
        out = out.at[i * c:(i + 1) * c, i * d:(i + 1) * d].set(w[i])
    return out


def _layer_params(l, W):
    row = lambda x: x[l].reshape(1, -1)
    w_in = W["w_in"][l].astype(BF16)
    w_in_t = jnp.swapaxes(W["w_in"], 1, 2)[l].astype(BF16)
    cuts = (0, A_W, A_W + B_W, A_W + B_W + C_W, A_W + B_W + C_W + D_W)
    return dict(
        norm_g=W["norm_g"][l],
        w_parts=[w_in[:, cuts[i]:cuts[i + 1]] for i in range(4)],
        w_parts_t=[w_in_t[cuts[i]:cuts[i + 1], :] for i in range(2)],
        w_out=W["w_out"][l].astype(BF16),
        onorm_g=row(W["a_onorm_g"]),
        rwkv=[row(W["b_mu"]), row(W["b_w0"]), W["b_w2"][l], row(W["b_a0"]), W["b_a2"][l],
              row(W["b_k_k"]), row(W["b_k_a"]), row(W["b_r_k"]), row(W["b_gn_w"]), row(W["b_gn_b"])],
        qg=jnp.tile(W["c_qnorm_g"][l], NH).reshape(1, D_BR),
        kg=jnp.tile(W["c_knorm_g"][l], NH).reshape(1, D_BR),
        w_pool=_block_diag(W["d_w_pool"][l]),
        p_scale=row(W["d_scale"]),
    )


def _run_groups(x, xs, mods, mods_s, layers, log_lb, log1m_lb, sA, sBw, sBs, ck, cv, cd):
    B, T, _ = x.shape
    Bs = xs.shape[0]
    assert xs.shape[1] == 1 and ck.shape[2] == PAST_LEN == C_WIN_MAX
    cos, sin = _rope_tables(jnp.arange(T))
    cos_s, sin_s = _rope_tables(jnp.full((1,), PAST_LEN))
    c_keep = min(C_WIN_MAX, T)
    d_keep = min(POOL_MAX - 1, T)
    zero_state = jnp.zeros((B, NH, HD, HD), F32)
    zero_shift = jnp.zeros((B, 1, B_SHIFT_W), F32)
    zero_pool = jnp.zeros((B, POOL_MAX, D_BR), F32)
    ck_t = jnp.transpose(ck, (0, 1, 3, 4, 2))
    cv_t = jnp.transpose(cv, (0, 1, 3, 4, 2))
    sA_t = jnp.transpose(sA, (0, 2, 3, 4, 1))
    sBw_t = jnp.transpose(sBw, (0, 2, 3, 4, 1))
    sBs_t = jnp.transpose(sBs, (0, 2, 1))
    cd_t = jnp.transpose(cd, (0, 2, 1, 3))
    xr = xs.reshape(Bs, D_MODEL)
    pA, pBw, pBs, pK, pV, pD = [], [], [], [], [], []
    nA, nBw, nBs, nK, nV, nD = [], [], [], [], [], []
    for l, L in enumerate(layers):
        shift, scale, gate_s = (mods_s[l][:, i * D_MODEL:(i + 1) * D_MODEL] for i in range(3))
        pa_t, pb_t, pc_s, pd_s = _in_proj_step(xr, shift, scale, L["norm_g"], *L["w_parts_t"], *L["w_parts"][2:])
        shift, scale, gate = (mods[l][:, None, i * D_MODEL:(i + 1) * D_MODEL] for i in range(3))
        pa, pb, pc, pd = _in_proj(x, shift, scale, L["norm_g"], L["w_parts"])
        oa, sa_t = _hgrn2_seq(pa, zero_state, log_lb[l], log1m_lb[l], L["onorm_g"])
        ob, sb, sbs, (oc_s, k_new, v_new) = _rwkv_seq(
            pb, zero_shift, zero_state, L["rwkv"], (pc_s, cos_s, sin_s, L["qg"], L["kg"], ck_t, cv_t, l))
        oc, k_rot = _attn_seq(pc, cos, sin, L["qg"], L["kg"])
        od = _pool_seq(pd, zero_pool, L["w_pool"], L["p_scale"], 0)
        x = _out_proj(oa, ob, oc, od, x, gate, L["w_out"])
        pA.append(jnp.swapaxes(sa_t, -1, -2))
        pBw.append(sb)
        pBs.append(sbs[:, 0])
        pK.append(k_rot[:, T - c_keep:].reshape(B, c_keep, NH, HD))
        pV.append(pc[:, T - c_keep:, 2 * D_BR:3 * D_BR].reshape(B, c_keep, NH, HD))
        pD.append(pd[:, T - d_keep:, 0:D_BR])
        oa_t, sa = _hgrn2_step(pa_t, sA_t, l, log_lb[l], log1m_lb[l], L["onorm_g"])
        ob_t, sbw = _rwkv_step(pb_t, sBs_t, sBw_t, l, L["rwkv"])
        od_s = _pool_step(pd_s, cd_t, l, L["w_pool"], L["p_scale"], PAST_LEN)
        xr = _out_proj_step(oa_t, ob_t, oc_s, od_s, xr, gate_s, L["w_out"])
        nA.append(sa)
        nBw.append(sbw)
        nBs.append(pb_t[0:B_SHIFT_W])
        nK.append(k_new.reshape(Bs, 1, NH, HD))
        nV.append(v_new.reshape(Bs, 1, NH, HD))
        nD.append(pd_s[:, None, 0:D_BR])
    back = lambda a: jnp.transpose(jnp.stack(a), (0, 4, 1, 2, 3))
    prompt = (x,) + tuple(jnp.stack(a) for a in (pA, pBw, pBs, pK, pV, pD))
    sample = (xr.reshape(Bs, 1, D_MODEL), back(nA), back(nBw), jnp.transpose(jnp.stack(nBs), (0, 2, 1)),
              jnp.stack(nK), jnp.stack(nV), jnp.stack(nD))
    return prompt, sample


def kernel(x_prompt, x_sample, state_A, state_B_wkv, state_B_shift, cache_C_k, cache_C_v, cache_D_pool, c_prompt, c_sample, ada_w, ada_b, norm_g, w_in, w_out, a_lb_logits, a_onorm_g, b_mu, b_w0, b_w2, b_a0, b_a2, b_k_k, b_k_a, b_r_k, b_gn_w, b_gn_b, c_qnorm_g, c_knorm_g, d_w_pool, d_scale):
    W = dict(norm_g=norm_g, w_in=w_in, w_out=w_out, a_onorm_g=a_onorm_g, b_mu=b_mu, b_w0=b_w0,
             b_w2=b_w2, b_a0=b_a0, b_a2=b_a2, b_k_k=b_k_k, b_k_a=b_k_a, b_r_k=b_r_k,
             b_gn_w=b_gn_w, b_gn_b=b_gn_b, c_qnorm_g=c_qnorm_g, c_knorm_g=c_knorm_g,
             d_w_pool=d_w_pool, d_scale=d_scale)
    layers = [_layer_params(l, W) for l in range(DEPTH)]
    lb_sm = jax.nn.softmax(a_lb_logits.astype(F32), axis=0)
    lower = jnp.cumsum(lb_sm, axis=0) - lb_sm[0:1]
    log_lb = jnp.log(jnp.maximum(lower, LB_FLOOR))[:, None, :]
    log1m_lb = jnp.log1p(-lower)[:, None, :]

    nb = x_prompt.shape[0]
    mods = _modulation(jnp.concatenate([c_prompt, c_sample], axis=0), ada_w, ada_b)
    (y_p, pA, pBw, pBs, pK, pV, pD), (y_s, sA, sBw, sBs, sK, sV, sD) = _run_groups(
        x_prompt, x_sample, mods[:, :nb], mods[:, nb:], layers, log_lb, log1m_lb,
        state_A, state_B_wkv, state_B_shift, cache_C_k, cache_C_v, cache_D_pool)
    return (y_p, y_s, pA, sA, pBw, sBw, pBs, sBs, pK, sK, pV, sV, pD, sD)
```

```python
import functools

import jax
import jax.numpy as jnp
from jax import lax
from jax.experimental import pallas as pl
from jax.experimental.pallas import tpu as pltpu

F32 = jnp.float32
BF16 = jnp.bfloat16

D_MODEL = 1024
DEPTH = 4
PAST_LEN = 2048
D_BR = 256
HD = 64
NH = D_BR // HD
LB_FLOOR = 1e-30
LORA = 32
RWKV_GN_EPS = 64e-5
C_CONFIGS = ((128, 1), (512, 4), (2048, 16))
C_WIN_MAX = 2048
MASK_VALUE = -1e30
ROPE_THETA = 10000.0
POOL_WINDOWS = (2, 4, 8, 16)
POOL_MAX = 16
EPS = 1e-6
A_W = 4 * D_BR
B_SHIFT_W = 3 * D_BR + 2 * LORA
B_W = B_SHIFT_W + D_BR
C_W = 4 * D_BR
D_W = 2 * D_BR

LANES = 128

ROW_TILE = 512
MIX_TILE = 256
CHUNK_A = 8
CHUNKS_PER_IT_A = 8
CHUNK_B = 64
INV_BLOCK = 16
CHUNKS_PER_PASS = 4
Q_BLOCK = 128
ATTN_TILE = 512
POOL_STEP_ROWS = 32
VMEM_LIMIT = 56 * 1024 * 1024


def _cparams(*sem):
    return pltpu.CompilerParams(dimension_semantics=sem, vmem_limit_bytes=VMEM_LIMIT)


def _sigmoid(x):
    return 1.0 / (1.0 + jnp.exp(-x))


def _silu(x):
    return x * _sigmoid(x)


def _softplus(x):
    return jnp.maximum(x, 0.0) + jnp.log1p(jnp.exp(-jnp.abs(x)))


def _log_sigmoid(x):
    return -_softplus(-x)


def _logaddexp(a, b):
    return jnp.maximum(a, b) + jnp.log1p(jnp.exp(-jnp.abs(a - b)))


def _bdot(a, b):
    return jnp.dot(a.astype(BF16), b.astype(BF16), preferred_element_type=F32)


def _bdot_nt(a, b):
    return lax.dot_general(a.astype(BF16), b.astype(BF16), (((1,), (1,)), ((), ())),
                           preferred_element_type=F32)


def _bdot_tn(a, b):
    return lax.dot_general(a.astype(BF16), b.astype(BF16), (((0,), (0,)), ((), ())),
                           preferred_element_type=F32)


def _split3(x):
    h1 = x.astype(BF16)
    r1 = x - h1.astype(F32)
    h2 = r1.astype(BF16)
    r2 = r1 - h2.astype(F32)
    return h1, h2, r2.astype(BF16)


def _seg_sum(x):
    n = x.shape[-1]
    r = lax.broadcasted_iota(jnp.int32, (n, n), 0) // HD
    c = lax.broadcasted_iota(jnp.int32, (n, n), 1) // HD
    ones = jnp.where(r == c, 1.0, 0.0).astype(BF16)
    h1, h2, h3 = _split3(x)
    d = lambda h: jnp.dot(h, ones, preferred_element_type=F32)
    return d(h1) + d(h2) + d(h3)


def _chunk_cumsum(x, chunk):
    t = x.shape[0]
    r = lax.broadcasted_iota(jnp.int32, (t, t), 0)
    c = lax.broadcasted_iota(jnp.int32, (t, t), 1)
    tri = jnp.where((r // chunk == c // chunk) & (c <= r), 1.0, 0.0).astype(BF16)
    h1, h2, h3 = _split3(x)
    d = lambda h: jnp.dot(tri, h, preferred_element_type=F32)
    return d(h1) + d(h2) + d(h3)


def _head_rms(x, gamma):
    ms = _seg_sum(x * x) * (1.0 / HD)
    return x * lax.rsqrt(ms + EPS) * gamma


def _rope(x, cos, sin_signed):
    lane = lax.broadcasted_iota(jnp.int32, x.shape, 1) % HD
    partner = jnp.where(lane < HD // 2,
                        pltpu.roll(x, D_BR - HD // 2, 1),
                        pltpu.roll(x, HD // 2, 1))
    return x * cos + partner * sin_signed


def _eye(n):
    r = lax.broadcasted_iota(jnp.int32, (n, n), 0)
    c = lax.broadcasted_iota(jnp.int32, (n, n), 1)
    return jnp.where(r == c, 1.0, 0.0).astype(F32)


def _to_col(row, eye):
    return jnp.sum(eye * row, axis=1, keepdims=True)


def _to_row(col, eye):
    return jnp.sum(eye * col, axis=0, keepdims=True)


def _mod_kernel(c_ref, w_ref, b_ref, o_ref):
    c = _silu(c_ref[...])
    o_ref[0] = _bdot(c, w_ref[0]) + b_ref[0]


def _modulation(c_all, ada_w, ada_b):
    n = c_all.shape[0]
    tn = 512
    return pl.pallas_call(
        _mod_kernel,
        grid=(DEPTH, 3 * D_MODEL // tn),
        in_specs=[pl.BlockSpec((n, D_MODEL), lambda l, j: (0, 0)),
                  pl.BlockSpec((1, D_MODEL, tn), lambda l, j: (l, 0, j)),
                  pl.BlockSpec((1, 1, tn), lambda l, j: (l, 0, j))],
        out_specs=pl.BlockSpec((1, n, tn), lambda l, j: (l, 0, j)),
        out_shape=jax.ShapeDtypeStruct((DEPTH, n, 3 * D_MODEL), F32),
        compiler_params=_cparams("parallel", "parallel"),
        name="adaln_modulation",
    )(c_all, ada_w, ada_b.reshape(DEPTH, 1, 3 * D_MODEL))


def _in_proj_kernel(x_ref, shift_ref, scale_ref, g_ref, wa_ref, wb_ref, wc_ref, wd_ref,
                    cos_ref, sin_ref, qg_ref, kg_ref, pa_ref, pb_ref, pc_ref, pd_ref, *attn_prep, tm):
    x = x_ref[0]
    ms = jnp.mean(x * x, axis=-1, keepdims=True)
    h = x * lax.rsqrt(ms + EPS) * g_ref[...]
    h = (h * (1.0 + scale_ref[0]) + shift_ref[0]).astype(BF16)
    pc_ref[0] = jnp.dot(h, wc_ref[...], preferred_element_type=F32)
    pa_ref[0] = jnp.dot(h, wa_ref[...], preferred_element_type=F32)
    pb_ref[0] = jnp.dot(h, wb_ref[...], preferred_element_type=F32)
    pd_ref[0] = jnp.dot(h, wd_ref[...], preferred_element_type=F32)
    _qkv_prep_kernel(pc_ref, cos_ref, sin_ref, qg_ref, kg_ref, *attn_prep, tm=tm)


def _in_proj(x, shift, scale, norm_g, w_parts, cos, sin, qg, kg):
    B, T, D = x.shape
    tm = min(ROW_TILE, T)
    mod = pl.BlockSpec((1, 1, D), lambda b, i: (b, 0, 0))
    vec = pl.BlockSpec((1, D_BR), lambda b, i: (0, 0))
    tab = pl.BlockSpec((tm, D_BR), lambda b, i: (i, 0))
    widths = [w.shape[1] for w in w_parts]
    out_specs = [pl.BlockSpec((1, tm, w), lambda b, i: (b, i, 0)) for w in widths]
    out_shape = [jax.ShapeDtypeStruct((B, T, w), F32) for w in widths]
    for _, dil in C_CONFIGS:
        for _ in range(3):
            out_specs.append(pl.BlockSpec((1, dil, NH, tm // dil, HD), lambda b, i: (b, 0, 0, i, 0)))
            out_shape.append(jax.ShapeDtypeStruct((B, dil, NH, T // dil, HD), BF16))
    out_specs.append(pl.BlockSpec((1, tm, D_BR), lambda b, i: (b, i, 0)))
    out_shape.append(jax.ShapeDtypeStruct((B, T, D_BR), F32))
    res = pl.pallas_call(
        functools.partial(_in_proj_kernel, tm=tm),
        grid=(B, T // tm),
        in_specs=[pl.BlockSpec((1, tm, D), lambda b, i: (b, i, 0)), mod, mod,
                  pl.BlockSpec((1, D), lambda b, i: (0, 0))]
                 + [pl.BlockSpec((D, w), lambda b, i: (0, 0), pipeline_mode=pl.Buffered(1)) for w in widths]
                 + [tab, tab, vec, vec],
        out_specs=out_specs,
        out_shape=out_shape,
        scratch_shapes=[pltpu.VMEM((D_BR // LANES, tm, LANES), F32)] * 3,
        compiler_params=_cparams("parallel", "parallel"),
        name="norm_in_proj",
    )(x, shift, scale, norm_g.reshape(1, D), *w_parts, cos, sin, qg, kg)
    return res[:4], res[4:]


def _in_proj_step_kernel(x_ref, shift_ref, scale_ref, g_ref, wat_ref, wbt_ref, wc_ref, wd_ref,
                         pat_ref, pbt_ref, pc_ref, pd_ref):
    x = x_ref[...]
    ms = jnp.mean(x * x, axis=-1, keepdims=True)
    h = x * lax.rsqrt(ms + EPS) * g_ref[...]
    h = (h * (1.0 + scale_ref[...]) + shift_ref[...]).astype(BF16)
    nt = lambda w: lax.dot_general(w, h, (((1,), (1,)), ((), ())), preferred_element_type=F32)
    pat_ref[...] = nt(wat_ref[...])
    pbt_ref[...] = nt(wbt_ref[...])
    pc_ref[...] = jnp.dot(h, wc_ref[...], preferred_element_type=F32)
    pd_ref[...] = jnp.dot(h, wd_ref[...], preferred_element_type=F32)


def _in_proj_step(x, shift, scale, norm_g, wa_t, wb_t, wc, wd):
    B, D = x.shape
    ins = (x, shift, scale, norm_g.reshape(1, D), wa_t, wb_t, wc, wd)
    shapes = [(wa_t.shape[0], B), (wb_t.shape[0], B), (B, wc.shape[1]), (B, wd.shape[1])]
    return pl.pallas_call(
        _in_proj_step_kernel,
        grid=(1,),
        in_specs=[pl.BlockSpec(a.shape, lambda i: (0, 0)) for a in ins],
        out_specs=[pl.BlockSpec(s, lambda i: (0, 0)) for s in shapes],
        out_shape=[jax.ShapeDtypeStruct(s, F32) for s in shapes],
        compiler_params=_cparams("arbitrary"),
        name="norm_in_proj_step",
    )(*ins)


def _out_proj_step_kernel(oat_ref, obt_ref, oc_ref, od_ref, x_ref, gate_ref, w_ref, y_ref):
    tn = lambda a, w: lax.dot_general(a.astype(BF16), w, (((0,), (0,)), ((), ())), preferred_element_type=F32)
    nn = lambda a, w: jnp.dot(a.astype(BF16), w, preferred_element_type=F32)
    res = (tn(oat_ref[...], w_ref[0:D_BR, :]) + tn(obt_ref[...], w_ref[D_BR:2 * D_BR, :])
           + nn(oc_ref[...], w_ref[2 * D_BR:3 * D_BR, :]) + nn(od_ref[...], w_ref[3 * D_BR:4 * D_BR, :]))
    y_ref[...] = x_ref[...] + gate_ref[...] * res


def _out_proj_step(oa_t, ob_t, oc, od, x, gate, w_out):
    ins = (oa_t, ob_t, oc, od, x, gate, w_out)
    return pl.pallas_call(
        _out_proj_step_kernel,
        grid=(1,),
        in_specs=[pl.BlockSpec(a.shape, lambda i: (0, 0)) for a in ins],
        out_specs=pl.BlockSpec(x.shape, lambda i: (0, 0)),
        out_shape=jax.ShapeDtypeStruct(x.shape, F32),
        compiler_params=_cparams("arbitrary"),
        name="out_proj_residual_step",
    )(*ins)


def _out_proj_kernel(oa_ref, ob_ref, *rest, tm, t0):
    nd = len(C_CONFIGS)
    attn_in = rest[:2 * nd + 1]
    pd_ref, buf0_ref, wp_ref, ps_ref, x_ref, gate_ref, w_ref, y_ref = rest[2 * nd + 1:2 * nd + 9]
    oc_s, od_s, o_s, lse_s, carry_s = rest[2 * nd + 9:]
    _attn_combine_kernel(*attn_in, oc_s, o_s, lse_s, tm=tm)
    _pool_seq_kernel(pd_ref, buf0_ref, wp_ref, ps_ref, od_s, carry_s, tb=tm, t0=t0)
    mix = jnp.concatenate([oa_ref[0], ob_ref[0], oc_s[0], od_s[0]], axis=-1).astype(BF16)
    res = jnp.dot(mix, w_ref[...], preferred_element_type=F32)
    y_ref[0] = x_ref[0] + gate_ref[0] * res


def _out_proj(oa, ob, attn_outs, attn_lses, pc, pd, pool_buf0, w_pool, p_scale, t0, x, gate, w_out):
    B, T, D = x.shape
    tm = min(ROW_TILE, T)
    nd = len(C_CONFIGS)
    br = pl.BlockSpec((1, tm, D_BR), lambda b, i: (b, i, 0))
    parts = [pl.BlockSpec((1, dil, tm // dil, D_BR), lambda b, i: (b, 0, i, 0)) for _, dil in C_CONFIGS]
    return pl.pallas_call(
        functools.partial(_out_proj_kernel, tm=tm, t0=t0),
        grid=(B, T // tm),
        in_specs=[br, br] + parts + parts
                 + [pl.BlockSpec((1, tm, D_BR), lambda b, i: (b, i, 3)),
                    pl.BlockSpec((1, tm, D_W), lambda b, i: (b, i, 0)),
                    pl.BlockSpec((1, POOL_MAX, D_BR), lambda b, i: (b, 0, 0)),
                    pl.BlockSpec((D_BR, D_BR), lambda b, i: (0, 0)),
                    pl.BlockSpec((1, D_BR), lambda b, i: (0, 0)),
                    pl.BlockSpec((1, tm, D), lambda b, i: (b, i, 0)),
                    pl.BlockSpec((1, 1, D), lambda b, i: (b, 0, 0)),
                    pl.BlockSpec((D, D), lambda b, i: (0, 0))],
        out_specs=pl.BlockSpec((1, tm, D), lambda b, i: (b, i, 0)),
        out_shape=jax.ShapeDtypeStruct((B, T, D), F32),
        scratch_shapes=[pltpu.VMEM((1, tm, D_BR), F32)] * 2
                       + [pltpu.VMEM((nd, D_BR // LANES, tm, LANES), F32)] * 2
                       + [pltpu.VMEM((POOL_MAX, D_BR), F32)],
        compiler_params=_cparams("parallel", "arbitrary"),
        name="out_proj_residual",
    )(oa, ob, *attn_outs, *attn_lses, pc, pd, pool_buf0, w_pool, p_scale, x, gate, w_out)


def _hgrn2_gates(pa, log_lb, log1m_lb):
    q = _silu(pa[:, 0:D_BR])
    fl = pa[:, D_BR:2 * D_BR]
    v = pa[:, 2 * D_BR:3 * D_BR]
    ls_neg = _log_sigmoid(-fl)
    logf = _logaddexp(_log_sigmoid(fl), log_lb + ls_neg)
    k = jnp.exp(log1m_lb + ls_neg)
    return q, logf, k, v


def _hgrn2_seq_kernel(pa_ref, s0_ref, loglb_ref, log1mlb_ref, og_ref, o_ref, sfin_ref,
                      q_s, k_s, v_s, g_s, o_s, st_s, *, tb, chunk, per_it):
    step = pl.program_id(1)
    hpl = LANES // HD
    npair = NH // hpl
    rl = lax.broadcasted_iota(jnp.int32, (LANES, LANES), 0) // HD
    cl = lax.broadcasted_iota(jnp.int32, (LANES, LANES), 1) // HD
    same_head = rl == cl

    @pl.when(step == 0)
    def _():
        for p in range(npair):
            blocks = [jnp.concatenate([s0_ref[0, p * hpl + a] if a == b else jnp.zeros((HD, HD), F32)
                                       for b in range(hpl)], axis=1) for a in range(hpl)]
            st_s[p] = jnp.concatenate(blocks, axis=0)

    q, logf, k, v = _hgrn2_gates(pa_ref[0], loglb_ref[...], log1mlb_ref[...])
    gcum = _chunk_cumsum(logf, chunk)
    for p in range(npair):
        sl = slice(p * LANES, (p + 1) * LANES)
        q_s[p] = q[:, sl]
        k_s[p] = k[:, sl]
        v_s[p] = v[:, sl]
        g_s[p] = gcum[:, sl]

    rows = lax.broadcasted_iota(jnp.int32, (chunk, 1), 0)
    lane_head = lax.broadcasted_iota(jnp.int32, (chunk, LANES), 1) // HD

    def body(it, carry):
        units = [(j, p) for j in range(per_it) for p in range(npair)]
        starts = [pl.multiple_of((it * per_it + j) * chunk, chunk) for j in range(per_it)]
        ld = lambda ref: [ref[p, pl.ds(starts[j], chunk), :] for j, p in units]
        qc, kc, vc, gc = ld(q_s), ld(k_s), ld(v_s), ld(g_s)
        glast = [g[chunk - 1:chunk, :] for g in gc]
        upd = [jnp.where(same_head, _bdot_tn(v_, k_ * jnp.exp(gl - g)), 0.0)
               for v_, k_, gl, g in zip(vc, kc, glast, gc)]
        decay = [jnp.exp(gl) for gl in glast]
        qe = [q_ * jnp.exp(g) for q_, g in zip(qc, gc)]
        o_acc = [jnp.zeros((chunk, LANES), F32) for _ in units]
        for t in range(chunk):
            x = [jnp.exp(jnp.minimum(g[t:t + 1, :] - g, 0.0)) * k_ * q_[t:t + 1, :]
                 for g, k_, q_ in zip(gc, kc, qc)]
            sel = [(lane_head == a) & (rows <= t) for a in range(hpl)]
            att = []
            for xi in x:
                sums = [jnp.sum(jnp.where(m, xi, 0.0), axis=1, keepdims=True) for m in sel]
                a_ = sums[-1]
                for a in range(hpl - 2, -1, -1):
                    a_ = jnp.where(lane_head == a, sums[a], a_)
                att.append(a_)
            ot = [jnp.sum(a * v_, axis=0, keepdims=True) for a, v_ in zip(att, vc)]
            o_acc = [jnp.where(rows == t, r, o) for o, r in zip(o_acc, ot)]
        st = [st_s[p] for p in range(npair)]
        for i, (j, p) in enumerate(units):
            o_s[pl.ds(starts[j], chunk), p * LANES:(p + 1) * LANES] = o_acc[i] + _bdot_nt(qe[i], st[p])
            st[p] = st[p] * decay[i] + upd[i]
        for p in range(npair):
            st_s[p] = st[p]
        return carry

    lax.fori_loop(0, tb // (chunk * per_it), body, 0)

    o = _head_rms(o_s[...], og_ref[...])
    o_ref[0] = o * _silu(pa_ref[0][:, 3 * D_BR:4 * D_BR])

    @pl.when(step == pl.num_programs(1) - 1)
    def _():
        for h in range(NH):
            a = (h % hpl) * HD
            sfin_ref[0, h] = st_s[h // hpl][a:a + HD, a:a + HD]


def _hgrn2_seq(pa, s0_t, log_lb, log1m_lb, onorm_g):
    B, T, _ = pa.shape
    tb = min(MIX_TILE, T)
    vec = pl.BlockSpec((1, D_BR), lambda b, i: (0, 0))
    st = pl.BlockSpec((1, NH, HD, HD), lambda b, i: (b, 0, 0, 0))
    return pl.pallas_call(
        functools.partial(_hgrn2_seq_kernel, tb=tb, chunk=CHUNK_A, per_it=CHUNKS_PER_IT_A),
        grid=(B, T // tb),
        in_specs=[pl.BlockSpec((1, tb, A_W), lambda b, i: (b, i, 0)), st, vec, vec, vec],
        out_specs=[pl.BlockSpec((1, tb, D_BR), lambda b, i: (b, i, 0)), st],
        out_shape=[jax.ShapeDtypeStruct((B, T, D_BR), F32),
                   jax.ShapeDtypeStruct((B, NH, HD, HD), F32)],
        scratch_shapes=[pltpu.VMEM((D_BR // LANES, tb, LANES), F32)] * 4
                       + [pltpu.VMEM((tb, D_BR), F32), pltpu.VMEM((D_BR // LANES, LANES, LANES), F32)],
        compiler_params=_cparams("parallel", "arbitrary"),
        name="hgrn2_seq",
    )(pa, s0_t, log_lb, log1m_lb, onorm_g)


def _hgrn2_step_kernel(q_ref, f_ref, i_ref, g_ref, s_ref, loglb_ref, log1mlb_ref, og_ref, o_ref, snew_ref):
    q = _silu(q_ref[...])
    fl = f_ref[...]
    v = i_ref[...]
    ls_neg = _log_sigmoid(-fl)
    f = jnp.exp(_logaddexp(_log_sigmoid(fl), loglb_ref[...] + ls_neg))
    k = jnp.exp(log1mlb_ref[...] + ls_neg)
    o = jnp.zeros_like(v)
    for kk in range(HD):
        s_new = f[kk:kk + 1, :] * s_ref[kk] + k[kk:kk + 1, :] * v
        snew_ref[kk] = s_new
        o = o + q[kk:kk + 1, :] * s_new
    ms = jnp.mean(o * o, axis=0, keepdims=True)
    o_ref[...] = o * lax.rsqrt(ms + EPS) * og_ref[...] * _silu(g_ref[...])


def _col_params(x):
    return x.reshape(NH, HD, 1)


def _hgrn2_step(pa_t, state_t, layer, log_lb, log1m_lb, onorm_g):
    B = pa_t.shape[1]
    parts = pa_t.reshape(A_W // HD, HD, B)
    part = lambda j: pl.BlockSpec((None, HD, B), lambda h: (j * NH + h, 0, 0))
    col = pl.BlockSpec((None, HD, 1), lambda h: (h, 0, 0))
    return pl.pallas_call(
        _hgrn2_step_kernel,
        grid=(NH,),
        in_specs=[part(0), part(1), part(2), part(3),
                  pl.BlockSpec((None, None, HD, HD, B), lambda h: (layer, h, 0, 0, 0)), col, col, col],
        out_specs=[pl.BlockSpec((HD, B), lambda h: (h, 0)),
                   pl.BlockSpec((None, HD, HD, B), lambda h: (h, 0, 0, 0))],
        out_shape=[jax.ShapeDtypeStruct((D_BR, B), F32),
                   jax.ShapeDtypeStruct((NH, HD, HD, B), F32)],
        compiler_params=_cparams("parallel"),
        name="hgrn2_step",
    )(parts, parts, parts, parts, state_t, _col_params(log_lb), _col_params(log1m_lb), _col_params(onorm_g))


def _rwkv_prep(xs, prev, p):
    xm = xs + (prev - xs) * p["mu"]
    r = xm[:, 0:D_BR]
    k = xm[:, D_BR:2 * D_BR]
    v = xm[:, 2 * D_BR:3 * D_BR]
    xw = xm[:, 3 * D_BR:3 * D_BR + LORA]
    xa = xm[:, 3 * D_BR + LORA:3 * D_BR + 2 * LORA]
    w = -_softplus(-(p["w0"] + _bdot(jnp.tanh(xw), p["w2"]))) - 0.5
    logw = -jnp.exp(w)
    a = _sigmoid(p["a0"] + _bdot(xa, p["a2"]))
    kk = k * p["k_k"]
    kk = kk / jnp.maximum(jnp.sqrt(_seg_sum(kk * kk)), 1e-12)
    k = k * (1.0 + (a - 1.0) * p["k_a"])
    bonus = _seg_sum(r * k * p["r_k"]) * v
    return r, logw, k, v, -kk, kk * a, bonus


def _rwkv_post(y, bonus, g, p):
    mean = _seg_sum(y) * (1.0 / HD)
    yc = y - mean
    var = _seg_sum(yc * yc) * (1.0 / HD)
    yn = yc * lax.rsqrt(var + RWKV_GN_EPS) * p["gn_w"] + p["gn_b"]
    return (yn + bonus) * _silu(g)


def _unit_lower_inverse(m, m_t, n):
    nb = INV_BLOCK
    rows = lax.broadcasted_iota(jnp.int32, (nb, 1), 0)
    nm = len(m)
    per = n // nb
    group = lax.broadcasted_iota(jnp.int32, (nb, n), 1) // nb
    diag_t = []
    for q in range(nm):
        dq = m_t[q][0:nb, :]
        for b in range(1, per):
            dq = jnp.where(group == b, m_t[q][b * nb:(b + 1) * nb, :], dq)
        diag_t.append(dq)
    qpl = LANES // n
    tiles = [jnp.concatenate(diag_t[t * qpl:(t + 1) * qpl], axis=1) for t in range(nm // qpl)]
    lane = lax.broadcasted_iota(jnp.int32, (nb, LANES), 1)
    base = lane - lane % nb
    ds = [jnp.where(lane % nb == rows, 1.0, 0.0).astype(F32)] * len(tiles)
    for i in range(1, nb):
        coef = [jnp.take_along_axis(t, base + i, axis=1) for t in tiles]
        rws = [jnp.sum(cf * d, axis=0, keepdims=True) for cf, d in zip(coef, ds)]
        ds = [d + jnp.where(rows == i, rw, 0.0) for d, rw in zip(ds, rws)]
    blocks = [[ds[q // qpl][:, (q % qpl) * n + b * nb:(q % qpl) * n + (b + 1) * nb] for b in range(per)]
              for q in range(nm)]
    size = nb
    while size < n:
        pairs = [(q, j) for q in range(nm) for j in range(0, len(blocks[q]), 2)]
        inner = [_rw_inv(m[q][(j + 1) * size:(j + 2) * size, j * size:(j + 1) * size], blocks[q][j])
                 for q, j in pairs]
        cross = [_rw_inv(blocks[q][j + 1], t) for (q, j), t in zip(pairs, inner)]
        merged = [[] for _ in range(nm)]
        for (q, j), cr in zip(pairs, cross):
            top = jnp.concatenate([blocks[q][j], jnp.zeros((size, size), F32)], axis=1)
            bot = jnp.concatenate([cr, blocks[q][j + 1]], axis=1)
            merged[q].append(jnp.concatenate([top, bot], axis=0))
        blocks = merged
        size *= 2
    return [b[0] for b in blocks]


_rw_score = _bdot_nt
_rw_score_out = _bdot_nt
_rw_inv = _bdot
_rw_solve = _bdot
_rw_read = _bdot_nt
_rw_read_out = _bdot_nt
_rw_apply = _bdot
_rw_apply_out = _bdot
_rw_update = _bdot_tn
_rw_state = _bdot

_RWKV_PARAMS = ("mu", "w0", "w2", "a0", "a2", "k_k", "k_a", "r_k", "gn_w", "gn_b")


def _rwkv_seq_kernel(pb_ref, shift0_ref, s0_ref, *rest, tb, chunk, attn_rows):
    np_ = len(_RWKV_PARAMS)
    p = {n: r[...] for n, r in zip(_RWKV_PARAMS, rest[:np_])}
    n_ai, n_ao = 8, 3
    attn_in = rest[np_:np_ + n_ai]
    o_ref, sfin_ref, shift_ref = rest[np_ + n_ai:np_ + n_ai + 3]
    attn_out = rest[np_ + n_ai + 3:np_ + n_ai + 3 + n_ao]
    r_s, k_s, v_s, a_s, b_s, w_s, g_s, y_s, bon_s, st_s, carry_s = rest[np_ + n_ai + 3 + n_ao:]
    step = pl.program_id(1)
    _attn_step_kernel(*attn_in, *attn_out, bb=attn_rows)

    @pl.when(step == 0)
    def _():
        st_s[...] = s0_ref[0]
        carry_s[...] = shift0_ref[0]

    xs = pb_ref[0][:, 0:B_SHIFT_W]
    row = lax.broadcasted_iota(jnp.int32, (tb, 1), 0)
    prev = jnp.where(row == 0, carry_s[...], pltpu.roll(xs, 1, 0))
    carry_s[...] = xs[tb - 1:tb, :]
    r, logw, k, v, av, bv, bonus = _rwkv_prep(xs, prev, p)
    gcum = _chunk_cumsum(logw, chunk)
    bon_s[...] = bonus
    for h in range(NH):
        sl = slice(h * HD, (h + 1) * HD)
        r_s[h] = r[:, sl]
        k_s[h] = k[:, sl]
        v_s[h] = v[:, sl]
        a_s[h] = av[:, sl]
        b_s[h] = bv[:, sl]
        w_s[h] = logw[:, sl]
        g_s[h] = gcum[:, sl]

    ri = lax.broadcasted_iota(jnp.int32, (chunk, chunk), 0)
    ci = lax.broadcasted_iota(jnp.int32, (chunk, chunk), 1)
    strict = ci < ri
    incl = ci <= ri

    def run_chunks(chunks):
        units = [(c, h) for c in chunks for h in range(NH)]
        hs = range(len(units))
        ld = lambda ref: [ref[h, c * chunk:(c + 1) * chunk, :] for c, h in units]
        rc, kc, vc, ac, bc, wc, gc = (ld(x) for x in (r_s, k_s, v_s, a_s, b_s, w_s, g_s))
        glast = [g[chunk - 1:chunk, :] for g in gc]
        einv = [jnp.exp(-g) for g in gc]
        etail = [jnp.exp(gl - g) for gl, g in zip(glast, gc)]
        a_t = [a * jnp.exp(g - w) for a, g, w in zip(ac, gc, wc)]
        r_t = [r * jnp.exp(g) for r, g in zip(rc, gc)]
        b_t = [b * e for b, e in zip(bc, einv)]
        k_t = [k * e for k, e in zip(kc, einv)]
        b_h = [b * e for b, e in zip(bc, etail)]
        k_h = [k * e for k, e in zip(kc, etail)]
        m_ab = [jnp.where(strict, _rw_score(a_t[h], b_t[h]), 0.0) for h in hs]
        m_ab_t = [jnp.where(ri < ci, _rw_score(b_t[h], a_t[h]), 0.0) for h in hs]
        m_ak = [jnp.where(strict, _rw_score(a_t[h], k_t[h]), 0.0) for h in hs]
        n_rb = [jnp.where(incl, _rw_score_out(r_t[h], b_t[h]), 0.0) for h in hs]
        n_rk = [jnp.where(incl, _rw_score_out(r_t[h], k_t[h]), 0.0) for h in hs]
        inv = _unit_lower_inverse(m_ab, m_ab_t, chunk)
        x = [_rw_apply(m_ak[h], vc[h]) for h in hs]
        a_hat = [_rw_solve(inv[h], a_t[h]) for h in hs]
        u = [_rw_solve(inv[h], x[h]) for h in hs]
        r_hat = [r_t[h] + _rw_apply_out(n_rb[h], a_hat[h]) for h in hs]
        y0 = [_rw_apply_out(n_rb[h], u[h]) + _rw_apply_out(n_rk[h], vc[h]) for h in hs]
        w_kk = [_rw_update(a_hat[h], b_h[h]) for h in hs]
        z = [_rw_update(u[h], b_h[h]) + _rw_update(vc[h], k_h[h]) for h in hs]
        decay = [jnp.exp(gl) for gl in glast]
        s = [st_s[h] for h in range(NH)]
        for i, (c, h) in enumerate(units):
            y_s[c * chunk:(c + 1) * chunk, h * HD:(h + 1) * HD] = _rw_read_out(r_hat[i], s[h]) + y0[i]
            s[h] = s[h] * decay[i] + _rw_state(s[h], w_kk[i]) + z[i]
        for h in range(NH):
            st_s[h] = s[h]

    n_chunks = tb // chunk
    for c0 in range(0, n_chunks, CHUNKS_PER_PASS):
        run_chunks(range(c0, min(c0 + CHUNKS_PER_PASS, n_chunks)))

    o_ref[0] = _rwkv_post(y_s[...], bon_s[...], pb_ref[0][:, B_SHIFT_W:B_W], p)

    @pl.when(step == pl.num_programs(1) - 1)
    def _():
        sfin_ref[0] = st_s[...]
        shift_ref[0] = carry_s[...]


def _rwkv_param_specs(params, nargs):
    zero2 = (lambda b, i: (0, 0)) if nargs == 2 else (lambda i: (0, 0))
    return [pl.BlockSpec(x.shape, zero2) for x in params]


def _rwkv_seq(pb, shift0, s0, params, attn_args):
    B, T, _ = pb.shape
    tb = min(MIX_TILE, T)
    nt = T // tb
    rows, a_in, a_in_specs, a_out_specs, a_out_shapes = _attn_step_operands(
        *attn_args, n_steps=B * nt, step_of=lambda b, i: b * nt + i)
    st = pl.BlockSpec((1, NH, HD, HD), lambda b, i: (b, 0, 0, 0))
    sh = pl.BlockSpec((1, 1, B_SHIFT_W), lambda b, i: (b, 0, 0))
    res = pl.pallas_call(
        functools.partial(_rwkv_seq_kernel, tb=tb, chunk=CHUNK_B, attn_rows=rows),
        grid=(B, nt),
        in_specs=[pl.BlockSpec((1, tb, B_W), lambda b, i: (b, i, 0)), sh, st]
                 + _rwkv_param_specs(params, 2) + a_in_specs,
        out_specs=[pl.BlockSpec((1, tb, D_BR), lambda b, i: (b, i, 0)), st, sh] + a_out_specs,
        out_shape=[jax.ShapeDtypeStruct((B, T, D_BR), F32),
                   jax.ShapeDtypeStruct((B, NH, HD, HD), F32),
                   jax.ShapeDtypeStruct((B, 1, B_SHIFT_W), F32)] + a_out_shapes,
        scratch_shapes=[pltpu.VMEM((NH, tb, HD), F32)] * 7
                       + [pltpu.VMEM((tb, D_BR), F32), pltpu.VMEM((tb, D_BR), F32),
                          pltpu.VMEM((NH, HD, HD), F32), pltpu.VMEM((1, B_SHIFT_W), F32)],
        compiler_params=_cparams("parallel", "arbitrary"),
        name="rwkv7_seq",
    )(pb, shift0, s0, *params, *a_in)
    return res[0], res[1], res[2], [a.reshape(-1, D_BR) for a in res[3:]]


def _rwkv_step_kernel(r_ref, k_ref, v_ref, lr_ref, g_ref, pr_ref, pk_ref, pv_ref, plr_ref,
                      mr_ref, mk_ref, mv_ref, mlr_ref, w0_ref, w2_ref, a0_ref, a2_ref,
                      kk_ref, ka_ref, rk_ref, gw_ref, gb_ref, s_ref, o_ref, snew_ref, y_s):
    lerp = lambda x, prev, mu: x[...] + (prev[...] - x[...]) * mu[...]
    r = lerp(r_ref, pr_ref, mr_ref)
    k = lerp(k_ref, pk_ref, mk_ref)
    v = lerp(v_ref, pv_ref, mv_ref)
    lr = lerp(lr_ref, plr_ref, mlr_ref)
    w = -_softplus(-(w0_ref[...] + _bdot(w2_ref[...], jnp.tanh(lr[0:LORA])))) - 0.5
    dec = jnp.exp(-jnp.exp(w))
    a = _sigmoid(a0_ref[...] + _bdot(a2_ref[...], lr[LORA:2 * LORA]))
    kk = k * kk_ref[...]
    kk = kk / jnp.maximum(jnp.sqrt(jnp.sum(kk * kk, axis=0, keepdims=True)), 1e-12)
    k = k * (1.0 + (a - 1.0) * ka_ref[...])
    bonus = jnp.sum(r * k * rk_ref[...], axis=0, keepdims=True) * v
    av = -kk
    bv = kk * a
    for i in range(HD):
        s = s_ref[i]
        sa = jnp.sum(s * av, axis=0, keepdims=True)
        s_new = s * dec + sa * bv + v[i:i + 1, :] * k
        snew_ref[i] = s_new
        y_s[i:i + 1, :] = jnp.sum(s_new * r, axis=0, keepdims=True)
    y = y_s[...]
    mean = jnp.mean(y, axis=0, keepdims=True)
    yc = y - mean
    var = jnp.mean(yc * yc, axis=0, keepdims=True)
    yn = yc * lax.rsqrt(var + RWKV_GN_EPS) * gw_ref[...] + gb_ref[...]
    o_ref[...] = (yn + bonus) * _silu(g_ref[...])


def _rwkv_step(pb_t, shift_t, state_t, layer, params):
    B = pb_t.shape[1]
    p = dict(zip(_RWKV_PARAMS, params))
    parts = pb_t.reshape(B_W // HD, HD, B)
    prev = shift_t.reshape(DEPTH, B_SHIFT_W // HD, HD, B)
    mu = p["mu"].reshape(B_SHIFT_W // HD, HD, 1)
    lr_row = 3 * NH
    part = lambda j: pl.BlockSpec((None, HD, B), lambda h: (j * NH + h, 0, 0))
    pprev = lambda j: pl.BlockSpec((None, None, HD, B), lambda h: (layer, j * NH + h, 0, 0))
    pmu = lambda j: pl.BlockSpec((None, HD, 1), lambda h: (j * NH + h, 0, 0))
    col = pl.BlockSpec((None, HD, 1), lambda h: (h, 0, 0))
    lora = pl.BlockSpec((HD, LORA), lambda h: (h, 0))
    cols = [_col_params(p[n]) for n in ("k_k", "k_a", "r_k", "gn_w", "gn_b")]
    return pl.pallas_call(
        _rwkv_step_kernel,
        grid=(NH,),
        in_specs=[part(0), part(1), part(2),
                  pl.BlockSpec((None, HD, B), lambda h: (lr_row, 0, 0)),
                  pl.BlockSpec((None, HD, B), lambda h: (lr_row + 1 + h, 0, 0)),
                  pprev(0), pprev(1), pprev(2),
                  pl.BlockSpec((None, None, HD, B), lambda h: (layer, lr_row, 0, 0)),
                  pmu(0), pmu(1), pmu(2),
                  pl.BlockSpec((None, HD, 1), lambda h: (lr_row, 0, 0)),
                  col, lora, col, lora, col, col, col, col, col,
                  pl.BlockSpec((None, None, HD, HD, B), lambda h: (layer, h, 0, 0, 0))],
        out_specs=[pl.BlockSpec((HD, B), lambda h: (h, 0)),
                   pl.BlockSpec((None, HD, HD, B), lambda h: (h, 0, 0, 0))],
        out_shape=[jax.ShapeDtypeStruct((D_BR, B), F32),
                   jax.ShapeDtypeStruct((NH, HD, HD, B), F32)],
        scratch_shapes=[pltpu.VMEM((HD, B), F32)],
        compiler_params=_cparams("parallel"),
        name="rwkv7_step",
    )(parts, parts, parts, parts, parts, prev, prev, prev, prev, mu, mu, mu, mu,
      _col_params(p["w0"]), p["w2"].T, _col_params(p["a0"]), p["a2"].T, *cols, state_t)


def _qkv_prep_kernel(pc_ref, cos_ref, sin_ref, qg_ref, kg_ref, *rest, tm):
    nd = len(C_CONFIGS)
    outs = rest[:3 * nd]
    kc_ref = rest[3 * nd]
    scr = rest[3 * nd + 1:]
    pc = pc_ref[0]
    cos = cos_ref[...]
    sin = sin_ref[...]
    q = _rope(_head_rms(pc[:, 0:D_BR], qg_ref[...]), cos, sin) * (HD ** -0.5)
    k = _rope(_head_rms(pc[:, D_BR:2 * D_BR], kg_ref[...]), cos, sin)
    kc_ref[0] = k
    for a, x in enumerate((q, k, pc[:, 2 * D_BR:3 * D_BR])):
        for half in range(D_BR // LANES):
            scr[a][half] = x[:, half * LANES:(half + 1) * LANES]
    hpl = LANES // HD
    for ci, (_, dil) in enumerate(C_CONFIGS):
        for a in range(3):
            out = outs[3 * ci + a]
            for rho in range(dil):
                for half in range(D_BR // LANES):
                    rows = scr[a][half, pl.ds(rho, tm // dil, stride=dil), :]
                    for j in range(hpl):
                        out[0, rho, half * hpl + j] = rows[:, j * HD:(j + 1) * HD].astype(BF16)


def _swa_kernel(q_ref, kc_ref, vc_ref, kp_ref, vp_ref, o_ref, lse_ref, *, qt):
    i = pl.program_id(1)
    n = Q_BLOCK
    r = lax.broadcasted_iota(jnp.int32, (n, 2 * n), 0)
    c = lax.broadcasted_iota(jnp.int32, (n, 2 * n), 1)
    band = (c >= r) & (c <= r + n)
    first = band & (c >= jnp.where(i > 0, 0, n))
    units = [(u, h) for u in range(qt // n) for h in range(NH)]

    def window(cur_ref, prev_ref, u, h):
        if u == 0:
            return jnp.concatenate([prev_ref[0, h], cur_ref[0, h, 0:n, :]], axis=0)
        return cur_ref[0, h, (u - 1) * n:(u + 1) * n, :]

    s = [lax.dot_general(q_ref[0, h, u * n:(u + 1) * n, :], window(kc_ref, kp_ref, u, h),
                         (((1,), (1,)), ((), ())), preferred_element_type=F32) for u, h in units]
    s = [jnp.where(first if u == 0 else band, x, MASK_VALUE) for (u, h), x in zip(units, s)]
    m = [jnp.max(x, axis=1, keepdims=True) for x in s]
    p = [jnp.exp(x - mx) for x, mx in zip(s, m)]
    l = [jnp.sum(x, axis=1, keepdims=True) for x in p]
    o = [jnp.dot(x.astype(BF16), window(vc_ref, vp_ref, u, h), preferred_element_type=F32) / lx
         for (u, h), x, lx in zip(units, p, l)]
    lse = [jnp.broadcast_to(mx + jnp.log(lx), (n, HD)) for mx, lx in zip(m, l)]
    for u in range(qt // n):
        o_ref[0, u * n:(u + 1) * n, :] = jnp.concatenate(o[u * NH:(u + 1) * NH], axis=1)
        lse_ref[0, u * n:(u + 1) * n, :] = jnp.concatenate(lse[u * NH:(u + 1) * NH], axis=1)


def _swa(q, k, v):
    N, _, L, _ = q.shape
    n = Q_BLOCK
    qt = min(ATTN_TILE, L)
    cur = pl.BlockSpec((1, NH, qt, HD), lambda s, i: (s, 0, i, 0))
    prev = pl.BlockSpec((1, NH, n, HD), lambda s, i: (s, 0, jnp.maximum(i * (qt // n) - 1, 0), 0))
    out = pl.BlockSpec((1, qt, D_BR), lambda s, i: (s, i, 0))
    return pl.pallas_call(
        functools.partial(_swa_kernel, qt=qt),
        grid=(N, L // qt),
        in_specs=[cur, cur, cur, prev, prev],
        out_specs=[out, out],
        out_shape=[jax.ShapeDtypeStruct((N, L, D_BR), F32)] * 2,
        compiler_params=_cparams("parallel", "parallel"),
        name="sliding_window_attn",
    )(q, k, v, k, v)


def _attn_combine_kernel(*refs, tm):
    nd = len(C_CONFIGS)
    o_refs, lse_refs = refs[0:nd], refs[nd:2 * nd]
    g_ref, out_ref = refs[2 * nd], refs[2 * nd + 1]
    o_s, lse_s = refs[2 * nd + 2], refs[2 * nd + 3]
    os_, lses = [], []
    for ci, (_, dil) in enumerate(C_CONFIGS):
        if dil == 1:
            os_.append(o_refs[ci][0, 0])
            lses.append(lse_refs[ci][0, 0])
            continue
        nh = D_BR // LANES
        for rho in range(dil):
            for half in range(nh):
                ls = slice(half * LANES, (half + 1) * LANES)
                o_s[ci, half, pl.ds(rho, tm // dil, stride=dil), :] = o_refs[ci][0, rho][:, ls]
                lse_s[ci, half, pl.ds(rho, tm // dil, stride=dil), :] = lse_refs[ci][0, rho][:, ls]
        os_.append(jnp.concatenate([o_s[ci, half] for half in range(nh)], axis=1))
        lses.append(jnp.concatenate([lse_s[ci, half] for half in range(nh)], axis=1))
    top = functools.reduce(jnp.maximum, lses)
    ws = [jnp.exp(x - top) for x in lses]
    num = functools.reduce(lambda a, b: a + b, [w * o for w, o in zip(ws, os_)])
    den = functools.reduce(lambda a, b: a + b, ws)
    out_ref[0] = num / den * _silu(g_ref[0])


def _attn_seq(prep, B, T):
    assert T % (Q_BLOCK * max(d for _, d in C_CONFIGS)) == 0
    outs, lses = [], []
    for ci, (w, dil) in enumerate(C_CONFIGS):
        assert w == Q_BLOCK * dil
        q, k, v = (x.reshape(B * dil, NH, T // dil, HD) for x in prep[3 * ci:3 * ci + 3])
        o, lse = _swa(q, k, v)
        outs.append(o.reshape(B, dil, T // dil, D_BR))
        lses.append(lse.reshape(B, dil, T // dil, D_BR))
    return outs, lses


def _attn_step_kernel(pc_ref, cos_ref, sin_ref, qg_ref, kg_ref, mult_ref, kt_ref, vt_ref,
                      o_ref, knew_ref, vnew_ref, *, bb):
    pc = pc_ref[...]
    cos = cos_ref[...]
    sin = sin_ref[...]
    q = _rope(_head_rms(pc[:, 0:D_BR], qg_ref[...]), cos, sin) * (HD ** -0.5)
    k = _rope(_head_rms(pc[:, D_BR:2 * D_BR], kg_ref[...]), cos, sin)
    v = pc[:, 2 * D_BR:3 * D_BR]
    knew_ref[...] = k
    vnew_ref[...] = v
    mult = mult_ref[...]
    live = mult > 0.0
    n_self = float(len(C_CONFIGS))
    eye = _eye(HD)
    for b in range(bb):
        parts = []
        for h in range(NH):
            sl = slice(h * HD, (h + 1) * HD)
            qr, kr, vr = q[b:b + 1, sl], k[b:b + 1, sl], v[b:b + 1, sl]
            s_self = jnp.sum(qr * kr, axis=1, keepdims=True)
            s = jnp.sum(kt_ref[b, h] * _to_col(qr, eye), axis=0, keepdims=True)
            s = jnp.where(live, s, MASK_VALUE)
            m = jnp.maximum(jnp.max(s, axis=1, keepdims=True), s_self)
            p = jnp.exp(s - m) * mult
            p_self = n_self * jnp.exp(s_self - m)
            den = jnp.sum(p, axis=1, keepdims=True) + p_self
            acc = _to_row(jnp.sum(vt_ref[b, h] * p, axis=1, keepdims=True), eye) + p_self * vr
            parts.append(acc / den)
        o_ref[b:b + 1, :] = jnp.concatenate(parts, axis=1)
    o_ref[...] = o_ref[...] * _silu(pc[:, 3 * D_BR:4 * D_BR])


def _attn_step_operands(pc, cos, sin, qg, kg, cache_kt, cache_vt, layer, n_steps, step_of):
    B = pc.shape[0]
    P = cache_kt.shape[-1]
    bb = B // n_steps
    assert bb * n_steps == B
    dist = P - jnp.arange(P)
    mult = jnp.zeros((P,), F32)
    for w, dil in C_CONFIGS:
        mult = mult + ((dist <= w) & (dist % dil == 0)).astype(F32)
    fixed = lambda shape: pl.BlockSpec(shape, lambda *g: (0,) * len(shape))
    row = pl.BlockSpec((None, bb, D_BR), lambda *g: (step_of(*g), 0, 0))
    cache = pl.BlockSpec((None, bb, NH, HD, P), lambda *g: (layer, step_of(*g), 0, 0, 0))
    inputs = (pc.reshape(n_steps, bb, C_W), cos, sin, qg, kg, mult.reshape(1, P), cache_kt, cache_vt)
    in_specs = [pl.BlockSpec((None, bb, C_W), lambda *g: (step_of(*g), 0, 0)),
                fixed((1, D_BR)), fixed((1, D_BR)), fixed((1, D_BR)), fixed((1, D_BR)), fixed((1, P)),
                cache, cache]
    out_shapes = [jax.ShapeDtypeStruct((n_steps, bb, D_BR), F32)] * 3
    return bb, inputs, in_specs, [row, row, row], out_shapes


def _pool_mix(pooled, u, g, w_ref, scale_ref):
    mixed = _bdot(pooled - u, w_ref[...]) * scale_ref[...]
    return mixed * _silu(g)


def _pool_select(sums, cnts):
    lane = lax.broadcasted_iota(jnp.int32, sums[0].shape, 1) // (D_BR // len(POOL_WINDOWS))
    out = sums[-1] / cnts[-1]
    for gi in range(len(POOL_WINDOWS) - 2, -1, -1):
        out = jnp.where(lane == gi, sums[gi] / cnts[gi], out)
    return out


def _pool_seq_kernel(pd_ref, buf0_ref, w_ref, scale_ref, o_ref, carry_s, *, tb, t0):
    step = pl.program_id(1)

    @pl.when(step == 0)
    def _():
        carry_s[...] = buf0_ref[0]

    pd = pd_ref[0]
    u = pd[:, 0:D_BR]
    ext = jnp.concatenate([carry_s[...], u], axis=0)
    carry_s[...] = u[tb - POOL_MAX:tb, :]
    pos = (t0 + step * tb + lax.broadcasted_iota(jnp.int32, (tb, 1), 0)).astype(F32)
    sums, cnts = [], []
    run = ext
    shift = 1
    for w in POOL_WINDOWS:
        while shift < w:
            run = run + pltpu.roll(run, shift, 0)
            shift *= 2
        sums.append(run[POOL_MAX:, :])
        cnts.append(jnp.minimum(float(w), pos + 1.0))
    o_ref[0] = _pool_mix(_pool_select(sums, cnts), u, pd[:, D_BR:2 * D_BR], w_ref, scale_ref)


def _pool_step_kernel(pd_ref, buf_ref, w_ref, scale_ref, o_ref, *, t0):
    pd = pd_ref[...]
    u = pd[:, 0:D_BR]
    nbuf = POOL_MAX - 1
    sums, cnts = [], []
    run = u
    taken = 1
    for w in POOL_WINDOWS:
        while taken < w:
            run = run + buf_ref[nbuf - taken]
            taken += 1
        sums.append(run)
        cnts.append(float(min(w, t0 + 1)))
    o_ref[...] = _pool_mix(_pool_select(sums, cnts), u, pd[:, D_BR:2 * D_BR], w_ref, scale_ref)


def _pool_step(pd, buf, layer, w_bd, scale, t0):
    B = pd.shape[0]
    bb = min(POOL_STEP_ROWS, B)
    return pl.pallas_call(
        functools.partial(_pool_step_kernel, t0=t0),
        grid=(B // bb,),
        in_specs=[pl.BlockSpec((bb, D_W), lambda i: (i, 0)),
                  pl.BlockSpec((None, POOL_MAX - 1, bb, D_BR), lambda i: (layer, 0, i, 0)),
                  pl.BlockSpec((D_BR, D_BR), lambda i: (0, 0)),
                  pl.BlockSpec((1, D_BR), lambda i: (0, 0))],
        out_specs=pl.BlockSpec((bb, D_BR), lambda i: (i, 0)),
        out_shape=jax.ShapeDtypeStruct((B, D_BR), F32),
        compiler_params=_cparams("parallel"),
        name="pool_step",
    )(pd, buf, w_bd, scale)


def _rope_tables(pos):
    half = HD // 2
    inv_freq = jnp.float32(ROPE_THETA) ** (-jnp.arange(half, dtype=F32) / half)
    ang = pos.astype(F32)[:, None] * inv_freq[None, :]
    cos = jnp.cos(ang)
    sin = jnp.sin(ang)
    cos_t = jnp.tile(jnp.concatenate([cos, cos], axis=-1), (1, NH))
    sin_t = jnp.tile(jnp.concatenate([-sin, sin], axis=-1), (1, NH))
    return cos_t, sin_t


def _block_diag(w):
    g, c, d = w.shape
    out = jnp.zeros((g * c, g * d), w.dtype)
    for i in range(g):
        out = out.at[i * c:(i + 1) * c, i * d:(i + 1) * d].set(w[i])
    return out


def _layer_params(l, W):
    row = lambda x: x[l].reshape(1, -1)
    w_in = W["w_in"][l].astype(BF16)
    w_in_t = jnp.swapaxes(W["w_in"], 1, 2)[l].astype(BF16)
    cuts = (0, A_W, A_W + B_W, A_W + B_W + C_W, A_W + B_W + C_W + D_W)
    return dict(
        norm_g=W["norm_g"][l],
        w_parts=[w_in[:, cuts[i]:cuts[i + 1]] for i in range(4)],
        w_parts_t=[w_in_t[cuts[i]:cuts[i + 1], :] for i in range(2)],
        w_out=W["w_out"][l].astype(BF16),
        onorm_g=row(W["a_onorm_g"]),
        rwkv=[row(W["b_mu"]), row(W["b_w0"]), W["b_w2"][l], row(W["b_a0"]), W["b_a2"][l],
              row(W["b_k_k"]), row(W["b_k_a"]), row(W["b_r_k"]), row(W["b_gn_w"]), row(W["b_gn_b"])],
        qg=jnp.tile(W["c_qnorm_g"][l], NH).reshape(1, D_BR),
        kg=jnp.tile(W["c_knorm_g"][l], NH).reshape(1, D_BR),
        w_pool=_block_diag(W["d_w_pool"][l]),
        p_scale=row(W["d_scale"]),
    )


def _run_groups(x, xs, mods, mods_s, layers, log_lb, log1m_lb, sA, sBw, sBs, ck, cv, cd):
    B, T, _ = x.shape
    Bs = xs.shape[0]
    assert xs.shape[1] == 1 and ck.shape[2] == PAST_LEN == C_WIN_MAX
    cos, sin = _rope_tables(jnp.arange(T))
    cos_s, sin_s = _rope_tables(jnp.full((1,), PAST_LEN))
    c_keep = min(C_WIN_MAX, T)
    d_keep = min(POOL_MAX - 1, T)
    zero_state = jnp.zeros((B, NH, HD, HD), F32)
    zero_shift = jnp.zeros((B, 1, B_SHIFT_W), F32)
    zero_pool = jnp.zeros((B, POOL_MAX, D_BR), F32)
    ck_t = jnp.transpose(ck, (0, 1, 3, 4, 2))
    cv_t = jnp.transpose(cv, (0, 1, 3, 4, 2))
    sA_t = jnp.transpose(sA, (0, 2, 3, 4, 1))
    sBw_t = jnp.transpose(sBw, (0, 2, 3, 4, 1))
    sBs_t = jnp.transpose(sBs, (0, 2, 1))
    cd_t = jnp.transpose(cd, (0, 2, 1, 3))
    xr = xs.reshape(Bs, D_MODEL)
    pA, pBw, pBs, pK, pV, pD = [], [], [], [], [], []
    nA, nBw, nBs, nK, nV, nD = [], [], [], [], [], []
    for l, L in enumerate(layers):
        shift, scale, gate_s = (mods_s[l][:, i * D_MODEL:(i + 1) * D_MODEL] for i in range(3))
        pa_t, pb_t, pc_s, pd_s = _in_proj_step(xr, shift, scale, L["norm_g"], *L["w_parts_t"], *L["w_parts"][2:])
        shift, scale, gate = (mods[l][:, None, i * D_MODEL:(i + 1) * D_MODEL] for i in range(3))
        (pa, pb, pc, pd), prep = _in_proj(x, shift, scale, L["norm_g"], L["w_parts"],
                                          cos, sin, L["qg"], L["kg"])
        k_rot = prep[-1]
        oa, sa_t = _hgrn2_seq(pa, zero_state, log_lb[l], log1m_lb[l], L["onorm_g"])
        ob, sb, sbs, (oc_s, k_new, v_new) = _rwkv_seq(
            pb, zero_shift, zero_state, L["rwkv"], (pc_s, cos_s, sin_s, L["qg"], L["kg"], ck_t, cv_t, l))
        attn_outs, attn_lses = _attn_seq(prep, B, T)
        x = _out_proj(oa, ob, attn_outs, attn_lses, pc, pd, zero_pool, L["w_pool"], L["p_scale"], 0,
                      x, gate, L["w_out"])
        pA.append(jnp.swapaxes(sa_t, -1, -2))
        pBw.append(sb)
        pBs.append(sbs[:, 0])
        pK.append(k_rot[:, T - c_keep:].reshape(B, c_keep, NH, HD))
        pV.append(pc[:, T - c_keep:, 2 * D_BR:3 * D_BR].reshape(B, c_keep, NH, HD))
        pD.append(pd[:, T - d_keep:, 0:D_BR])
        oa_t, sa = _hgrn2_step(pa_t, sA_t, l, log_lb[l], log1m_lb[l], L["onorm_g"])
        ob_t, sbw = _rwkv_step(pb_t, sBs_t, sBw_t, l, L["rwkv"])
        od_s = _pool_step(pd_s, cd_t, l, L["w_pool"], L["p_scale"], PAST_LEN)
        xr = _out_proj_step(oa_t, ob_t, oc_s, od_s, xr, gate_s, L["w_out"])
        nA.append(sa)
        nBw.append(sbw)
        nBs.append(pb_t[0:B_SHIFT_W])
        nK.append(k_new.reshape(Bs, 1, NH, HD))
        nV.append(v_new.reshape(Bs, 1, NH, HD))
        nD.append(pd_s[:, None, 0:D_BR])
    back = lambda a: jnp.transpose(jnp.stack(a), (0, 4, 1, 2, 3))
    prompt = (x,) + tuple(jnp.stack(a) for a in (pA, pBw, pBs, pK, pV, pD))
    sample = (xr.reshape(Bs, 1, D_MODEL), back(nA), back(nBw), jnp.transpose(jnp.stack(nBs), (0, 2, 1)),
              jnp.stack(nK), jnp.stack(nV), jnp.stack(nD))
    return prompt, sample


def kernel(x_prompt, x_sample, state_A, state_B_wkv, state_B_shift, cache_C_k, cache_C_v, cache_D_pool, c_prompt, c_sample, ada_w, ada_b, norm_g, w_in, w_out, a_lb_logits, a_onorm_g, b_mu, b_w0, b_w2, b_a0, b_a2, b_k_k, b_k_a, b_r_k, b_gn_w, b_gn_b, c_qnorm_g, c_knorm_g, d_w_pool, d_scale):
    W = dict(norm_g=norm_g, w_in=w_in, w_out=w_out, a_onorm_g=a_onorm_g, b_mu=b_mu, b_w0=b_w0,
             b_w2=b_w2, b_a0=b_a0, b_a2=b_a2, b_k_k=b_k_k, b_k_a=b_k_a, b_r_k=b_r_k,
             b_gn_w=b_gn_w, b_gn_b=b_gn_b, c_qnorm_g=c_qnorm_g, c_knorm_g=c_knorm_g,
             d_w_pool=d_w_pool, d_scale=d_scale)
    layers = [_layer_params(l, W) for l in range(DEPTH)]
    lb_sm = jax.nn.softmax(a_lb_logits.astype(F32), axis=0)
    lower = jnp.cumsum(lb_sm, axis=0) - lb_sm[0:1]
    log_lb = jnp.log(jnp.maximum(lower, LB_FLOOR))[:, None, :]
    log1m_lb = jnp.log1p(-lower)[:, None, :]

    nb = x_prompt.shape[0]
    mods = _modulation(jnp.concatenate([c_prompt, c_sample], axis=0), ada_w, ada_b)
    (y_p, pA, pBw, pBs, pK, pV, pD), (y_s, sA, sBw, sBs, sK, sV, sD) = _run_groups(
        x_prompt, x_sample, mods[:, :nb], mods[:, nb:], layers, log_lb, log1m_lb,
        state_A, state_B_wkv, state_B_shift, cache_C_k, cache_C_v, cache_D_pool)
    return (y_p, y_s, pA, sA, pBw, sBw, pBs, sBs, pK, sK, pV, sV, pD, sD)
```

```python
import functools

import jax
import jax.numpy as jnp
from jax import lax
from jax.experimental import pallas as pl
from jax.experimental.pallas import tpu as pltpu

F32 = jnp.float32
BF16 = jnp.bfloat16

D_MODEL = 1024
DEPTH = 4
PAST_LEN = 2048
D_BR = 256
HD = 64
NH = D_BR // HD
LB_FLOOR = 1e-30
LORA = 32
RWKV_GN_EPS = 64e-5
C_CONFIGS = ((128, 1), (512, 4), (2048, 16))
C_WIN_MAX = 2048
MASK_VALUE = -1e30
ROPE_THETA = 10000.0
POOL_WINDOWS = (2, 4, 8, 16)
POOL_MAX = 16
EPS = 1e-6
A_W = 4 * D_BR
B_SHIFT_W = 3 * D_BR + 2 * LORA
B_W = B_SHIFT_W + D_BR
C_W = 4 * D_BR
D_W = 2 * D_BR

LANES = 128

ROW_TILE = 512
MIX_TILE = 256
CHUNK_A = 8
CHUNKS_PER_IT_A = 16
CHUNK_B = 64
INV_BLOCK = 16
CHUNKS_PER_PASS = 4
Q_BLOCK = 128
ATTN_TILE = 512
POOL_STEP_ROWS = 32
VMEM_LIMIT = 56 * 1024 * 1024


def _cparams(*sem):
    return pltpu.CompilerParams(dimension_semantics=sem, vmem_limit_bytes=VMEM_LIMIT)


def _sigmoid(x):
    return 1.0 / (1.0 + jnp.exp(-x))


def _silu(x):
    return x * _sigmoid(x)


def _softplus(x):
    return jnp.maximum(x, 0.0) + jnp.log1p(jnp.exp(-jnp.abs(x)))


def _log_sigmoid_pair(x):
    t = jnp.log1p(jnp.exp(-jnp.abs(x)))
    return -(jnp.maximum(-x, 0.0) + t), -(jnp.maximum(x, 0.0) + t)


def _logaddexp(a, b):
    return jnp.maximum(a, b) + jnp.log1p(jnp.exp(-jnp.abs(a - b)))


def _bdot(a, b):
    return jnp.dot(a.astype(BF16), b.astype(BF16), preferred_element_type=F32)


def _bdot_nt(a, b):
    return lax.dot_general(a.astype(BF16), b.astype(BF16), (((1,), (1,)), ((), ())),
                           preferred_element_type=F32)


def _bdot_tn(a, b):
    return lax.dot_general(a.astype(BF16), b.astype(BF16), (((0,), (0,)), ((), ())),
                           preferred_element_type=F32)


def _split3(x):
    h1 = x.astype(BF16)
    r1 = x - h1.astype(F32)
    h2 = r1.astype(BF16)
    r2 = r1 - h2.astype(F32)
    return h1, h2, r2.astype(BF16)


def _seg_sum(x):
    n = x.shape[-1]
    r = lax.broadcasted_iota(jnp.int32, (n, n), 0) // HD
    c = lax.broadcasted_iota(jnp.int32, (n, n), 1) // HD
    ones = jnp.where(r == c, 1.0, 0.0).astype(BF16)
    h1, h2, h3 = _split3(x)
    d = lambda h: jnp.dot(h, ones, preferred_element_type=F32)
    return d(h1) + d(h2) + d(h3)


def _chunk_cumsum(x, chunk):
    t = x.shape[0]
    r = lax.broadcasted_iota(jnp.int32, (t, t), 0)
    c = lax.broadcasted_iota(jnp.int32, (t, t), 1)
    tri = jnp.where((r // chunk == c // chunk) & (c <= r), 1.0, 0.0).astype(BF16)
    h1, h2, h3 = _split3(x)
    d = lambda h: jnp.dot(tri, h, preferred_element_type=F32)
    return d(h1) + d(h2) + d(h3)


def _head_rms(x, gamma):
    ms = _seg_sum(x * x) * (1.0 / HD)
    return x * lax.rsqrt(ms + EPS) * gamma


def _rope(x, cos, sin_signed):
    lane = lax.broadcasted_iota(jnp.int32, x.shape, 1) % HD
    partner = jnp.where(lane < HD // 2,
                        pltpu.roll(x, D_BR - HD // 2, 1),
                        pltpu.roll(x, HD // 2, 1))
    return x * cos + partner * sin_signed


def _eye(n):
    r = lax.broadcasted_iota(jnp.int32, (n, n), 0)
    c = lax.broadcasted_iota(jnp.int32, (n, n), 1)
    return jnp.where(r == c, 1.0, 0.0).astype(F32)


def _to_col(row, eye):
    return jnp.sum(eye * row, axis=1, keepdims=True)


def _to_row(col, eye):
    return jnp.sum(eye * col, axis=0, keepdims=True)


def _mod_kernel(c_ref, w_ref, b_ref, o_ref):
    c = _silu(c_ref[...])
    o_ref[0] = _bdot(c, w_ref[0]) + b_ref[0]


def _modulation(c_all, ada_w, ada_b):
    n = c_all.shape[0]
    tn = 512
    return pl.pallas_call(
        _mod_kernel,
        grid=(DEPTH, 3 * D_MODEL // tn),
        in_specs=[pl.BlockSpec((n, D_MODEL), lambda l, j: (0, 0)),
                  pl.BlockSpec((1, D_MODEL, tn), lambda l, j: (l, 0, j)),
                  pl.BlockSpec((1, 1, tn), lambda l, j: (l, 0, j))],
        out_specs=pl.BlockSpec((1, n, tn), lambda l, j: (l, 0, j)),
        out_shape=jax.ShapeDtypeStruct((DEPTH, n, 3 * D_MODEL), F32),
        compiler_params=_cparams("parallel", "parallel"),
        name="adaln_modulation",
    )(c_all, ada_w, ada_b.reshape(DEPTH, 1, 3 * D_MODEL))


def _in_proj_kernel(x_ref, shift_ref, scale_ref, g_ref, wa_ref, wb_ref, wc_ref, wd_ref,
                    cos_ref, sin_ref, qg_ref, kg_ref, pa_ref, pb_ref, pc_ref, pd_ref, *attn_prep, tm):
    x = x_ref[0]
    ms = jnp.mean(x * x, axis=-1, keepdims=True)
    h = x * lax.rsqrt(ms + EPS) * g_ref[...]
    h = (h * (1.0 + scale_ref[0]) + shift_ref[0]).astype(BF16)
    pc_ref[0] = jnp.dot(h, wc_ref[...], preferred_element_type=F32)
    pa_ref[0] = jnp.dot(h, wa_ref[...], preferred_element_type=F32)
    pb_ref[0] = jnp.dot(h, wb_ref[...], preferred_element_type=F32)
    pd_ref[0] = jnp.dot(h, wd_ref[...], preferred_element_type=F32)
    _qkv_prep_kernel(pc_ref, cos_ref, sin_ref, qg_ref, kg_ref, *attn_prep, tm=tm)


def _in_proj(x, shift, scale, norm_g, w_parts, cos, sin, qg, kg):
    B, T, D = x.shape
    tm = min(ROW_TILE, T)
    mod = pl.BlockSpec((1, 1, D), lambda b, i: (b, 0, 0))
    vec = pl.BlockSpec((1, D_BR), lambda b, i: (0, 0))
    tab = pl.BlockSpec((tm, D_BR), lambda b, i: (i, 0))
    widths = [w.shape[1] for w in w_parts]
    out_specs = [pl.BlockSpec((1, tm, w), lambda b, i: (b, i, 0)) for w in widths]
    out_shape = [jax.ShapeDtypeStruct((B, T, w), F32) for w in widths]
    for _, dil in C_CONFIGS:
        for _ in range(3):
            out_specs.append(pl.BlockSpec((1, dil, NH, tm // dil, HD), lambda b, i: (b, 0, 0, i, 0)))
            out_shape.append(jax.ShapeDtypeStruct((B, dil, NH, T // dil, HD), BF16))
    out_specs.append(pl.BlockSpec((1, tm, D_BR), lambda b, i: (b, i, 0)))
    out_shape.append(jax.ShapeDtypeStruct((B, T, D_BR), F32))
    res = pl.pallas_call(
        functools.partial(_in_proj_kernel, tm=tm),
        grid=(B, T // tm),
        in_specs=[pl.BlockSpec((1, tm, D), lambda b, i: (b, i, 0)), mod, mod,
                  pl.BlockSpec((1, D), lambda b, i: (0, 0))]
                 + [pl.BlockSpec((D, w), lambda b, i: (0, 0), pipeline_mode=pl.Buffered(1)) for w in widths]
                 + [tab, tab, vec, vec],
        out_specs=out_specs,
        out_shape=out_shape,
        scratch_shapes=[pltpu.VMEM((D_BR // LANES, tm, LANES), F32)] * 3,
        compiler_params=_cparams("parallel", "parallel"),
        name="norm_in_proj",
    )(x, shift, scale, norm_g.reshape(1, D), *w_parts, cos, sin, qg, kg)
    return res[:4], res[4:]


def _in_proj_step_kernel(x_ref, shift_ref, scale_ref, g_ref, wat_ref, wbt_ref, wc_ref, wd_ref,
                         pat_ref, pbt_ref, pc_ref, pd_ref):
    x = x_ref[...]
    ms = jnp.mean(x * x, axis=-1, keepdims=True)
    h = x * lax.rsqrt(ms + EPS) * g_ref[...]
    h = (h * (1.0 + scale_ref[...]) + shift_ref[...]).astype(BF16)
    nt = lambda w: lax.dot_general(w, h, (((1,), (1,)), ((), ())), preferred_element_type=F32)
    pat_ref[...] = nt(wat_ref[...])
    pbt_ref[...] = nt(wbt_ref[...])
    pc_ref[...] = jnp.dot(h, wc_ref[...], preferred_element_type=F32)
    pd_ref[...] = jnp.dot(h, wd_ref[...], preferred_element_type=F32)


def _in_proj_step(x, shift, scale, norm_g, wa_t, wb_t, wc, wd):
    B, D = x.shape
    ins = (x, shift, scale, norm_g.reshape(1, D), wa_t, wb_t, wc, wd)
    shapes = [(wa_t.shape[0], B), (wb_t.shape[0], B), (B, wc.shape[1]), (B, wd.shape[1])]
    return pl.pallas_call(
        _in_proj_step_kernel,
        grid=(1,),
        in_specs=[pl.BlockSpec(a.shape, lambda i: (0, 0)) for a in ins],
        out_specs=[pl.BlockSpec(s, lambda i: (0, 0)) for s in shapes],
        out_shape=[jax.ShapeDtypeStruct(s, F32) for s in shapes],
        compiler_params=_cparams("arbitrary"),
        name="norm_in_proj_step",
    )(*ins)


def _out_proj_step_kernel(oat_ref, obt_ref, oc_ref, od_ref, x_ref, gate_ref, w_ref, y_ref):
    tn = lambda a, w: lax.dot_general(a.astype(BF16), w, (((0,), (0,)), ((), ())), preferred_element_type=F32)
    nn = lambda a, w: jnp.dot(a.astype(BF16), w, preferred_element_type=F32)
    res = (tn(oat_ref[...], w_ref[0:D_BR, :]) + tn(obt_ref[...], w_ref[D_BR:2 * D_BR, :])
           + nn(oc_ref[...], w_ref[2 * D_BR:3 * D_BR, :]) + nn(od_ref[...], w_ref[3 * D_BR:4 * D_BR, :]))
    y_ref[...] = x_ref[...] + gate_ref[...] * res


def _out_proj_step(oa_t, ob_t, oc, od, x, gate, w_out):
    ins = (oa_t, ob_t, oc, od, x, gate, w_out)
    return pl.pallas_call(
        _out_proj_step_kernel,
        grid=(1,),
        in_specs=[pl.BlockSpec(a.shape, lambda i: (0, 0)) for a in ins],
        out_specs=pl.BlockSpec(x.shape, lambda i: (0, 0)),
        out_shape=jax.ShapeDtypeStruct(x.shape, F32),
        compiler_params=_cparams("arbitrary"),
        name="out_proj_residual_step",
    )(*ins)


def _out_proj_kernel(oa_ref, ob_ref, *rest, tm, t0):
    nd = len(C_CONFIGS)
    attn_in = rest[:2 * nd + 1]
    pd_ref, buf0_ref, wp_ref, ps_ref, x_ref, gate_ref, w_ref, y_ref = rest[2 * nd + 1:2 * nd + 9]
    oc_s, od_s, o_s, lse_s, carry_s = rest[2 * nd + 9:]
    _attn_combine_kernel(*attn_in, oc_s, o_s, lse_s, tm=tm)
    _pool_seq_kernel(pd_ref, buf0_ref, wp_ref, ps_ref, od_s, carry_s, tb=tm, t0=t0)
    mix = jnp.concatenate([oa_ref[0], ob_ref[0], oc_s[0], od_s[0]], axis=-1).astype(BF16)
    res = jnp.dot(mix, w_ref[...], preferred_element_type=F32)
    y_ref[0] = x_ref[0] + gate_ref[0] * res


def _out_proj(oa, ob, attn_outs, attn_lses, pc, pd, pool_buf0, w_pool, p_scale, t0, x, gate, w_out):
    B, T, D = x.shape
    tm = min(ROW_TILE, T)
    nd = len(C_CONFIGS)
    br = pl.BlockSpec((1, tm, D_BR), lambda b, i: (b, i, 0))
    parts = [pl.BlockSpec((1, dil, tm // dil, D_BR), lambda b, i: (b, 0, i, 0)) for _, dil in C_CONFIGS]
    return pl.pallas_call(
        functools.partial(_out_proj_kernel, tm=tm, t0=t0),
        grid=(B, T // tm),
        in_specs=[br, br] + parts + parts
                 + [pl.BlockSpec((1, tm, D_BR), lambda b, i: (b, i, 3)),
                    pl.BlockSpec((1, tm, D_W), lambda b, i: (b, i, 0)),
                    pl.BlockSpec((1, POOL_MAX, D_BR), lambda b, i: (b, 0, 0)),
                    pl.BlockSpec((D_BR, D_BR), lambda b, i: (0, 0)),
                    pl.BlockSpec((1, D_BR), lambda b, i: (0, 0)),
                    pl.BlockSpec((1, tm, D), lambda b, i: (b, i, 0)),
                    pl.BlockSpec((1, 1, D), lambda b, i: (b, 0, 0)),
                    pl.BlockSpec((D, D), lambda b, i: (0, 0))],
        out_specs=pl.BlockSpec((1, tm, D), lambda b, i: (b, i, 0)),
        out_shape=jax.ShapeDtypeStruct((B, T, D), F32),
        scratch_shapes=[pltpu.VMEM((1, tm, D_BR), F32)] * 2
                       + [pltpu.VMEM((nd, D_BR // LANES, tm, LANES), F32)] * 2
                       + [pltpu.VMEM((POOL_MAX, D_BR), F32)],
        compiler_params=_cparams("parallel", "arbitrary"),
        name="out_proj_residual",
    )(oa, ob, *attn_outs, *attn_lses, pc, pd, pool_buf0, w_pool, p_scale, x, gate, w_out)


def _hgrn2_gates(pa, log_lb, log1m_lb):
    q = _silu(pa[:, 0:D_BR])
    fl = pa[:, D_BR:2 * D_BR]
    v = pa[:, 2 * D_BR:3 * D_BR]
    ls_pos, ls_neg = _log_sigmoid_pair(fl)
    logf = _logaddexp(ls_pos, log_lb + ls_neg)
    k = jnp.exp(log1m_lb + ls_neg)
    return q, logf, k, v


def _hgrn2_seq_kernel(pa_ref, s0_ref, loglb_ref, log1mlb_ref, og_ref, o_ref, sfin_ref,
                      q_s, k_s, v_s, g_s, o_s, st_s, *, tb, chunk, per_it):
    step = pl.program_id(1)
    hpl = LANES // HD
    npair = NH // hpl
    rl = lax.broadcasted_iota(jnp.int32, (LANES, LANES), 0) // HD
    cl = lax.broadcasted_iota(jnp.int32, (LANES, LANES), 1) // HD
    same_head = rl == cl

    @pl.when(step == 0)
    def _():
        for p in range(npair):
            blocks = [jnp.concatenate([s0_ref[0, p * hpl + a] if a == b else jnp.zeros((HD, HD), F32)
                                       for b in range(hpl)], axis=1) for a in range(hpl)]
            st_s[p] = jnp.concatenate(blocks, axis=0)

    q, logf, k, v = _hgrn2_gates(pa_ref[0], loglb_ref[...], log1mlb_ref[...])
    gcum = _chunk_cumsum(logf, chunk)
    for p in range(npair):
        sl = slice(p * LANES, (p + 1) * LANES)
        q_s[p] = q[:, sl]
        k_s[p] = k[:, sl]
        v_s[p] = v[:, sl]
        g_s[p] = gcum[:, sl]

    rows = lax.broadcasted_iota(jnp.int32, (chunk, 1), 0)
    lane_head = lax.broadcasted_iota(jnp.int32, (chunk, LANES), 1) // HD

    def body(it, carry):
        units = [(j, p) for j in range(per_it) for p in range(npair)]
        starts = [pl.multiple_of((it * per_it + j) * chunk, chunk) for j in range(per_it)]
        ld = lambda ref: [ref[p, pl.ds(starts[j], chunk), :] for j, p in units]
        qc, kc, vc, gc = ld(q_s), ld(k_s), ld(v_s), ld(g_s)
        glast = [g[chunk - 1:chunk, :] for g in gc]
        upd = [jnp.where(same_head, _bdot_tn(v_, k_ * jnp.exp(gl - g)), 0.0)
               for v_, k_, gl, g in zip(vc, kc, glast, gc)]
        decay = [jnp.exp(gl) for gl in glast]
        qe = [q_ * jnp.exp(g) for q_, g in zip(qc, gc)]
        o_acc = [jnp.zeros((chunk, LANES), F32) for _ in units]
        for t in range(chunk):
            x = [jnp.exp(jnp.minimum(g[t:t + 1, :] - g, 0.0)) * k_ * q_[t:t + 1, :]
                 for g, k_, q_ in zip(gc, kc, qc)]
            sel = [(lane_head == a) & (rows <= t) for a in range(hpl)]
            att = []
            for xi in x:
                sums = [jnp.sum(jnp.where(m, xi, 0.0), axis=1, keepdims=True) for m in sel]
                a_ = sums[-1]
                for a in range(hpl - 2, -1, -1):
                    a_ = jnp.where(lane_head == a, sums[a], a_)
                att.append(a_)
            ot = [jnp.sum(a * v_, axis=0, keepdims=True) for a, v_ in zip(att, vc)]
            o_acc = [jnp.where(rows == t, r, o) for o, r in zip(o_acc, ot)]
        st = [st_s[p] for p in range(npair)]
        for i, (j, p) in enumerate(units):
            o_s[pl.ds(starts[j], chunk), p * LANES:(p + 1) * LANES] = o_acc[i] + _bdot_nt(qe[i], st[p])
            st[p] = st[p] * decay[i] + upd[i]
        for p in range(npair):
            st_s[p] = st[p]
        return carry

    lax.fori_loop(0, tb // (chunk * per_it), body, 0)

    o = _head_rms(o_s[...], og_ref[...])
    o_ref[0] = o * _silu(pa_ref[0][:, 3 * D_BR:4 * D_BR])

    @pl.when(step == pl.num_programs(1) - 1)
    def _():
        for h in range(NH):
            a = (h % hpl) * HD
            sfin_ref[0, h] = st_s[h // hpl][a:a + HD, a:a + HD]


def _hgrn2_seq(pa, s0_t, log_lb, log1m_lb, onorm_g):
    B, T, _ = pa.shape
    tb = min(MIX_TILE, T)
    vec = pl.BlockSpec((1, D_BR), lambda b, i: (0, 0))
    st = pl.BlockSpec((1, NH, HD, HD), lambda b, i: (b, 0, 0, 0))
    return pl.pallas_call(
        functools.partial(_hgrn2_seq_kernel, tb=tb, chunk=CHUNK_A, per_it=CHUNKS_PER_IT_A),
        grid=(B, T // tb),
        in_specs=[pl.BlockSpec((1, tb, A_W), lambda b, i: (b, i, 0)), st, vec, vec, vec],
        out_specs=[pl.BlockSpec((1, tb, D_BR), lambda b, i: (b, i, 0)), st],
        out_shape=[jax.ShapeDtypeStruct((B, T, D_BR), F32),
                   jax.ShapeDtypeStruct((B, NH, HD, HD), F32)],
        scratch_shapes=[pltpu.VMEM((D_BR // LANES, tb, LANES), F32)] * 4
                       + [pltpu.VMEM((tb, D_BR), F32), pltpu.VMEM((D_BR // LANES, LANES, LANES), F32)],
        compiler_params=_cparams("parallel", "arbitrary"),
        name="hgrn2_seq",
    )(pa, s0_t, log_lb, log1m_lb, onorm_g)


def _hgrn2_step_kernel(q_ref, f_ref, i_ref, g_ref, s_ref, loglb_ref, log1mlb_ref, og_ref, o_ref, snew_ref):
    q = _silu(q_ref[...])
    fl = f_ref[...]
    v = i_ref[...]
    ls_pos, ls_neg = _log_sigmoid_pair(fl)
    f = jnp.exp(_logaddexp(ls_pos, loglb_ref[...] + ls_neg))
    k = jnp.exp(log1mlb_ref[...] + ls_neg)
    o = jnp.zeros_like(v)
    for kk in range(HD):
        s_new = f[kk:kk + 1, :] * s_ref[kk] + k[kk:kk + 1, :] * v
        snew_ref[kk] = s_new
        o = o + q[kk:kk + 1, :] * s_new
    ms = jnp.mean(o * o, axis=0, keepdims=True)
    o_ref[...] = o * lax.rsqrt(ms + EPS) * og_ref[...] * _silu(g_ref[...])


def _col_params(x):
    return x.reshape(NH, HD, 1)


def _hgrn2_step(pa_t, state_t, layer, log_lb, log1m_lb, onorm_g):
    B = pa_t.shape[1]
    parts = pa_t.reshape(A_W // HD, HD, B)
    part = lambda j: pl.BlockSpec((None, HD, B), lambda h: (j * NH + h, 0, 0))
    col = pl.BlockSpec((None, HD, 1), lambda h: (h, 0, 0))
    return pl.pallas_call(
        _hgrn2_step_kernel,
        grid=(NH,),
        in_specs=[part(0), part(1), part(2), part(3),
                  pl.BlockSpec((None, None, HD, HD, B), lambda h: (layer, h, 0, 0, 0)), col, col, col],
        out_specs=[pl.BlockSpec((HD, B), lambda h: (h, 0)),
                   pl.BlockSpec((None, HD, HD, B), lambda h: (h, 0, 0, 0))],
        out_shape=[jax.ShapeDtypeStruct((D_BR, B), F32),
                   jax.ShapeDtypeStruct((NH, HD, HD, B), F32)],
        compiler_params=_cparams("parallel"),
        name="hgrn2_step",
    )(parts, parts, parts, parts, state_t, _col_params(log_lb), _col_params(log1m_lb), _col_params(onorm_g))


def _rwkv_prep(xs, prev, p):
    xm = xs + (prev - xs) * p["mu"]
    r = xm[:, 0:D_BR]
    k = xm[:, D_BR:2 * D_BR]
    v = xm[:, 2 * D_BR:3 * D_BR]
    xw = xm[:, 3 * D_BR:3 * D_BR + LORA]
    xa = xm[:, 3 * D_BR + LORA:3 * D_BR + 2 * LORA]
    w = -_softplus(-(p["w0"] + _bdot(jnp.tanh(xw), p["w2"]))) - 0.5
    logw = -jnp.exp(w)
    a = _sigmoid(p["a0"] + _bdot(xa, p["a2"]))
    kk = k * p["k_k"]
    kk = kk / jnp.maximum(jnp.sqrt(_seg_sum(kk * kk)), 1e-12)
    k = k * (1.0 + (a - 1.0) * p["k_a"])
    bonus = _seg_sum(r * k * p["r_k"]) * v
    return r, logw, k, v, -kk, kk * a, bonus


def _rwkv_post(y, bonus, g, p):
    mean = _seg_sum(y) * (1.0 / HD)
    yc = y - mean
    var = _seg_sum(yc * yc) * (1.0 / HD)
    yn = yc * lax.rsqrt(var + RWKV_GN_EPS) * p["gn_w"] + p["gn_b"]
    return (yn + bonus) * _silu(g)


def _unit_lower_inverse(m, m_t, n):
    nb = INV_BLOCK
    rows = lax.broadcasted_iota(jnp.int32, (nb, 1), 0)
    nm = len(m)
    per = n // nb
    group = lax.broadcasted_iota(jnp.int32, (nb, n), 1) // nb
    diag_t = []
    for q in range(nm):
        dq = m_t[q][0:nb, :]
        for b in range(1, per):
            dq = jnp.where(group == b, m_t[q][b * nb:(b + 1) * nb, :], dq)
        diag_t.append(dq)
    qpl = LANES // n
    tiles = [jnp.concatenate(diag_t[t * qpl:(t + 1) * qpl], axis=1) for t in range(nm // qpl)]
    lane = lax.broadcasted_iota(jnp.int32, (nb, LANES), 1)
    base = lane - lane % nb
    ds = [jnp.where(lane % nb == rows, 1.0, 0.0).astype(F32)] * len(tiles)
    for i in range(1, nb):
        coef = [jnp.take_along_axis(t, base + i, axis=1) for t in tiles]
        rws = [jnp.sum(cf * d, axis=0, keepdims=True) for cf, d in zip(coef, ds)]
        ds = [d + jnp.where(rows == i, rw, 0.0) for d, rw in zip(ds, rws)]
    blocks = [[ds[q // qpl][:, (q % qpl) * n + b * nb:(q % qpl) * n + (b + 1) * nb] for b in range(per)]
              for q in range(nm)]
    size = nb
    while size < n:
        pairs = [(q, j) for q in range(nm) for j in range(0, len(blocks[q]), 2)]
        inner = [_rw_inv(m[q][(j + 1) * size:(j + 2) * size, j * size:(j + 1) * size], blocks[q][j])
                 for q, j in pairs]
        cross = [_rw_inv(blocks[q][j + 1], t) for (q, j), t in zip(pairs, inner)]
        merged = [[] for _ in range(nm)]
        for (q, j), cr in zip(pairs, cross):
            top = jnp.concatenate([blocks[q][j], jnp.zeros((size, size), F32)], axis=1)
            bot = jnp.concatenate([cr, blocks[q][j + 1]], axis=1)
            merged[q].append(jnp.concatenate([top, bot], axis=0))
        blocks = merged
        size *= 2
    return [b[0] for b in blocks]


_rw_score = _bdot_nt
_rw_score_out = _bdot_nt
_rw_inv = _bdot
_rw_solve = _bdot
_rw_read = _bdot_nt
_rw_read_out = _bdot_nt
_rw_apply = _bdot
_rw_apply_out = _bdot
_rw_update = _bdot_tn
_rw_state = _bdot

_RWKV_PARAMS = ("mu", "w0", "w2", "a0", "a2", "k_k", "k_a", "r_k", "gn_w", "gn_b")


def _rwkv_seq_kernel(pb_ref, shift0_ref, s0_ref, *rest, tb, chunk, attn_rows):
    np_ = len(_RWKV_PARAMS)
    p = {n: r[...] for n, r in zip(_RWKV_PARAMS, rest[:np_])}
    n_ai, n_ao = 8, 3
    attn_in = rest[np_:np_ + n_ai]
    o_ref, sfin_ref, shift_ref = rest[np_ + n_ai:np_ + n_ai + 3]
    attn_out = rest[np_ + n_ai + 3:np_ + n_ai + 3 + n_ao]
    r_s, k_s, v_s, a_s, b_s, w_s, g_s, y_s, bon_s, st_s, carry_s = rest[np_ + n_ai + 3 + n_ao:]
    step = pl.program_id(1)
    _attn_step_kernel(*attn_in, *attn_out, bb=attn_rows)

    @pl.when(step == 0)
    def _():
        st_s[...] = s0_ref[0]
        carry_s[...] = shift0_ref[0]

    xs = pb_ref[0][:, 0:B_SHIFT_W]
    row = lax.broadcasted_iota(jnp.int32, (tb, 1), 0)
    prev = jnp.where(row == 0, carry_s[...], pltpu.roll(xs, 1, 0))
    carry_s[...] = xs[tb - 1:tb, :]
    r, logw, k, v, av, bv, bonus = _rwkv_prep(xs, prev, p)
    gcum = _chunk_cumsum(logw, chunk)
    bon_s[...] = bonus
    for h in range(NH):
        sl = slice(h * HD, (h + 1) * HD)
        r_s[h] = r[:, sl]
        k_s[h] = k[:, sl]
        v_s[h] = v[:, sl]
        a_s[h] = av[:, sl]
        b_s[h] = bv[:, sl]
        w_s[h] = logw[:, sl]
        g_s[h] = gcum[:, sl]

    ri = lax.broadcasted_iota(jnp.int32, (chunk, chunk), 0)
    ci = lax.broadcasted_iota(jnp.int32, (chunk, chunk), 1)
    strict = ci < ri
    incl = ci <= ri

    def run_chunks(chunks):
        units = [(c, h) for c in chunks for h in range(NH)]
        hs = range(len(units))
        ld = lambda ref: [ref[h, c * chunk:(c + 1) * chunk, :] for c, h in units]
        rc, kc, vc, ac, bc, wc, gc = (ld(x) for x in (r_s, k_s, v_s, a_s, b_s, w_s, g_s))
        glast = [g[chunk - 1:chunk, :] for g in gc]
        einv = [jnp.exp(-g) for g in gc]
        etail = [jnp.exp(gl - g) for gl, g in zip(glast, gc)]
        a_t = [a * jnp.exp(g - w) for a, g, w in zip(ac, gc, wc)]
        r_t = [r * jnp.exp(g) for r, g in zip(rc, gc)]
        b_t = [b * e for b, e in zip(bc, einv)]
        k_t = [k * e for k, e in zip(kc, einv)]
        b_h = [b * e for b, e in zip(bc, etail)]
        k_h = [k * e for k, e in zip(kc, etail)]
        m_ab = [jnp.where(strict, _rw_score(a_t[h], b_t[h]), 0.0) for h in hs]
        m_ab_t = [jnp.where(ri < ci, _rw_score(b_t[h], a_t[h]), 0.0) for h in hs]
        m_ak = [jnp.where(strict, _rw_score(a_t[h], k_t[h]), 0.0) for h in hs]
        n_rb = [jnp.where(incl, _rw_score_out(r_t[h], b_t[h]), 0.0) for h in hs]
        n_rk = [jnp.where(incl, _rw_score_out(r_t[h], k_t[h]), 0.0) for h in hs]
        inv = _unit_lower_inverse(m_ab, m_ab_t, chunk)
        x = [_rw_apply(m_ak[h], vc[h]) for h in hs]
        a_hat = [_rw_solve(inv[h], a_t[h]) for h in hs]
        u = [_rw_solve(inv[h], x[h]) for h in hs]
        r_hat = [r_t[h] + _rw_apply_out(n_rb[h], a_hat[h]) for h in hs]
        y0 = [_rw_apply_out(n_rb[h], u[h]) + _rw_apply_out(n_rk[h], vc[h]) for h in hs]
        w_kk = [_rw_update(a_hat[h], b_h[h]) for h in hs]
        z = [_rw_update(u[h], b_h[h]) + _rw_update(vc[h], k_h[h]) for h in hs]
        decay = [jnp.exp(gl) for gl in glast]
        s = [st_s[h] for h in range(NH)]
        for i, (c, h) in enumerate(units):
            y_s[c * chunk:(c + 1) * chunk, h * HD:(h + 1) * HD] = _rw_read_out(r_hat[i], s[h]) + y0[i]
            s[h] = s[h] * decay[i] + _rw_state(s[h], w_kk[i]) + z[i]
        for h in range(NH):
            st_s[h] = s[h]

    n_chunks = tb // chunk
    for c0 in range(0, n_chunks, CHUNKS_PER_PASS):
        run_chunks(range(c0, min(c0 + CHUNKS_PER_PASS, n_chunks)))

    o_ref[0] = _rwkv_post(y_s[...], bon_s[...], pb_ref[0][:, B_SHIFT_W:B_W], p)

    @pl.when(step == pl.num_programs(1) - 1)
    def _():
        sfin_ref[0] = st_s[...]
        shift_ref[0] = carry_s[...]


def _rwkv_param_specs(params, nargs):
    zero2 = (lambda b, i: (0, 0)) if nargs == 2 else (lambda i: (0, 0))
    return [pl.BlockSpec(x.shape, zero2) for x in params]


def _rwkv_seq(pb, shift0, s0, params, attn_args):
    B, T, _ = pb.shape
    tb = min(MIX_TILE, T)
    nt = T // tb
    rows, a_in, a_in_specs, a_out_specs, a_out_shapes = _attn_step_operands(
        *attn_args, n_steps=B * nt, step_of=lambda b, i: b * nt + i)
    st = pl.BlockSpec((1, NH, HD, HD), lambda b, i: (b, 0, 0, 0))
    sh = pl.BlockSpec((1, 1, B_SHIFT_W), lambda b, i: (b, 0, 0))
    res = pl.pallas_call(
        functools.partial(_rwkv_seq_kernel, tb=tb, chunk=CHUNK_B, attn_rows=rows),
        grid=(B, nt),
        in_specs=[pl.BlockSpec((1, tb, B_W), lambda b, i: (b, i, 0)), sh, st]
                 + _rwkv_param_specs(params, 2) + a_in_specs,
        out_specs=[pl.BlockSpec((1, tb, D_BR), lambda b, i: (b, i, 0)), st, sh] + a_out_specs,
        out_shape=[jax.ShapeDtypeStruct((B, T, D_BR), F32),
                   jax.ShapeDtypeStruct((B, NH, HD, HD), F32),
                   jax.ShapeDtypeStruct((B, 1, B_SHIFT_W), F32)] + a_out_shapes,
        scratch_shapes=[pltpu.VMEM((NH, tb, HD), F32)] * 7
                       + [pltpu.VMEM((tb, D_BR), F32), pltpu.VMEM((tb, D_BR), F32),
                          pltpu.VMEM((NH, HD, HD), F32), pltpu.VMEM((1, B_SHIFT_W), F32)],
        compiler_params=_cparams("parallel", "arbitrary"),
        name="rwkv7_seq",
    )(pb, shift0, s0, *params, *a_in)
    return res[0], res[1], res[2], [a.reshape(-1, D_BR) for a in res[3:]]


def _rwkv_step_kernel(r_ref, k_ref, v_ref, lr_ref, g_ref, pr_ref, pk_ref, pv_ref, plr_ref,
                      mr_ref, mk_ref, mv_ref, mlr_ref, w0_ref, w2_ref, a0_ref, a2_ref,
                      kk_ref, ka_ref, rk_ref, gw_ref, gb_ref, s_ref, o_ref, snew_ref, y_s):
    lerp = lambda x, prev, mu: x[...] + (prev[...] - x[...]) * mu[...]
    r = lerp(r_ref, pr_ref, mr_ref)
    k = lerp(k_ref, pk_ref, mk_ref)
    v = lerp(v_ref, pv_ref, mv_ref)
    lr = lerp(lr_ref, plr_ref, mlr_ref)
    w = -_softplus(-(w0_ref[...] + _bdot(w2_ref[...], jnp.tanh(lr[0:LORA])))) - 0.5
    dec = jnp.exp(-jnp.exp(w))
    a = _sigmoid(a0_ref[...] + _bdot(a2_ref[...], lr[LORA:2 * LORA]))
    kk = k * kk_ref[...]
    kk = kk / jnp.maximum(jnp.sqrt(jnp.sum(kk * kk, axis=0, keepdims=True)), 1e-12)
    k = k * (1.0 + (a - 1.0) * ka_ref[...])
    bonus = jnp.sum(r * k * rk_ref[...], axis=0, keepdims=True) * v
    av = -kk
    bv = kk * a
    for i in range(HD):
        s = s_ref[i]
        sa = jnp.sum(s * av, axis=0, keepdims=True)
        s_new = s * dec + sa * bv + v[i:i + 1, :] * k
        snew_ref[i] = s_new
        y_s[i:i + 1, :] = jnp.sum(s_new * r, axis=0, keepdims=True)
    y = y_s[...]
    mean = jnp.mean(y, axis=0, keepdims=True)
    yc = y - mean
    var = jnp.mean(yc * yc, axis=0, keepdims=True)
    yn = yc * lax.rsqrt(var + RWKV_GN_EPS) * gw_ref[...] + gb_ref[...]
    o_ref[...] = (yn + bonus) * _silu(g_ref[...])


def _rwkv_step(pb_t, shift_t, state_t, layer, params):
    B = pb_t.shape[1]
    p = dict(zip(_RWKV_PARAMS, params))
    parts = pb_t.reshape(B_W // HD, HD, B)
    prev = shift_t.reshape(DEPTH, B_SHIFT_W // HD, HD, B)
    mu = p["mu"].reshape(B_SHIFT_W // HD, HD, 1)
    lr_row = 3 * NH
    part = lambda j: pl.BlockSpec((None, HD, B), lambda h: (j * NH + h, 0, 0))
    pprev = lambda j: pl.BlockSpec((None, None, HD, B), lambda h: (layer, j * NH + h, 0, 0))
    pmu = lambda j: pl.BlockSpec((None, HD, 1), lambda h: (j * NH + h, 0, 0))
    col = pl.BlockSpec((None, HD, 1), lambda h: (h, 0, 0))
    lora = pl.BlockSpec((HD, LORA), lambda h: (h, 0))
    cols = [_col_params(p[n]) for n in ("k_k", "k_a", "r_k", "gn_w", "gn_b")]
    return pl.pallas_call(
        _rwkv_step_kernel,
        grid=(NH,),
        in_specs=[part(0), part(1), part(2),
                  pl.BlockSpec((None, HD, B), lambda h: (lr_row, 0, 0)),
                  pl.BlockSpec((None, HD, B), lambda h: (lr_row + 1 + h, 0, 0)),
                  pprev(0), pprev(1), pprev(2),
                  pl.BlockSpec((None, None, HD, B), lambda h: (layer, lr_row, 0, 0)),
                  pmu(0), pmu(1), pmu(2),
                  pl.BlockSpec((None, HD, 1), lambda h: (lr_row, 0, 0)),
                  col, lora, col, lora, col, col, col, col, col,
                  pl.BlockSpec((None, None, HD, HD, B), lambda h: (layer, h, 0, 0, 0))],
        out_specs=[pl.BlockSpec((HD, B), lambda h: (h, 0)),
                   pl.BlockSpec((None, HD, HD, B), lambda h: (h, 0, 0, 0))],
        out_shape=[jax.ShapeDtypeStruct((D_BR, B), F32),
                   jax.ShapeDtypeStruct((NH, HD, HD, B), F32)],
        scratch_shapes=[pltpu.VMEM((HD, B), F32)],
        compiler_params=_cparams("parallel"),
        name="rwkv7_step",
    )(parts, parts, parts, parts, parts, prev, prev, prev, prev, mu, mu, mu, mu,
      _col_params(p["w0"]), p["w2"].T, _col_params(p["a0"]), p["a2"].T, *cols, state_t)


def _qkv_prep_kernel(pc_ref, cos_ref, sin_ref, qg_ref, kg_ref, *rest, tm):
    nd = len(C_CONFIGS)
    outs = rest[:3 * nd]
    kc_ref = rest[3 * nd]
    scr = rest[3 * nd + 1:]
    pc = pc_ref[0]
    cos = cos_ref[...]
    sin = sin_ref[...]
    q = _rope(_head_rms(pc[:, 0:D_BR], qg_ref[...]), cos, sin) * (HD ** -0.5)
    k = _rope(_head_rms(pc[:, D_BR:2 * D_BR], kg_ref[...]), cos, sin)
    kc_ref[0] = k
    for a, x in enumerate((q, k, pc[:, 2 * D_BR:3 * D_BR])):
        for half in range(D_BR // LANES):
            scr[a][half] = x[:, half * LANES:(half + 1) * LANES]
    hpl = LANES // HD
    for ci, (_, dil) in enumerate(C_CONFIGS):
        for a in range(3):
            out = outs[3 * ci + a]
            for rho in range(dil):
                for half in range(D_BR // LANES):
                    rows = scr[a][half, pl.ds(rho, tm // dil, stride=dil), :]
                    for j in range(hpl):
                        out[0, rho, half * hpl + j] = rows[:, j * HD:(j + 1) * HD].astype(BF16)


def _swa_kernel(q_ref, kc_ref, vc_ref, kp_ref, vp_ref, o_ref, lse_ref, *, qt):
    i = pl.program_id(1)
    n = Q_BLOCK
    r = lax.broadcasted_iota(jnp.int32, (n, 2 * n), 0)
    c = lax.broadcasted_iota(jnp.int32, (n, 2 * n), 1)
    band = (c >= r) & (c <= r + n)
    first = band & (c >= jnp.where(i > 0, 0, n))
    units = [(u, h) for u in range(qt // n) for h in range(NH)]

    def window(cur_ref, prev_ref, u, h):
        if u == 0:
            return jnp.concatenate([prev_ref[0, h], cur_ref[0, h, 0:n, :]], axis=0)
        return cur_ref[0, h, (u - 1) * n:(u + 1) * n, :]

    s = [lax.dot_general(q_ref[0, h, u * n:(u + 1) * n, :], window(kc_ref, kp_ref, u, h),
                         (((1,), (1,)), ((), ())), preferred_element_type=F32) for u, h in units]
    s = [jnp.where(first if u == 0 else band, x, MASK_VALUE) for (u, h), x in zip(units, s)]
    m = [jnp.max(x, axis=1, keepdims=True) for x in s]
    p = [jnp.exp(x - mx) for x, mx in zip(s, m)]
    l = [jnp.sum(x, axis=1, keepdims=True) for x in p]
    o = [jnp.dot(x.astype(BF16), window(vc_ref, vp_ref, u, h), preferred_element_type=F32) / lx
         for (u, h), x, lx in zip(units, p, l)]
    lse = [jnp.broadcast_to(mx + jnp.log(lx), (n, HD)) for mx, lx in zip(m, l)]
    for u in range(qt // n):
        o_ref[0, u * n:(u + 1) * n, :] = jnp.concatenate(o[u * NH:(u + 1) * NH], axis=1)
        lse_ref[0, u * n:(u + 1) * n, :] = jnp.concatenate(lse[u * NH:(u + 1) * NH], axis=1)


def _swa(q, k, v):
    N, _, L, _ = q.shape
    n = Q_BLOCK
    qt = min(ATTN_TILE, L)
    cur = pl.BlockSpec((1, NH, qt, HD), lambda s, i: (s, 0, i, 0))
    prev = pl.BlockSpec((1, NH, n, HD), lambda s, i: (s, 0, jnp.maximum(i * (qt // n) - 1, 0), 0))
    out = pl.BlockSpec((1, qt, D_BR), lambda s, i: (s, i, 0))
    return pl.pallas_call(
        functools.partial(_swa_kernel, qt=qt),
        grid=(N, L // qt),
        in_specs=[cur, cur, cur, prev, prev],
        out_specs=[out, out],
        out_shape=[jax.ShapeDtypeStruct((N, L, D_BR), F32)] * 2,
        compiler_params=_cparams("parallel", "parallel"),
        name="sliding_window_attn",
    )(q, k, v, k, v)


def _attn_combine_kernel(*refs, tm):
    nd = len(C_CONFIGS)
    o_refs, lse_refs = refs[0:nd], refs[nd:2 * nd]
    g_ref, out_ref = refs[2 * nd], refs[2 * nd + 1]
    o_s, lse_s = refs[2 * nd + 2], refs[2 * nd + 3]
    os_, lses = [], []
    for ci, (_, dil) in enumerate(C_CONFIGS):
        if dil == 1:
            os_.append(o_refs[ci][0, 0])
            lses.append(lse_refs[ci][0, 0])
            continue
        nh = D_BR // LANES
        for rho in range(dil):
            for half in range(nh):
                ls = slice(half * LANES, (half + 1) * LANES)
                o_s[ci, half, pl.ds(rho, tm // dil, stride=dil), :] = o_refs[ci][0, rho][:, ls]
                lse_s[ci, half, pl.ds(rho, tm // dil, stride=dil), :] = lse_refs[ci][0, rho][:, ls]
        os_.append(jnp.concatenate([o_s[ci, half] for half in range(nh)], axis=1))
        lses.append(jnp.concatenate([lse_s[ci, half] for half in range(nh)], axis=1))
    top = functools.reduce(jnp.maximum, lses)
    ws = [jnp.exp(x - top) for x in lses]
    num = functools.reduce(lambda a, b: a + b, [w * o for w, o in zip(ws, os_)])
    den = functools.reduce(lambda a, b: a + b, ws)
    out_ref[0] = num / den * _silu(g_ref[0])


def _attn_seq(prep, B, T):
    assert T % (Q_BLOCK * max(d for _, d in C_CONFIGS)) == 0
    outs, lses = [], []
    for ci, (w, dil) in enumerate(C_CONFIGS):
        assert w == Q_BLOCK * dil
        q, k, v = (x.reshape(B * dil, NH, T // dil, HD) for x in prep[3 * ci:3 * ci + 3])
        o, lse = _swa(q, k, v)
        outs.append(o.reshape(B, dil, T // dil, D_BR))
        lses.append(lse.reshape(B, dil, T // dil, D_BR))
    return outs, lses


def _attn_step_kernel(pc_ref, cos_ref, sin_ref, qg_ref, kg_ref, mult_ref, kt_ref, vt_ref,
                      o_ref, knew_ref, vnew_ref, *, bb):
    pc = pc_ref[...]
    cos = cos_ref[...]
    sin = sin_ref[...]
    q = _rope(_head_rms(pc[:, 0:D_BR], qg_ref[...]), cos, sin) * (HD ** -0.5)
    k = _rope(_head_rms(pc[:, D_BR:2 * D_BR], kg_ref[...]), cos, sin)
    v = pc[:, 2 * D_BR:3 * D_BR]
    knew_ref[...] = k
    vnew_ref[...] = v
    mult = mult_ref[...]
    live = mult > 0.0
    n_self = float(len(C_CONFIGS))
    eye = _eye(HD)
    for b in range(bb):
        parts = []
        for h in range(NH):
            sl = slice(h * HD, (h + 1) * HD)
            qr, kr, vr = q[b:b + 1, sl], k[b:b + 1, sl], v[b:b + 1, sl]
            s_self = jnp.sum(qr * kr, axis=1, keepdims=True)
            s = jnp.sum(kt_ref[b, h] * _to_col(qr, eye), axis=0, keepdims=True)
            s = jnp.where(live, s, MASK_VALUE)
            m = jnp.maximum(jnp.max(s, axis=1, keepdims=True), s_self)
            p = jnp.exp(s - m) * mult
            p_self = n_self * jnp.exp(s_self - m)
            den = jnp.sum(p, axis=1, keepdims=True) + p_self
            acc = _to_row(jnp.sum(vt_ref[b, h] * p, axis=1, keepdims=True), eye) + p_self * vr
            parts.append(acc / den)
        o_ref[b:b + 1, :] = jnp.concatenate(parts, axis=1)
    o_ref[...] = o_ref[...] * _silu(pc[:, 3 * D_BR:4 * D_BR])


def _attn_step_operands(pc, cos, sin, qg, kg, cache_kt, cache_vt, layer, n_steps, step_of):
    B = pc.shape[0]
    P = cache_kt.shape[-1]
    bb = B // n_steps
    assert bb * n_steps == B
    dist = P - jnp.arange(P)
    mult = jnp.zeros((P,), F32)
    for w, dil in C_CONFIGS:
        mult = mult + ((dist <= w) & (dist % dil == 0)).astype(F32)
    fixed = lambda shape: pl.BlockSpec(shape, lambda *g: (0,) * len(shape))
    row = pl.BlockSpec((None, bb, D_BR), lambda *g: (step_of(*g), 0, 0))
    cache = pl.BlockSpec((None, bb, NH, HD, P), lambda *g: (layer, step_of(*g), 0, 0, 0))
    inputs = (pc.reshape(n_steps, bb, C_W), cos, sin, qg, kg, mult.reshape(1, P), cache_kt, cache_vt)
    in_specs = [pl.BlockSpec((None, bb, C_W), lambda *g: (step_of(*g), 0, 0)),
                fixed((1, D_BR)), fixed((1, D_BR)), fixed((1, D_BR)), fixed((1, D_BR)), fixed((1, P)),
                cache, cache]
    out_shapes = [jax.ShapeDtypeStruct((n_steps, bb, D_BR), F32)] * 3
    return bb, inputs, in_specs, [row, row, row], out_shapes


def _pool_mix(pooled, u, g, w_ref, scale_ref):
    mixed = _bdot(pooled - u, w_ref[...]) * scale_ref[...]
    return mixed * _silu(g)


def _pool_select(sums, cnts):
    lane = lax.broadcasted_iota(jnp.int32, sums[0].shape, 1) // (D_BR // len(POOL_WINDOWS))
    out = sums[-1] / cnts[-1]
    for gi in range(len(POOL_WINDOWS) - 2, -1, -1):
        out = jnp.where(lane == gi, sums[gi] / cnts[gi], out)
    return out


def _pool_seq_kernel(pd_ref, buf0_ref, w_ref, scale_ref, o_ref, carry_s, *, tb, t0):
    step = pl.program_id(1)

    @pl.when(step == 0)
    def _():
        carry_s[...] = buf0_ref[0]

    pd = pd_ref[0]
    u = pd[:, 0:D_BR]
    ext = jnp.concatenate([carry_s[...], u], axis=0)
    carry_s[...] = u[tb - POOL_MAX:tb, :]
    pos = (t0 + step * tb + lax.broadcasted_iota(jnp.int32, (tb, 1), 0)).astype(F32)
    sums, cnts = [], []
    run = ext
    shift = 1
    for w in POOL_WINDOWS:
        while shift < w:
            run = run + pltpu.roll(run, shift, 0)
            shift *= 2
        sums.append(run[POOL_MAX:, :])
        cnts.append(jnp.minimum(float(w), pos + 1.0))
    o_ref[0] = _pool_mix(_pool_select(sums, cnts), u, pd[:, D_BR:2 * D_BR], w_ref, scale_ref)


def _pool_step_kernel(pd_ref, buf_ref, w_ref, scale_ref, o_ref, *, t0):
    pd = pd_ref[...]
    u = pd[:, 0:D_BR]
    nbuf = POOL_MAX - 1
    sums, cnts = [], []
    run = u
    taken = 1
    for w in POOL_WINDOWS:
        while taken < w:
            run = run + buf_ref[nbuf - taken]
            taken += 1
        sums.append(run)
        cnts.append(float(min(w, t0 + 1)))
    o_ref[...] = _pool_mix(_pool_select(sums, cnts), u, pd[:, D_BR:2 * D_BR], w_ref, scale_ref)


def _pool_step(pd, buf, layer, w_bd, scale, t0):
    B = pd.shape[0]
    bb = min(POOL_STEP_ROWS, B)
    return pl.pallas_call(
        functools.partial(_pool_step_kernel, t0=t0),
        grid=(B // bb,),
        in_specs=[pl.BlockSpec((bb, D_W), lambda i: (i, 0)),
                  pl.BlockSpec((None, POOL_MAX - 1, bb, D_BR), lambda i: (layer, 0, i, 0)),
                  pl.BlockSpec((D_BR, D_BR), lambda i: (0, 0)),
                  pl.BlockSpec((1, D_BR), lambda i: (0, 0))],
        out_specs=pl.BlockSpec((bb, D_BR), lambda i: (i, 0)),
        out_shape=jax.ShapeDtypeStruct((B, D_BR), F32),
        compiler_params=_cparams("parallel"),
        name="pool_step",
    )(pd, buf, w_bd, scale)


def _rope_tables(pos):
    half = HD // 2
    inv_freq = jnp.float32(ROPE_THETA) ** (-jnp.arange(half, dtype=F32) / half)
    ang = pos.astype(F32)[:, None] * inv_freq[None, :]
    cos = jnp.cos(ang)
    sin = jnp.sin(ang)
    cos_t = jnp.tile(jnp.concatenate([cos, cos], axis=-1), (1, NH))
    sin_t = jnp.tile(jnp.concatenate([-sin, sin], axis=-1), (1, NH))
    return cos_t, sin_t


def _block_diag(w):
    g, c, d = w.shape
    out = jnp.zeros((g * c, g * d), w.dtype)
    for i in range(g):
        out = out.at[i * c:(i + 1) * c, i * d:(i + 1) * d].set(w[i])
    return out


def _layer_params(l, W):
    row = lambda x: x[l].reshape(1, -1)
    w_in = W["w_in"][l].astype(BF16)
    w_in_t = jnp.swapaxes(W["w_in"], 1, 2)[l].astype(BF16)
    cuts = (0, A_W, A_W + B_W, A_W + B_W + C_W, A_W + B_W + C_W + D_W)
    return dict(
        norm_g=W["norm_g"][l],
        w_parts=[w_in[:, cuts[i]:cuts[i + 1]] for i in range(4)],
        w_parts_t=[w_in_t[cuts[i]:cuts[i + 1], :] for i in range(2)],
        w_out=W["w_out"][l].astype(BF16),
        onorm_g=row(W["a_onorm_g"]),
        rwkv=[row(W["b_mu"]), row(W["b_w0"]), W["b_w2"][l], row(W["b_a0"]), W["b_a2"][l],
              row(W["b_k_k"]), row(W["b_k_a"]), row(W["b_r_k"]), row(W["b_gn_w"]), row(W["b_gn_b"])],
        qg=jnp.tile(W["c_qnorm_g"][l], NH).reshape(1, D_BR),
        kg=jnp.tile(W["c_knorm_g"][l], NH).reshape(1, D_BR),
        w_pool=_block_diag(W["d_w_pool"][l]),
        p_scale=row(W["d_scale"]),
    )


def _run_groups(x, xs, mods, mods_s, layers, log_lb, log1m_lb, sA, sBw, sBs, ck, cv, cd):
    B, T, _ = x.shape
    Bs = xs.shape[0]
    assert xs.shape[1] == 1 and ck.shape[2] == PAST_LEN == C_WIN_MAX
    cos, sin = _rope_tables(jnp.arange(T))
    cos_s, sin_s = _rope_tables(jnp.full((1,), PAST_LEN))
    c_keep = min(C_WIN_MAX, T)
    d_keep = min(POOL_MAX - 1, T)
    zero_state = jnp.zeros((B, NH, HD, HD), F32)
    zero_shift = jnp.zeros((B, 1, B_SHIFT_W), F32)
    zero_pool = jnp.zeros((B, POOL_MAX, D_BR), F32)
    ck_t = jnp.transpose(ck, (0, 1, 3, 4, 2))
    cv_t = jnp.transpose(cv, (0, 1, 3, 4, 2))
    sA_t = jnp.transpose(sA, (0, 2, 3, 4, 1))
    sBw_t = jnp.transpose(sBw, (0, 2, 3, 4, 1))
    sBs_t = jnp.transpose(sBs, (0, 2, 1))
    cd_t = jnp.transpose(cd, (0, 2, 1, 3))
    xr = xs.reshape(Bs, D_MODEL)
    pA, pBw, pBs, pK, pV, pD = [], [], [], [], [], []
    nA, nBw, nBs, nK, nV, nD = [], [], [], [], [], []
    for l, L in enumerate(layers):
        shift, scale, gate_s = (mods_s[l][:, i * D_MODEL:(i + 1) * D_MODEL] for i in range(3))
        pa_t, pb_t, pc_s, pd_s = _in_proj_step(xr, shift, scale, L["norm_g"], *L["w_parts_t"], *L["w_parts"][2:])
        shift, scale, gate = (mods[l][:, None, i * D_MODEL:(i + 1) * D_MODEL] for i in range(3))
        (pa, pb, pc, pd), prep = _in_proj(x, shift, scale, L["norm_g"], L["w_parts"],
                                          cos, sin, L["qg"], L["kg"])
        k_rot = prep[-1]
        oa, sa_t = _hgrn2_seq(pa, zero_state, log_lb[l], log1m_lb[l], L["onorm_g"])
        ob, sb, sbs, (oc_s, k_new, v_new) = _rwkv_seq(
            pb, zero_shift, zero_state, L["rwkv"], (pc_s, cos_s, sin_s, L["qg"], L["kg"], ck_t, cv_t, l))
        attn_outs, attn_lses = _attn_seq(prep, B, T)
        x = _out_proj(oa, ob, attn_outs, attn_lses, pc, pd, zero_pool, L["w_pool"], L["p_scale"], 0,
                      x, gate, L["w_out"])
        pA.append(jnp.swapaxes(sa_t, -1, -2))
        pBw.append(sb)
        pBs.append(sbs[:, 0])
        pK.append(k_rot[:, T - c_keep:].reshape(B, c_keep, NH, HD))
        pV.append(pc[:, T - c_keep:, 2 * D_BR:3 * D_BR].reshape(B, c_keep, NH, HD))
        pD.append(pd[:, T - d_keep:, 0:D_BR])
        oa_t, sa = _hgrn2_step(pa_t, sA_t, l, log_lb[l], log1m_lb[l], L["onorm_g"])
        ob_t, sbw = _rwkv_step(pb_t, sBs_t, sBw_t, l, L["rwkv"])
        od_s = _pool_step(pd_s, cd_t, l, L["w_pool"], L["p_scale"], PAST_LEN)
        xr = _out_proj_step(oa_t, ob_t, oc_s, od_s, xr, gate_s, L["w_out"])
        nA.append(sa)
        nBw.append(sbw)
        nBs.append(pb_t[0:B_SHIFT_W])
        nK.append(k_new.reshape(Bs, 1, NH, HD))
        nV.append(v_new.reshape(Bs, 1, NH, HD))
        nD.append(pd_s[:, None, 0:D_BR])
    back = lambda a: jnp.transpose(jnp.stack(a), (0, 4, 1, 2, 3))
    prompt = (x,) + tuple(jnp.stack(a) for a in (pA, pBw, pBs, pK, pV, pD))
    sample = (xr.reshape(Bs, 1, D_MODEL), back(nA), back(nBw), jnp.transpose(jnp.stack(nBs), (0, 2, 1)),
              jnp.stack(nK), jnp.stack(nV), jnp.stack(nD))
    return prompt, sample


def kernel(x_prompt, x_sample, state_A, state_B_wkv, state_B_shift, cache_C_k, cache_C_v, cache_D_pool, c_prompt, c_sample, ada_w, ada_b, norm_g, w_in, w_out, a_lb_logits, a_onorm_g, b_mu, b_w0, b_w2, b_a0, b_a2, b_k_k, b_k_a, b_r_k, b_gn_w, b_gn_b, c_qnorm_g, c_knorm_g, d_w_pool, d_scale):
    W = dict(norm_g=norm_g, w_in=w_in, w_out=w_out, a_onorm_g=a_onorm_g, b_mu=b_mu, b_w0=b_w0,
             b_w2=b_w2, b_a0=b_a0, b_a2=b_a2, b_k_k=b_k_k, b_k_a=b_k_a, b_r_k=b_r_k,
             b_gn_w=b_gn_w, b_gn_b=b_gn_b, c_qnorm_g=c_qnorm_g, c_knorm_g=c_knorm_g,
             d_w_pool=d_w_pool, d_scale=d_scale)
    layers = [_layer_params(l, W) for l in range(DEPTH)]
    lb_sm = jax.nn.softmax(a_lb_logits.astype(F32), axis=0)
    lower = jnp.cumsum(lb_sm, axis=0) - lb_sm[0:1]
    log_lb = jnp.log(jnp.maximum(lower, LB_FLOOR))[:, None, :]
    log1m_lb = jnp.log1p(-lower)[:, None, :]

    nb = x_prompt.shape[0]
    mods = _modulation(jnp.concatenate([c_prompt, c_sample], axis=0), ada_w, ada_b)
    (y_p, pA, pBw, pBs, pK, pV, pD), (y_s, sA, sBw, sBs, sK, sV, sD) = _run_groups(
        x_prompt, x_sample, mods[:, :nb], mods[:, nb:], layers, log_lb, log1m_lb,
        state_A, state_B_wkv, state_B_shift, cache_C_k, cache_C_v, cache_D_pool)
    return (y_p, y_s, pA, sA, pBw, sBw, pBs, sBs, pK, sK, pV, sV, pD, sD)
```

```python
import functools

import jax
import jax.numpy as jnp
from jax import lax
from jax.experimental import pallas as pl
from jax.experimental.pallas import tpu as pltpu

F32 = jnp.float32
BF16 = jnp.bfloat16

D_MODEL = 1024
DEPTH = 4
PAST_LEN = 2048
D_BR = 256
HD = 64
NH = D_BR // HD
LB_FLOOR = 1e-30
LORA = 32
RWKV_GN_EPS = 64e-5
C_CONFIGS = ((128, 1), (512, 4), (2048, 16))
C_WIN_MAX = 2048
MASK_VALUE = -1e30
ROPE_THETA = 10000.0
POOL_WINDOWS = (2, 4, 8, 16)
POOL_MAX = 16
EPS = 1e-6
A_W = 4 * D_BR
B_SHIFT_W = 3 * D_BR + 2 * LORA
B_W = B_SHIFT_W + D_BR
C_W = 4 * D_BR
D_W = 2 * D_BR

LANES = 128

ROW_TILE = 512
MIX_TILE = 256
CHUNK_A = 8
CHUNKS_PER_IT_A = 16
CHUNK_B = 64
INV_BLOCK = 16
CHUNKS_PER_PASS = 4
Q_BLOCK = 128
ATTN_TILE = 512
POOL_STEP_ROWS = 32
VMEM_LIMIT = 56 * 1024 * 1024


def _cparams(*sem):
    return pltpu.CompilerParams(dimension_semantics=sem, vmem_limit_bytes=VMEM_LIMIT)


def _sigmoid(x):
    return 1.0 / (1.0 + jnp.exp(-x))


def _silu(x):
    return x * _sigmoid(x)


def _softplus(x):
    return jnp.maximum(x, 0.0) + jnp.log1p(jnp.exp(-jnp.abs(x)))


def _log_sigmoid_pair(x):
    t = jnp.log1p(jnp.exp(-jnp.abs(x)))
    return -(jnp.maximum(-x, 0.0) + t), -(jnp.maximum(x, 0.0) + t)


def _logaddexp(a, b):
    return jnp.maximum(a, b) + jnp.log1p(jnp.exp(-jnp.abs(a - b)))


def _bdot(a, b):
    return jnp.dot(a.astype(BF16), b.astype(BF16), preferred_element_type=F32)


def _bdot_nt(a, b):
    return lax.dot_general(a.astype(BF16), b.astype(BF16), (((1,), (1,)), ((), ())),
                           preferred_element_type=F32)


def _bdot_tn(a, b):
    return lax.dot_general(a.astype(BF16), b.astype(BF16), (((0,), (0,)), ((), ())),
                           preferred_element_type=F32)


def _split3(x):
    h1 = x.astype(BF16)
    r1 = x - h1.astype(F32)
    h2 = r1.astype(BF16)
    r2 = r1 - h2.astype(F32)
    return h1, h2, r2.astype(BF16)


def _seg_sum(x):
    n = x.shape[-1]
    r = lax.broadcasted_iota(jnp.int32, (n, n), 0) // HD
    c = lax.broadcasted_iota(jnp.int32, (n, n), 1) // HD
    ones = jnp.where(r == c, 1.0, 0.0).astype(BF16)
    h1, h2, h3 = _split3(x)
    d = lambda h: jnp.dot(h, ones, preferred_element_type=F32)
    return d(h1) + d(h2) + d(h3)


def _chunk_cumsum(x, chunk):
    t = x.shape[0]
    r = lax.broadcasted_iota(jnp.int32, (t, t), 0)
    c = lax.broadcasted_iota(jnp.int32, (t, t), 1)
    tri = jnp.where((r // chunk == c // chunk) & (c <= r), 1.0, 0.0).astype(BF16)
    h1, h2, h3 = _split3(x)
    d = lambda h: jnp.dot(tri, h, preferred_element_type=F32)
    return d(h1) + d(h2) + d(h3)


def _head_rms(x, gamma):
    ms = _seg_sum(x * x) * (1.0 / HD)
    return x * lax.rsqrt(ms + EPS) * gamma


def _rope(x, cos, sin_signed):
    lane = lax.broadcasted_iota(jnp.int32, x.shape, 1) % HD
    partner = jnp.where(lane < HD // 2,
                        pltpu.roll(x, D_BR - HD // 2, 1),
                        pltpu.roll(x, HD // 2, 1))
    return x * cos + partner * sin_signed


def _eye(n):
    r = lax.broadcasted_iota(jnp.int32, (n, n), 0)
    c = lax.broadcasted_iota(jnp.int32, (n, n), 1)
    return jnp.where(r == c, 1.0, 0.0).astype(F32)


def _to_col(row, eye):
    return jnp.sum(eye * row, axis=1, keepdims=True)


def _to_row(col, eye):
    return jnp.sum(eye * col, axis=0, keepdims=True)


def _mod_kernel(c_ref, w_ref, b_ref, o_ref):
    c = _silu(c_ref[...])
    o_ref[0] = _bdot(c, w_ref[0]) + b_ref[0]


def _modulation(c_all, ada_w, ada_b):
    n = c_all.shape[0]
    tn = 512
    return pl.pallas_call(
        _mod_kernel,
        grid=(DEPTH, 3 * D_MODEL // tn),
        in_specs=[pl.BlockSpec((n, D_MODEL), lambda l, j: (0, 0)),
                  pl.BlockSpec((1, D_MODEL, tn), lambda l, j: (l, 0, j)),
                  pl.BlockSpec((1, 1, tn), lambda l, j: (l, 0, j))],
        out_specs=pl.BlockSpec((1, n, tn), lambda l, j: (l, 0, j)),
        out_shape=jax.ShapeDtypeStruct((DEPTH, n, 3 * D_MODEL), F32),
        compiler_params=_cparams("parallel", "parallel"),
        name="adaln_modulation",
    )(c_all, ada_w, ada_b.reshape(DEPTH, 1, 3 * D_MODEL))


def _in_proj_kernel(x_ref, shift_ref, scale_ref, g_ref, wa_ref, wb_ref, wc_ref, wd_ref,
                    cos_ref, sin_ref, qg_ref, kg_ref, pa_ref, pb_ref, pc_ref, pd_ref, *attn_prep, tm):
    x = x_ref[0]
    ms = jnp.mean(x * x, axis=-1, keepdims=True)
    h = x * lax.rsqrt(ms + EPS) * g_ref[...]
    h = (h * (1.0 + scale_ref[0]) + shift_ref[0]).astype(BF16)
    pc_ref[0] = jnp.dot(h, wc_ref[...], preferred_element_type=F32)
    pa_ref[0] = jnp.dot(h, wa_ref[...], preferred_element_type=F32)
    pb_ref[0] = jnp.dot(h, wb_ref[...], preferred_element_type=F32)
    pd_ref[0] = jnp.dot(h, wd_ref[...], preferred_element_type=F32)
    _qkv_prep_kernel(pc_ref, cos_ref, sin_ref, qg_ref, kg_ref, *attn_prep, tm=tm)


def _in_proj(x, shift, scale, norm_g, w_parts, cos, sin, qg, kg):
    B, T, D = x.shape
    tm = min(ROW_TILE, T)
    mod = pl.BlockSpec((1, 1, D), lambda b, i: (b, 0, 0))
    vec = pl.BlockSpec((1, D_BR), lambda b, i: (0, 0))
    tab = pl.BlockSpec((tm, D_BR), lambda b, i: (i, 0))
    widths = [w.shape[1] for w in w_parts]
    out_specs = [pl.BlockSpec((1, tm, w), lambda b, i: (b, i, 0)) for w in widths]
    out_shape = [jax.ShapeDtypeStruct((B, T, w), F32) for w in widths]
    for _, dil in C_CONFIGS:
        for _ in range(3):
            out_specs.append(pl.BlockSpec((1, dil, NH, tm // dil, HD), lambda b, i: (b, 0, 0, i, 0)))
            out_shape.append(jax.ShapeDtypeStruct((B, dil, NH, T // dil, HD), BF16))
    out_specs.append(pl.BlockSpec((1, tm, D_BR), lambda b, i: (b, i, 0)))
    out_shape.append(jax.ShapeDtypeStruct((B, T, D_BR), F32))
    res = pl.pallas_call(
        functools.partial(_in_proj_kernel, tm=tm),
        grid=(B, T // tm),
        in_specs=[pl.BlockSpec((1, tm, D), lambda b, i: (b, i, 0)), mod, mod,
                  pl.BlockSpec((1, D), lambda b, i: (0, 0))]
                 + [pl.BlockSpec((D, w), lambda b, i: (0, 0), pipeline_mode=pl.Buffered(1)) for w in widths]
                 + [tab, tab, vec, vec],
        out_specs=out_specs,
        out_shape=out_shape,
        scratch_shapes=[pltpu.VMEM((D_BR // LANES, tm, LANES), F32)] * 3,
        compiler_params=_cparams("parallel", "parallel"),
        name="norm_in_proj",
    )(x, shift, scale, norm_g.reshape(1, D), *w_parts, cos, sin, qg, kg)
    return res[:4], res[4:]


def _in_proj_step_kernel(x_ref, shift_ref, scale_ref, g_ref, wat_ref, wbt_ref, wc_ref, wd_ref,
                         pat_ref, pbt_ref, pc_ref, pd_ref):
    x = x_ref[...]
    ms = jnp.mean(x * x, axis=-1, keepdims=True)
    h = x * lax.rsqrt(ms + EPS) * g_ref[...]
    h = (h * (1.0 + scale_ref[...]) + shift_ref[...]).astype(BF16)
    nt = lambda w: lax.dot_general(w, h, (((1,), (1,)), ((), ())), preferred_element_type=F32)
    pat_ref[...] = nt(wat_ref[...])
    pbt_ref[...] = nt(wbt_ref[...])
    pc_ref[...] = jnp.dot(h, wc_ref[...], preferred_element_type=F32)
    pd_ref[...] = jnp.dot(h, wd_ref[...], preferred_element_type=F32)


def _in_proj_step(x, shift, scale, norm_g, wa_t, wb_t, wc, wd):
    B, D = x.shape
    ins = (x, shift, scale, norm_g.reshape(1, D), wa_t, wb_t, wc, wd)
    shapes = [(wa_t.shape[0], B), (wb_t.shape[0], B), (B, wc.shape[1]), (B, wd.shape[1])]
    return pl.pallas_call(
        _in_proj_step_kernel,
        grid=(1,),
        in_specs=[pl.BlockSpec(a.shape, lambda i: (0, 0)) for a in ins],
        out_specs=[pl.BlockSpec(s, lambda i: (0, 0)) for s in shapes],
        out_shape=[jax.ShapeDtypeStruct(s, F32) for s in shapes],
        compiler_params=_cparams("arbitrary"),
        name="norm_in_proj_step",
    )(*ins)


def _out_proj_step_kernel(oat_ref, obt_ref, oc_ref, od_ref, x_ref, gate_ref, w_ref, y_ref):
    tn = lambda a, w: lax.dot_general(a.astype(BF16), w, (((0,), (0,)), ((), ())), preferred_element_type=F32)
    nn = lambda a, w: jnp.dot(a.astype(BF16), w, preferred_element_type=F32)
    res = (tn(oat_ref[...], w_ref[0:D_BR, :]) + tn(obt_ref[...], w_ref[D_BR:2 * D_BR, :])
           + nn(oc_ref[...], w_ref[2 * D_BR:3 * D_BR, :]) + nn(od_ref[...], w_ref[3 * D_BR:4 * D_BR, :]))
    y_ref[...] = x_ref[...] + gate_ref[...] * res


def _out_proj_step(oa_t, ob_t, oc, od, x, gate, w_out):
    ins = (oa_t, ob_t, oc, od, x, gate, w_out)
    return pl.pallas_call(
        _out_proj_step_kernel,
        grid=(1,),
        in_specs=[pl.BlockSpec(a.shape, lambda i: (0, 0)) for a in ins],
        out_specs=pl.BlockSpec(x.shape, lambda i: (0, 0)),
        out_shape=jax.ShapeDtypeStruct(x.shape, F32),
        compiler_params=_cparams("arbitrary"),
        name="out_proj_residual_step",
    )(*ins)


def _out_proj_kernel(oa_ref, ob_ref, *rest, tm, t0):
    nd = len(C_CONFIGS)
    attn_in = rest[:2 * nd + 1]
    pd_ref, buf0_ref, wp_ref, ps_ref, x_ref, gate_ref, w_ref, y_ref = rest[2 * nd + 1:2 * nd + 9]
    oc_s, od_s, o_s, lse_s, carry_s = rest[2 * nd + 9:]
    _attn_combine_kernel(*attn_in, oc_s, o_s, lse_s, tm=tm)
    _pool_seq_kernel(pd_ref, buf0_ref, wp_ref, ps_ref, od_s, carry_s, tb=tm, t0=t0)
    mix = jnp.concatenate([oa_ref[0], ob_ref[0], oc_s[0], od_s[0]], axis=-1).astype(BF16)
    res = jnp.dot(mix, w_ref[...], preferred_element_type=F32)
    y_ref[0] = x_ref[0] + gate_ref[0] * res


def _out_proj(oa, ob, attn_outs, attn_lses, pc, pd, pool_buf0, w_pool, p_scale, t0, x, gate, w_out):
    B, T, D = x.shape
    tm = min(ROW_TILE, T)
    nd = len(C_CONFIGS)
    br = pl.BlockSpec((1, tm, D_BR), lambda b, i: (b, i, 0))
    parts = [pl.BlockSpec((1, dil, tm // dil, D_BR), lambda b, i: (b, 0, i, 0)) for _, dil in C_CONFIGS]
    return pl.pallas_call(
        functools.partial(_out_proj_kernel, tm=tm, t0=t0),
        grid=(B, T // tm),
        in_specs=[br, br] + parts + parts
                 + [pl.BlockSpec((1, tm, D_BR), lambda b, i: (b, i, 3)),
                    pl.BlockSpec((1, tm, D_W), lambda b, i: (b, i, 0)),
                    pl.BlockSpec((1, POOL_MAX, D_BR), lambda b, i: (b, 0, 0)),
                    pl.BlockSpec((D_BR, D_BR), lambda b, i: (0, 0)),
                    pl.BlockSpec((1, D_BR), lambda b, i: (0, 0)),
                    pl.BlockSpec((1, tm, D), lambda b, i: (b, i, 0)),
                    pl.BlockSpec((1, 1, D), lambda b, i: (b, 0, 0)),
                    pl.BlockSpec((D, D), lambda b, i: (0, 0))],
        out_specs=pl.BlockSpec((1, tm, D), lambda b, i: (b, i, 0)),
        out_shape=jax.ShapeDtypeStruct((B, T, D), F32),
        scratch_shapes=[pltpu.VMEM((1, tm, D_BR), F32)] * 2
                       + [pltpu.VMEM((nd, D_BR // LANES, tm, LANES), F32)] * 2
                       + [pltpu.VMEM((POOL_MAX, D_BR), F32)],
        compiler_params=_cparams("parallel", "arbitrary"),
        name="out_proj_residual",
    )(oa, ob, *attn_outs, *attn_lses, pc, pd, pool_buf0, w_pool, p_scale, x, gate, w_out)


def _hgrn2_gates(pa, log_lb, log1m_lb):
    q = _silu(pa[:, 0:D_BR])
    fl = pa[:, D_BR:2 * D_BR]
    v = pa[:, 2 * D_BR:3 * D_BR]
    ls_pos, ls_neg = _log_sigmoid_pair(fl)
    logf = _logaddexp(ls_pos, log_lb + ls_neg)
    k = jnp.exp(log1m_lb + ls_neg)
    return q, logf, k, v


def _hgrn2_seq_kernel(pa_ref, s0_ref, loglb_ref, log1mlb_ref, og_ref, o_ref, sfin_ref,
                      q_s, k_s, v_s, g_s, o_s, st_s, *, tb, chunk, per_it):
    step = pl.program_id(1)
    hpl = LANES // HD
    npair = NH // hpl
    rl = lax.broadcasted_iota(jnp.int32, (LANES, LANES), 0) // HD
    cl = lax.broadcasted_iota(jnp.int32, (LANES, LANES), 1) // HD
    same_head = rl == cl

    @pl.when(step == 0)
    def _():
        for p in range(npair):
            blocks = [jnp.concatenate([s0_ref[0, p * hpl + a] if a == b else jnp.zeros((HD, HD), F32)
                                       for b in range(hpl)], axis=1) for a in range(hpl)]
            st_s[p] = jnp.concatenate(blocks, axis=0)

    q, logf, k, v = _hgrn2_gates(pa_ref[0], loglb_ref[...], log1mlb_ref[...])
    gcum = _chunk_cumsum(logf, chunk)
    for p in range(npair):
        sl = slice(p * LANES, (p + 1) * LANES)
        q_s[p] = q[:, sl]
        k_s[p] = k[:, sl]
        v_s[p] = v[:, sl]
        g_s[p] = gcum[:, sl]

    rows = lax.broadcasted_iota(jnp.int32, (chunk, 1), 0)
    lane_head = lax.broadcasted_iota(jnp.int32, (chunk, LANES), 1) // HD

    def body(it, carry):
        units = [(j, p) for j in range(per_it) for p in range(npair)]
        starts = [pl.multiple_of((it * per_it + j) * chunk, chunk) for j in range(per_it)]
        ld = lambda ref: [ref[p, pl.ds(starts[j], chunk), :] for j, p in units]
        qc, kc, vc, gc = ld(q_s), ld(k_s), ld(v_s), ld(g_s)
        glast = [g[chunk - 1:chunk, :] for g in gc]
        upd = [jnp.where(same_head, _bdot_tn(v_, k_ * jnp.exp(gl - g)), 0.0)
               for v_, k_, gl, g in zip(vc, kc, glast, gc)]
        decay = [jnp.exp(gl) for gl in glast]
        qe = [q_ * jnp.exp(g) for q_, g in zip(qc, gc)]
        o_acc = [jnp.zeros((chunk, LANES), F32) for _ in units]
        for t in range(chunk):
            x = [jnp.exp(jnp.minimum(g[t:t + 1, :] - g, 0.0)) * k_ * q_[t:t + 1, :]
                 for g, k_, q_ in zip(gc, kc, qc)]
            sel = [(lane_head == a) & (rows <= t) for a in range(hpl)]
            att = []
            for xi in x:
                sums = [jnp.sum(jnp.where(m, xi, 0.0), axis=1, keepdims=True) for m in sel]
                a_ = sums[-1]
                for a in range(hpl - 2, -1, -1):
                    a_ = jnp.where(lane_head == a, sums[a], a_)
                att.append(a_)
            ot = [jnp.sum(a * v_, axis=0, keepdims=True) for a, v_ in zip(att, vc)]
            o_acc = [jnp.where(rows == t, r, o) for o, r in zip(o_acc, ot)]
        st = [st_s[p] for p in range(npair)]
        for i, (j, p) in enumerate(units):
            o_s[pl.ds(starts[j], chunk), p * LANES:(p + 1) * LANES] = o_acc[i] + _bdot_nt(qe[i], st[p])
            st[p] = st[p] * decay[i] + upd[i]
        for p in range(npair):
            st_s[p] = st[p]
        return carry

    lax.fori_loop(0, tb // (chunk * per_it), body, 0)

    o = _head_rms(o_s[...], og_ref[...])
    o_ref[0] = o * _silu(pa_ref[0][:, 3 * D_BR:4 * D_BR])

    @pl.when(step == pl.num_programs(1) - 1)
    def _():
        for h in range(NH):
            a = (h % hpl) * HD
            sfin_ref[0, h] = st_s[h // hpl][a:a + HD, a:a + HD]


def _hgrn2_seq(pa, s0_t, log_lb, log1m_lb, onorm_g):
    B, T, _ = pa.shape
    tb = min(MIX_TILE, T)
    vec = pl.BlockSpec((1, D_BR), lambda b, i: (0, 0))
    st = pl.BlockSpec((1, NH, HD, HD), lambda b, i: (b, 0, 0, 0))
    return pl.pallas_call(
        functools.partial(_hgrn2_seq_kernel, tb=tb, chunk=CHUNK_A, per_it=CHUNKS_PER_IT_A),
        grid=(B, T // tb),
        in_specs=[pl.BlockSpec((1, tb, A_W), lambda b, i: (b, i, 0)), st, vec, vec, vec],
        out_specs=[pl.BlockSpec((1, tb, D_BR), lambda b, i: (b, i, 0)), st],
        out_shape=[jax.ShapeDtypeStruct((B, T, D_BR), F32),
                   jax.ShapeDtypeStruct((B, NH, HD, HD), F32)],
        scratch_shapes=[pltpu.VMEM((D_BR // LANES, tb, LANES), F32)] * 4
                       + [pltpu.VMEM((tb, D_BR), F32), pltpu.VMEM((D_BR // LANES, LANES, LANES), F32)],
        compiler_params=_cparams("parallel", "arbitrary"),
        name="hgrn2_seq",
    )(pa, s0_t, log_lb, log1m_lb, onorm_g)


def _hgrn2_step_kernel(q_ref, f_ref, i_ref, g_ref, s_ref, loglb_ref, log1mlb_ref, og_ref, o_ref, snew_ref):
    q = _silu(q_ref[...])
    fl = f_ref[...]
    v = i_ref[...]
    ls_pos, ls_neg = _log_sigmoid_pair(fl)
    f = jnp.exp(_logaddexp(ls_pos, loglb_ref[...] + ls_neg))
    k = jnp.exp(log1mlb_ref[...] + ls_neg)
    o = jnp.zeros_like(v)
    for kk in range(HD):
        s_new = f[kk:kk + 1, :] * s_ref[kk] + k[kk:kk + 1, :] * v
        snew_ref[kk] = s_new
        o = o + q[kk:kk + 1, :] * s_new
    ms = jnp.mean(o * o, axis=0, keepdims=True)
    o_ref[...] = o * lax.rsqrt(ms + EPS) * og_ref[...] * _silu(g_ref[...])


def _col_params(x):
    return x.reshape(NH, HD, 1)


def _hgrn2_step(pa_t, state_t, layer, log_lb, log1m_lb, onorm_g):
    B = pa_t.shape[1]
    parts = pa_t.reshape(A_W // HD, HD, B)
    part = lambda j: pl.BlockSpec((None, HD, B), lambda h: (j * NH + h, 0, 0))
    col = pl.BlockSpec((None, HD, 1), lambda h: (h, 0, 0))
    return pl.pallas_call(
        _hgrn2_step_kernel,
        grid=(NH,),
        in_specs=[part(0), part(1), part(2), part(3),
                  pl.BlockSpec((None, None, HD, HD, B), lambda h: (layer, h, 0, 0, 0)), col, col, col],
        out_specs=[pl.BlockSpec((HD, B), lambda h: (h, 0)),
                   pl.BlockSpec((None, HD, HD, B), lambda h: (h, 0, 0, 0))],
        out_shape=[jax.ShapeDtypeStruct((D_BR, B), F32),
                   jax.ShapeDtypeStruct((NH, HD, HD, B), F32)],
        compiler_params=_cparams("parallel"),
        name="hgrn2_step",
    )(parts, parts, parts, parts, state_t, _col_params(log_lb), _col_params(log1m_lb), _col_params(onorm_g))


def _rwkv_prep(xs, prev, p):
    xm = xs + (prev - xs) * p["mu"]
    r = xm[:, 0:D_BR]
    k = xm[:, D_BR:2 * D_BR]
    v = xm[:, 2 * D_BR:3 * D_BR]
    xw = xm[:, 3 * D_BR:3 * D_BR + LORA]
    xa = xm[:, 3 * D_BR + LORA:3 * D_BR + 2 * LORA]
    w = -_softplus(-(p["w0"] + _bdot(jnp.tanh(xw), p["w2"]))) - 0.5
    logw = -jnp.exp(w)
    a = _sigmoid(p["a0"] + _bdot(xa, p["a2"]))
    kk = k * p["k_k"]
    kk = kk / jnp.maximum(jnp.sqrt(_seg_sum(kk * kk)), 1e-12)
    k = k * (1.0 + (a - 1.0) * p["k_a"])
    bonus = _seg_sum(r * k * p["r_k"]) * v
    return r, logw, k, v, -kk, kk * a, bonus


def _rwkv_post(y, bonus, g, p):
    mean = _seg_sum(y) * (1.0 / HD)
    yc = y - mean
    var = _seg_sum(yc * yc) * (1.0 / HD)
    yn = yc * lax.rsqrt(var + RWKV_GN_EPS) * p["gn_w"] + p["gn_b"]
    return (yn + bonus) * _silu(g)


def _unit_lower_inverse(m, m_t, n):
    nb = INV_BLOCK
    rows = lax.broadcasted_iota(jnp.int32, (nb, 1), 0)
    nm = len(m)
    per = n // nb
    group = lax.broadcasted_iota(jnp.int32, (nb, n), 1) // nb
    diag_t = []
    for q in range(nm):
        dq = m_t[q][0:nb, :]
        for b in range(1, per):
            dq = jnp.where(group == b, m_t[q][b * nb:(b + 1) * nb, :], dq)
        diag_t.append(dq)
    qpl = LANES // n
    tiles = [jnp.concatenate(diag_t[t * qpl:(t + 1) * qpl], axis=1) for t in range(nm // qpl)]
    lane = lax.broadcasted_iota(jnp.int32, (nb, LANES), 1)
    base = lane - lane % nb
    ds = [jnp.where(lane % nb == rows, 1.0, 0.0).astype(F32)] * len(tiles)
    for i in range(1, nb):
        coef = [jnp.take_along_axis(t, base + i, axis=1) for t in tiles]
        rws = [jnp.sum(cf * d, axis=0, keepdims=True) for cf, d in zip(coef, ds)]
        ds = [d + jnp.where(rows == i, rw, 0.0) for d, rw in zip(ds, rws)]
    blocks = [[ds[q // qpl][:, (q % qpl) * n + b * nb:(q % qpl) * n + (b + 1) * nb] for b in range(per)]
              for q in range(nm)]
    size = nb
    while size < n:
        pairs = [(q, j) for q in range(nm) for j in range(0, len(blocks[q]), 2)]
        inner = [_rw_inv(m[q][(j + 1) * size:(j + 2) * size, j * size:(j + 1) * size], blocks[q][j])
                 for q, j in pairs]
        cross = [_rw_inv(blocks[q][j + 1], t) for (q, j), t in zip(pairs, inner)]
        merged = [[] for _ in range(nm)]
        for (q, j), cr in zip(pairs, cross):
            top = jnp.concatenate([blocks[q][j], jnp.zeros((size, size), F32)], axis=1)
            bot = jnp.concatenate([cr, blocks[q][j + 1]], axis=1)
            merged[q].append(jnp.concatenate([top, bot], axis=0))
        blocks = merged
        size *= 2
    return [b[0] for b in blocks]


_rw_score = _bdot_nt
_rw_score_out = _bdot_nt
_rw_inv = _bdot
_rw_solve = _bdot
_rw_read = _bdot_nt
_rw_read_out = _bdot_nt
_rw_apply = _bdot
_rw_apply_out = _bdot
_rw_update = _bdot_tn
_rw_state = _bdot

_RWKV_PARAMS = ("mu", "w0", "w2", "a0", "a2", "k_k", "k_a", "r_k", "gn_w", "gn_b")


def _rwkv_seq_kernel(pb_ref, shift0_ref, s0_ref, *rest, tb, chunk, attn_rows):
    np_ = len(_RWKV_PARAMS)
    p = {n: r[...] for n, r in zip(_RWKV_PARAMS, rest[:np_])}
    n_ai, n_ao = 8, 3
    attn_in = rest[np_:np_ + n_ai]
    o_ref, sfin_ref, shift_ref = rest[np_ + n_ai:np_ + n_ai + 3]
    attn_out = rest[np_ + n_ai + 3:np_ + n_ai + 3 + n_ao]
    r_s, k_s, v_s, a_s, b_s, w_s, g_s, y_s, bon_s, st_s, carry_s = rest[np_ + n_ai + 3 + n_ao:]
    step = pl.program_id(1)
    _attn_step_kernel(*attn_in, *attn_out, bb=attn_rows)

    @pl.when(step == 0)
    def _():
        st_s[...] = s0_ref[0]
        carry_s[...] = shift0_ref[0]

    xs = pb_ref[0][:, 0:B_SHIFT_W]
    row = lax.broadcasted_iota(jnp.int32, (tb, 1), 0)
    prev = jnp.where(row == 0, carry_s[...], pltpu.roll(xs, 1, 0))
    carry_s[...] = xs[tb - 1:tb, :]
    r, logw, k, v, av, bv, bonus = _rwkv_prep(xs, prev, p)
    gcum = _chunk_cumsum(logw, chunk)
    bon_s[...] = bonus
    for h in range(NH):
        sl = slice(h * HD, (h + 1) * HD)
        r_s[h] = r[:, sl]
        k_s[h] = k[:, sl]
        v_s[h] = v[:, sl]
        a_s[h] = av[:, sl]
        b_s[h] = bv[:, sl]
        w_s[h] = logw[:, sl]
        g_s[h] = gcum[:, sl]

    ri = lax.broadcasted_iota(jnp.int32, (chunk, chunk), 0)
    ci = lax.broadcasted_iota(jnp.int32, (chunk, chunk), 1)
    strict = ci < ri
    incl = ci <= ri

    def run_chunks(chunks):
        units = [(c, h) for c in chunks for h in range(NH)]
        hs = range(len(units))
        ld = lambda ref: [ref[h, c * chunk:(c + 1) * chunk, :] for c, h in units]
        rc, kc, vc, ac, bc, wc, gc = (ld(x) for x in (r_s, k_s, v_s, a_s, b_s, w_s, g_s))
        glast = [g[chunk - 1:chunk, :] for g in gc]
        einv = [jnp.exp(-g) for g in gc]
        etail = [jnp.exp(gl - g) for gl, g in zip(glast, gc)]
        a_t = [a * jnp.exp(g - w) for a, g, w in zip(ac, gc, wc)]
        r_t = [r * jnp.exp(g) for r, g in zip(rc, gc)]
        b_t = [b * e for b, e in zip(bc, einv)]
        k_t = [k * e for k, e in zip(kc, einv)]
        b_h = [b * e for b, e in zip(bc, etail)]
        k_h = [k * e for k, e in zip(kc, etail)]
        m_ab = [jnp.where(strict, _rw_score(a_t[h], b_t[h]), 0.0) for h in hs]
        m_ab_t = [jnp.where(ri < ci, _rw_score(b_t[h], a_t[h]), 0.0) for h in hs]
        m_ak = [jnp.where(strict, _rw_score(a_t[h], k_t[h]), 0.0) for h in hs]
        n_rb = [jnp.where(incl, _rw_score_out(r_t[h], b_t[h]), 0.0) for h in hs]
        n_rk = [jnp.where(incl, _rw_score_out(r_t[h], k_t[h]), 0.0) for h in hs]
        inv = _unit_lower_inverse(m_ab, m_ab_t, chunk)
        x = [_rw_apply(m_ak[h], vc[h]) for h in hs]
        a_hat = [_rw_solve(inv[h], a_t[h]) for h in hs]
        u = [_rw_solve(inv[h], x[h]) for h in hs]
        r_hat = [r_t[h] + _rw_apply_out(n_rb[h], a_hat[h]) for h in hs]
        y0 = [_rw_apply_out(n_rb[h], u[h]) + _rw_apply_out(n_rk[h], vc[h]) for h in hs]
        w_kk = [_rw_update(a_hat[h], b_h[h]) for h in hs]
        z = [_rw_update(u[h], b_h[h]) + _rw_update(vc[h], k_h[h]) for h in hs]
        decay = [jnp.exp(gl) for gl in glast]
        s = [st_s[h] for h in range(NH)]
        for i, (c, h) in enumerate(units):
            y_s[c * chunk:(c + 1) * chunk, h * HD:(h + 1) * HD] = _rw_read_out(r_hat[i], s[h]) + y0[i]
            s[h] = s[h] * decay[i] + _rw_state(s[h], w_kk[i]) + z[i]
        for h in range(NH):
            st_s[h] = s[h]

    n_chunks = tb // chunk
    for c0 in range(0, n_chunks, CHUNKS_PER_PASS):
        run_chunks(range(c0, min(c0 + CHUNKS_PER_PASS, n_chunks)))

    o_ref[0] = _rwkv_post(y_s[...], bon_s[...], pb_ref[0][:, B_SHIFT_W:B_W], p)

    @pl.when(step == pl.num_programs(1) - 1)
    def _():
        sfin_ref[0] = st_s[...]
        shift_ref[0] = carry_s[...]


def _rwkv_param_specs(params, nargs):
    zero2 = (lambda b, i: (0, 0)) if nargs == 2 else (lambda i: (0, 0))
    return [pl.BlockSpec(x.shape, zero2) for x in params]


def _rwkv_seq(pb, shift0, s0, params, attn_args):
    B, T, _ = pb.shape
    tb = min(2 * MIX_TILE, T)
    nt = T // tb
    rows, a_in, a_in_specs, a_out_specs, a_out_shapes = _attn_step_operands(
        *attn_args, n_steps=B * nt, step_of=lambda b, i: b * nt + i)
    st = pl.BlockSpec((1, NH, HD, HD), lambda b, i: (b, 0, 0, 0))
    sh = pl.BlockSpec((1, 1, B_SHIFT_W), lambda b, i: (b, 0, 0))
    res = pl.pallas_call(
        functools.partial(_rwkv_seq_kernel, tb=tb, chunk=CHUNK_B, attn_rows=rows),
        grid=(B, nt),
        in_specs=[pl.BlockSpec((1, tb, B_W), lambda b, i: (b, i, 0)), sh, st]
                 + _rwkv_param_specs(params, 2) + a_in_specs,
        out_specs=[pl.BlockSpec((1, tb, D_BR), lambda b, i: (b, i, 0)), st, sh] + a_out_specs,
        out_shape=[jax.ShapeDtypeStruct((B, T, D_BR), F32),
                   jax.ShapeDtypeStruct((B, NH, HD, HD), F32),
                   jax.ShapeDtypeStruct((B, 1, B_SHIFT_W), F32)] + a_out_shapes,
        scratch_shapes=[pltpu.VMEM((NH, tb, HD), F32)] * 7
                       + [pltpu.VMEM((tb, D_BR), F32), pltpu.VMEM((tb, D_BR), F32),
                          pltpu.VMEM((NH, HD, HD), F32), pltpu.VMEM((1, B_SHIFT_W), F32)],
        compiler_params=_cparams("parallel", "arbitrary"),
        name="rwkv7_seq",
    )(pb, shift0, s0, *params, *a_in)
    return res[0], res[1], res[2], [a.reshape(-1, D_BR) for a in res[3:]]


def _rwkv_step_kernel(r_ref, k_ref, v_ref, lr_ref, g_ref, pr_ref, pk_ref, pv_ref, plr_ref,
                      mr_ref, mk_ref, mv_ref, mlr_ref, w0_ref, w2_ref, a0_ref, a2_ref,
                      kk_ref, ka_ref, rk_ref, gw_ref, gb_ref, s_ref, o_ref, snew_ref, y_s):
    lerp = lambda x, prev, mu: x[...] + (prev[...] - x[...]) * mu[...]
    r = lerp(r_ref, pr_ref, mr_ref)
    k = lerp(k_ref, pk_ref, mk_ref)
    v = lerp(v_ref, pv_ref, mv_ref)
    lr = lerp(lr_ref, plr_ref, mlr_ref)
    w = -_softplus(-(w0_ref[...] + _bdot(w2_ref[...], jnp.tanh(lr[0:LORA])))) - 0.5
    dec = jnp.exp(-jnp.exp(w))
    a = _sigmoid(a0_ref[...] + _bdot(a2_ref[...], lr[LORA:2 * LORA]))
    kk = k * kk_ref[...]
    kk = kk / jnp.maximum(jnp.sqrt(jnp.sum(kk * kk, axis=0, keepdims=True)), 1e-12)
    k = k * (1.0 + (a - 1.0) * ka_ref[...])
    bonus = jnp.sum(r * k * rk_ref[...], axis=0, keepdims=True) * v
    av = -kk
    bv = kk * a
    for i in range(HD):
        s = s_ref[i]
        sa = jnp.sum(s * av, axis=0, keepdims=True)
        s_new = s * dec + sa * bv + v[i:i + 1, :] * k
        snew_ref[i] = s_new
        y_s[i:i + 1, :] = jnp.sum(s_new * r, axis=0, keepdims=True)
    y = y_s[...]
    mean = jnp.mean(y, axis=0, keepdims=True)
    yc = y - mean
    var = jnp.mean(yc * yc, axis=0, keepdims=True)
    yn = yc * lax.rsqrt(var + RWKV_GN_EPS) * gw_ref[...] + gb_ref[...]
    o_ref[...] = (yn + bonus) * _silu(g_ref[...])


def _rwkv_step(pb_t, shift_t, state_t, layer, params):
    B = pb_t.shape[1]
    p = dict(zip(_RWKV_PARAMS, params))
    parts = pb_t.reshape(B_W // HD, HD, B)
    prev = shift_t.reshape(DEPTH, B_SHIFT_W // HD, HD, B)
    mu = p["mu"].reshape(B_SHIFT_W // HD, HD, 1)
    lr_row = 3 * NH
    part = lambda j: pl.BlockSpec((None, HD, B), lambda h: (j * NH + h, 0, 0))
    pprev = lambda j: pl.BlockSpec((None, None, HD, B), lambda h: (layer, j * NH + h, 0, 0))
    pmu = lambda j: pl.BlockSpec((None, HD, 1), lambda h: (j * NH + h, 0, 0))
    col = pl.BlockSpec((None, HD, 1), lambda h: (h, 0, 0))
    lora = pl.BlockSpec((HD, LORA), lambda h: (h, 0))
    cols = [_col_params(p[n]) for n in ("k_k", "k_a", "r_k", "gn_w", "gn_b")]
    return pl.pallas_call(
        _rwkv_step_kernel,
        grid=(NH,),
        in_specs=[part(0), part(1), part(2),
                  pl.BlockSpec((None, HD, B), lambda h: (lr_row, 0, 0)),
                  pl.BlockSpec((None, HD, B), lambda h: (lr_row + 1 + h, 0, 0)),
                  pprev(0), pprev(1), pprev(2),
                  pl.BlockSpec((None, None, HD, B), lambda h: (layer, lr_row, 0, 0)),
                  pmu(0), pmu(1), pmu(2),
                  pl.BlockSpec((None, HD, 1), lambda h: (lr_row, 0, 0)),
                  col, lora, col, lora, col, col, col, col, col,
                  pl.BlockSpec((None, None, HD, HD, B), lambda h: (layer, h, 0, 0, 0))],
        out_specs=[pl.BlockSpec((HD, B), lambda h: (h, 0)),
                   pl.BlockSpec((None, HD, HD, B), lambda h: (h, 0, 0, 0))],
        out_shape=[jax.ShapeDtypeStruct((D_BR, B), F32),
                   jax.ShapeDtypeStruct((NH, HD, HD, B), F32)],
        scratch_shapes=[pltpu.VMEM((HD, B), F32)],
        compiler_params=_cparams("parallel"),
        name="rwkv7_step",
    )(parts, parts, parts, parts, parts, prev, prev, prev, prev, mu, mu, mu, mu,
      _col_params(p["w0"]), p["w2"].T, _col_params(p["a0"]), p["a2"].T, *cols, state_t)


def _qkv_prep_kernel(pc_ref, cos_ref, sin_ref, qg_ref, kg_ref, *rest, tm):
    nd = len(C_CONFIGS)
    outs = rest[:3 * nd]
    kc_ref = rest[3 * nd]
    scr = rest[3 * nd + 1:]
    pc = pc_ref[0]
    cos = cos_ref[...]
    sin = sin_ref[...]
    q = _rope(_head_rms(pc[:, 0:D_BR], qg_ref[...]), cos, sin) * (HD ** -0.5)
    k = _rope(_head_rms(pc[:, D_BR:2 * D_BR], kg_ref[...]), cos, sin)
    kc_ref[0] = k
    for a, x in enumerate((q, k, pc[:, 2 * D_BR:3 * D_BR])):
        for half in range(D_BR // LANES):
            scr[a][half] = x[:, half * LANES:(half + 1) * LANES]
    hpl = LANES // HD
    for ci, (_, dil) in enumerate(C_CONFIGS):
        for a in range(3):
            out = outs[3 * ci + a]
            for rho in range(dil):
                for half in range(D_BR // LANES):
                    rows = scr[a][half, pl.ds(rho, tm // dil, stride=dil), :]
                    for j in range(hpl):
                        out[0, rho, half * hpl + j] = rows[:, j * HD:(j + 1) * HD].astype(BF16)


def _swa_kernel(q_ref, kc_ref, vc_ref, kp_ref, vp_ref, o_ref, lse_ref, *, qt):
    i = pl.program_id(1)
    n = Q_BLOCK
    r = lax.broadcasted_iota(jnp.int32, (n, 2 * n), 0)
    c = lax.broadcasted_iota(jnp.int32, (n, 2 * n), 1)
    band = (c >= r) & (c <= r + n)
    first = band & (c >= jnp.where(i > 0, 0, n))
    units = [(u, h) for u in range(qt // n) for h in range(NH)]

    def window(cur_ref, prev_ref, u, h):
        if u == 0:
            return jnp.concatenate([prev_ref[0, h], cur_ref[0, h, 0:n, :]], axis=0)
        return cur_ref[0, h, (u - 1) * n:(u + 1) * n, :]

    s = [lax.dot_general(q_ref[0, h, u * n:(u + 1) * n, :], window(kc_ref, kp_ref, u, h),
                         (((1,), (1,)), ((), ())), preferred_element_type=F32) for u, h in units]
    s = [jnp.where(first if u == 0 else band, x, MASK_VALUE) for (u, h), x in zip(units, s)]
    m = [jnp.max(x, axis=1, keepdims=True) for x in s]
    p = [jnp.exp(x - mx) for x, mx in zip(s, m)]
    l = [jnp.sum(x, axis=1, keepdims=True) for x in p]
    o = [jnp.dot(x.astype(BF16), window(vc_ref, vp_ref, u, h), preferred_element_type=F32) / lx
         for (u, h), x, lx in zip(units, p, l)]
    lse = [jnp.broadcast_to(mx + jnp.log(lx), (n, HD)) for mx, lx in zip(m, l)]
    for u in range(qt // n):
        o_ref[0, u * n:(u + 1) * n, :] = jnp.concatenate(o[u * NH:(u + 1) * NH], axis=1)
        lse_ref[0, u * n:(u + 1) * n, :] = jnp.concatenate(lse[u * NH:(u + 1) * NH], axis=1)


def _swa(q, k, v):
    N, _, L, _ = q.shape
    n = Q_BLOCK
    qt = min(ATTN_TILE, L)
    cur = pl.BlockSpec((1, NH, qt, HD), lambda s, i: (s, 0, i, 0))
    prev = pl.BlockSpec((1, NH, n, HD), lambda s, i: (s, 0, jnp.maximum(i * (qt // n) - 1, 0), 0))
    out = pl.BlockSpec((1, qt, D_BR), lambda s, i: (s, i, 0))
    return pl.pallas_call(
        functools.partial(_swa_kernel, qt=qt),
        grid=(N, L // qt),
        in_specs=[cur, cur, cur, prev, prev],
        out_specs=[out, out],
        out_shape=[jax.ShapeDtypeStruct((N, L, D_BR), F32)] * 2,
        compiler_params=_cparams("parallel", "parallel"),
        name="sliding_window_attn",
    )(q, k, v, k, v)


def _attn_combine_kernel(*refs, tm):
    nd = len(C_CONFIGS)
    o_refs, lse_refs = refs[0:nd], refs[nd:2 * nd]
    g_ref, out_ref = refs[2 * nd], refs[2 * nd + 1]
    o_s, lse_s = refs[2 * nd + 2], refs[2 * nd + 3]
    os_, lses = [], []
    for ci, (_, dil) in enumerate(C_CONFIGS):
        if dil == 1:
            os_.append(o_refs[ci][0, 0])
            lses.append(lse_refs[ci][0, 0])
            continue
        nh = D_BR // LANES
        for rho in range(dil):
            for half in range(nh):
                ls = slice(half * LANES, (half + 1) * LANES)
                o_s[ci, half, pl.ds(rho, tm // dil, stride=dil), :] = o_refs[ci][0, rho][:, ls]
                lse_s[ci, half, pl.ds(rho, tm // dil, stride=dil), :] = lse_refs[ci][0, rho][:, ls]
        os_.append(jnp.concatenate([o_s[ci, half] for half in range(nh)], axis=1))
        lses.append(jnp.concatenate([lse_s[ci, half] for half in range(nh)], axis=1))
    top = functools.reduce(jnp.maximum, lses)
    ws = [jnp.exp(x - top) for x in lses]
    num = functools.reduce(lambda a, b: a + b, [w * o for w, o in zip(ws, os_)])
    den = functools.reduce(lambda a, b: a + b, ws)
    out_ref[0] = num / den * _silu(g_ref[0])


def _attn_seq(prep, B, T):
    assert T % (Q_BLOCK * max(d for _, d in C_CONFIGS)) == 0
    outs, lses = [], []
    for ci, (w, dil) in enumerate(C_CONFIGS):
        assert w == Q_BLOCK * dil
        q, k, v = (x.reshape(B * dil, NH, T // dil, HD) for x in prep[3 * ci:3 * ci + 3])
        o, lse = _swa(q, k, v)
        outs.append(o.reshape(B, dil, T // dil, D_BR))
        lses.append(lse.reshape(B, dil, T // dil, D_BR))
    return outs, lses


def _attn_step_kernel(pc_ref, cos_ref, sin_ref, qg_ref, kg_ref, mult_ref, kt_ref, vt_ref,
                      o_ref, knew_ref, vnew_ref, *, bb):
    pc = pc_ref[...]
    cos = cos_ref[...]
    sin = sin_ref[...]
    q = _rope(_head_rms(pc[:, 0:D_BR], qg_ref[...]), cos, sin) * (HD ** -0.5)
    k = _rope(_head_rms(pc[:, D_BR:2 * D_BR], kg_ref[...]), cos, sin)
    v = pc[:, 2 * D_BR:3 * D_BR]
    knew_ref[...] = k
    vnew_ref[...] = v
    mult = mult_ref[...]
    live = mult > 0.0
    n_self = float(len(C_CONFIGS))
    eye = _eye(HD)
    for b in range(bb):
        parts = []
        for h in range(NH):
            sl = slice(h * HD, (h + 1) * HD)
            qr, kr, vr = q[b:b + 1, sl], k[b:b + 1, sl], v[b:b + 1, sl]
            s_self = jnp.sum(qr * kr, axis=1, keepdims=True)
            s = jnp.sum(kt_ref[b, h] * _to_col(qr, eye), axis=0, keepdims=True)
            s = jnp.where(live, s, MASK_VALUE)
            m = jnp.maximum(jnp.max(s, axis=1, keepdims=True), s_self)
            p = jnp.exp(s - m) * mult
            p_self = n_self * jnp.exp(s_self - m)
            den = jnp.sum(p, axis=1, keepdims=True) + p_self
            acc = _to_row(jnp.sum(vt_ref[b, h] * p, axis=1, keepdims=True), eye) + p_self * vr
            parts.append(acc / den)
        o_ref[b:b + 1, :] = jnp.concatenate(parts, axis=1)
    o_ref[...] = o_ref[...] * _silu(pc[:, 3 * D_BR:4 * D_BR])


def _attn_step_operands(pc, cos, sin, qg, kg, cache_kt, cache_vt, layer, n_steps, step_of):
    B = pc.shape[0]
    P = cache_kt.shape[-1]
    bb = B // n_steps
    assert bb * n_steps == B
    dist = P - jnp.arange(P)
    mult = jnp.zeros((P,), F32)
    for w, dil in C_CONFIGS:
        mult = mult + ((dist <= w) & (dist % dil == 0)).astype(F32)
    fixed = lambda shape: pl.BlockSpec(shape, lambda *g: (0,) * len(shape))
    row = pl.BlockSpec((None, bb, D_BR), lambda *g: (step_of(*g), 0, 0))
    cache = pl.BlockSpec((None, bb, NH, HD, P), lambda *g: (layer, step_of(*g), 0, 0, 0))
    inputs = (pc.reshape(n_steps, bb, C_W), cos, sin, qg, kg, mult.reshape(1, P), cache_kt, cache_vt)
    in_specs = [pl.BlockSpec((None, bb, C_W), lambda *g: (step_of(*g), 0, 0)),
                fixed((1, D_BR)), fixed((1, D_BR)), fixed((1, D_BR)), fixed((1, D_BR)), fixed((1, P)),
                cache, cache]
    out_shapes = [jax.ShapeDtypeStruct((n_steps, bb, D_BR), F32)] * 3
    return bb, inputs, in_specs, [row, row, row], out_shapes


def _pool_mix(pooled, u, g, w_ref, scale_ref):
    mixed = _bdot(pooled - u, w_ref[...]) * scale_ref[...]
    return mixed * _silu(g)


def _pool_select(sums, cnts):
    lane = lax.broadcasted_iota(jnp.int32, sums[0].shape, 1) // (D_BR // len(POOL_WINDOWS))
    out = sums[-1] / cnts[-1]
    for gi in range(len(POOL_WINDOWS) - 2, -1, -1):
        out = jnp.where(lane == gi, sums[gi] / cnts[gi], out)
    return out


def _pool_seq_kernel(pd_ref, buf0_ref, w_ref, scale_ref, o_ref, carry_s, *, tb, t0):
    step = pl.program_id(1)

    @pl.when(step == 0)
    def _():
        carry_s[...] = buf0_ref[0]

    pd = pd_ref[0]
    u = pd[:, 0:D_BR]
    ext = jnp.concatenate([carry_s[...], u], axis=0)
    carry_s[...] = u[tb - POOL_MAX:tb, :]
    pos = (t0 + step * tb + lax.broadcasted_iota(jnp.int32, (tb, 1), 0)).astype(F32)
    sums, cnts = [], []
    run = ext
    shift = 1
    for w in POOL_WINDOWS:
        while shift < w:
            run = run + pltpu.roll(run, shift, 0)
            shift *= 2
        sums.append(run[POOL_MAX:, :])
        cnts.append(jnp.minimum(float(w), pos + 1.0))
    o_ref[0] = _pool_mix(_pool_select(sums, cnts), u, pd[:, D_BR:2 * D_BR], w_ref, scale_ref)


def _pool_step_kernel(pd_ref, buf_ref, w_ref, scale_ref, o_ref, *, t0):
    pd = pd_ref[...]
    u = pd[:, 0:D_BR]
    nbuf = POOL_MAX - 1
    sums, cnts = [], []
    run = u
    taken = 1
    for w in POOL_WINDOWS:
        while taken < w:
            run = run + buf_ref[nbuf - taken]
            taken += 1
        sums.append(run)
        cnts.append(float(min(w, t0 + 1)))
    o_ref[...] = _pool_mix(_pool_select(sums, cnts), u, pd[:, D_BR:2 * D_BR], w_ref, scale_ref)


def _pool_step(pd, buf, layer, w_bd, scale, t0):
    B = pd.shape[0]
    bb = min(POOL_STEP_ROWS, B)
    return pl.pallas_call(
        functools.partial(_pool_step_kernel, t0=t0),
        grid=(B // bb,),
        in_specs=[pl.BlockSpec((bb, D_W), lambda i: (i, 0)),
                  pl.BlockSpec((None, POOL_MAX - 1, bb, D_BR), lambda i: (layer, 0, i, 0)),
                  pl.BlockSpec((D_BR, D_BR), lambda i: (0, 0)),
                  pl.BlockSpec((1, D_BR), lambda i: (0, 0))],
        out_specs=pl.BlockSpec((bb, D_BR), lambda i: (i, 0)),
        out_shape=jax.ShapeDtypeStruct((B, D_BR), F32),
        compiler_params=_cparams("parallel"),
        name="pool_step",
    )(pd, buf, w_bd, scale)


def _rope_tables(pos):
    half = HD // 2
    inv_freq = jnp.float32(ROPE_THETA) ** (-jnp.arange(half, dtype=F32) / half)
    ang = pos.astype(F32)[:, None] * inv_freq[None, :]
    cos = jnp.cos(ang)
    sin = jnp.sin(ang)
    cos_t = jnp.tile(jnp.concatenate([cos, cos], axis=-1), (1, NH))
    sin_t = jnp.tile(jnp.concatenate([-sin, sin], axis=-1), (1, NH))
    return cos_t, sin_t


def _block_diag(w):
    g, c, d = w.shape
    out = jnp.zeros((g * c, g * d), w.dtype)
    for i in range(g):
        out = out.at[i * c:(i + 1) * c, i * d:(i + 1) * d].set(w[i])
    return out


def _layer_params(l, W):
    row = lambda x: x[l].reshape(1, -1)
    w_in = W["w_in"][l].astype(BF16)
    w_in_t = jnp.swapaxes(W["w_in"], 1, 2)[l].astype(BF16)
    cuts = (0, A_W, A_W + B_W, A_W + B_W + C_W, A_W + B_W + C_W + D_W)
    return dict(
        norm_g=W["norm_g"][l],
        w_parts=[w_in[:, cuts[i]:cuts[i + 1]] for i in range(4)],
        w_parts_t=[w_in_t[cuts[i]:cuts[i + 1], :] for i in range(2)],
        w_out=W["w_out"][l].astype(BF16),
        onorm_g=row(W["a_onorm_g"]),
        rwkv=[row(W["b_mu"]), row(W["b_w0"]), W["b_w2"][l], row(W["b_a0"]), W["b_a2"][l],
              row(W["b_k_k"]), row(W["b_k_a"]), row(W["b_r_k"]), row(W["b_gn_w"]), row(W["b_gn_b"])],
        qg=jnp.tile(W["c_qnorm_g"][l], NH).reshape(1, D_BR),
        kg=jnp.tile(W["c_knorm_g"][l], NH).reshape(1, D_BR),
        w_pool=_block_diag(W["d_w_pool"][l]),
        p_scale=row(W["d_scale"]),
    )


def _run_groups(x, xs, mods, mods_s, layers, log_lb, log1m_lb, sA, sBw, sBs, ck, cv, cd):
    B, T, _ = x.shape
    Bs = xs.shape[0]
    assert xs.shape[1] == 1 and ck.shape[2] == PAST_LEN == C_WIN_MAX
    cos, sin = _rope_tables(jnp.arange(T))
    cos_s, sin_s = _rope_tables(jnp.full((1,), PAST_LEN))
    c_keep = min(C_WIN_MAX, T)
    d_keep = min(POOL_MAX - 1, T)
    zero_state = jnp.zeros((B, NH, HD, HD), F32)
    zero_shift = jnp.zeros((B, 1, B_SHIFT_W), F32)
    zero_pool = jnp.zeros((B, POOL_MAX, D_BR), F32)
    ck_t = jnp.transpose(ck, (0, 1, 3, 4, 2))
    cv_t = jnp.transpose(cv, (0, 1, 3, 4, 2))
    sA_t = jnp.transpose(sA, (0, 2, 3, 4, 1))
    sBw_t = jnp.transpose(sBw, (0, 2, 3, 4, 1))
    sBs_t = jnp.transpose(sBs, (0, 2, 1))
    cd_t = jnp.transpose(cd, (0, 2, 1, 3))
    xr = xs.reshape(Bs, D_MODEL)
    pA, pBw, pBs, pK, pV, pD = [], [], [], [], [], []
    nA, nBw, nBs, nK, nV, nD = [], [], [], [], [], []
    for l, L in enumerate(layers):
        shift, scale, gate_s = (mods_s[l][:, i * D_MODEL:(i + 1) * D_MODEL] for i in range(3))
        pa_t, pb_t, pc_s, pd_s = _in_proj_step(xr, shift, scale, L["norm_g"], *L["w_parts_t"], *L["w_parts"][2:])
        shift, scale, gate = (mods[l][:, None, i * D_MODEL:(i + 1) * D_MODEL] for i in range(3))
        (pa, pb, pc, pd), prep = _in_proj(x, shift, scale, L["norm_g"], L["w_parts"],
                                          cos, sin, L["qg"], L["kg"])
        k_rot = prep[-1]
        oa, sa_t = _hgrn2_seq(pa, zero_state, log_lb[l], log1m_lb[l], L["onorm_g"])
        ob, sb, sbs, (oc_s, k_new, v_new) = _rwkv_seq(
            pb, zero_shift, zero_state, L["rwkv"], (pc_s, cos_s, sin_s, L["qg"], L["kg"], ck_t, cv_t, l))
        attn_outs, attn_lses = _attn_seq(prep, B, T)
        x = _out_proj(oa, ob, attn_outs, attn_lses, pc, pd, zero_pool, L["w_pool"], L["p_scale"], 0,
                      x, gate, L["w_out"])
        pA.append(jnp.swapaxes(sa_t, -1, -2))
        pBw.append(sb)
        pBs.append(sbs[:, 0])
        pK.append(k_rot[:, T - c_keep:].reshape(B, c_keep, NH, HD))
        pV.append(pc[:, T - c_keep:, 2 * D_BR:3 * D_BR].reshape(B, c_keep, NH, HD))
        pD.append(pd[:, T - d_keep:, 0:D_BR])
        oa_t, sa = _hgrn2_step(pa_t, sA_t, l, log_lb[l], log1m_lb[l], L["onorm_g"])
        ob_t, sbw = _rwkv_step(pb_t, sBs_t, sBw_t, l, L["rwkv"])
        od_s = _pool_step(pd_s, cd_t, l, L["w_pool"], L["p_scale"], PAST_LEN)
        xr = _out_proj_step(oa_t, ob_t, oc_s, od_s, xr, gate_s, L["w_out"])
        nA.append(sa)
        nBw.append(sbw)
        nBs.append(pb_t[0:B_SHIFT_W])
        nK.append(k_new.reshape(Bs, 1, NH, HD))
        nV.append(v_new.reshape(Bs, 1, NH, HD))
        nD.append(pd_s[:, None, 0:D_BR])
    back = lambda a: jnp.transpose(jnp.stack(a), (0, 4, 1, 2, 3))
    prompt = (x,) + tuple(jnp.stack(a) for a in (pA, pBw, pBs, pK, pV, pD))
    sample = (xr.reshape(Bs, 1, D_MODEL), back(nA), back(nBw), jnp.transpose(jnp.stack(nBs), (0, 2, 1)),
              jnp.stack(nK), jnp.stack(nV), jnp.stack(nD))
    return prompt, sample


def kernel(x_prompt, x_sample, state_A, state_B_wkv, state_B_shift, cache_C_k, cache_C_v, cache_D_pool, c_prompt, c_sample, ada_w, ada_b, norm_g, w_in, w_out, a_lb_logits, a_onorm_g, b_mu, b_w0, b_w2, b_a0, b_a2, b_k_k, b_k_a, b_r_k, b_gn_w, b_gn_b, c_qnorm_g, c_knorm_g, d_w_pool, d_scale):
    W = dict(norm_g=norm_g, w_in=w_in, w_out=w_out, a_onorm_g=a_onorm_g, b_mu=b_mu, b_w0=b_w0,
             b_w2=b_w2, b_a0=b_a0, b_a2=b_a2, b_k_k=b_k_k, b_k_a=b_k_a, b_r_k=b_r_k,
             b_gn_w=b_gn_w, b_gn_b=b_gn_b, c_qnorm_g=c_qnorm_g, c_knorm_g=c_knorm_g,
             d_w_pool=d_w_pool, d_scale=d_scale)
    layers = [_layer_params(l, W) for l in range(DEPTH)]
    lb_sm = jax.nn.softmax(a_lb_logits.astype(F32), axis=0)
    lower = jnp.cumsum(lb_sm, axis=0) - lb_sm[0:1]
    log_lb = jnp.log(jnp.maximum(lower, LB_FLOOR))[:, None, :]
    log1m_lb = jnp.log1p(-lower)[:, None, :]

    nb = x_prompt.shape[0]
    mods = _modulation(jnp.concatenate([c_prompt, c_sample], axis=0), ada_w, ada_b)
    (y_p, pA, pBw, pBs, pK, pV, pD), (y_s, sA, sBw, sBs, sK, sV, sD) = _run_groups(
        x_prompt, x_sample, mods[:, :nb], mods[:, nb:], layers, log_lb, log1m_lb,
        state_A, state_B_wkv, state_B_shift, cache_C_k, cache_C_v, cache_D_pool)
    return (y_p, y_s, pA, sA, pBw, sBw, pBs, sBs, pK, sK, pV, sV, pD, sD)
```
